```python
import math
import jax
import jax.numpy as jnp
from jax import lax
import numpy as np

D_MODEL = 2048
BATCH = 4
SEQ = 2048
DEPTH = 1
DEC_BATCH = 128
DEC_SEQ = 4
PAST_LEN = 16384
PAGE_SIZE = 128

MIX_A = D_MODEL // 2
MIX_B = D_MODEL - MIX_A
A_HEAD = 64
A_HEADS = MIX_A // A_HEAD
A_DECAY_RANK = 64
A_ICL_RANK = 64
A_GATE_RANK = 128
A_GN_EPS = A_HEAD * 1e-5
B_HEADS = 4
B_DV = MIX_B // B_HEADS
B_DQK = B_DV // 2
B_CONV = 4
B_CHUNK = 64
PEER_HEADS = 8
PEER_NKEYS = 128
PEER_NEXP = PEER_NKEYS * PEER_NKEYS
PEER_DKEY = 256
PEER_TOPK = 16
PEER_BLOCK = 128
NORM_EPS = 1e-6

A_SHIFT_COLS = 3 * MIX_A + A_DECAY_RANK + A_ICL_RANK + A_GATE_RANK
B_QK_COLS = 2 * B_HEADS * B_DQK
B_REST_COLS = 2 * MIX_B + 2 * B_HEADS
N_IN = A_SHIFT_COLS + B_QK_COLS + B_REST_COLS
A_SPLITS = (MIX_A, 2 * MIX_A, 3 * MIX_A, 3 * MIX_A + A_DECAY_RANK, 3 * MIX_A + A_DECAY_RANK + A_ICL_RANK)
B_SPLITS = (MIX_B, 2 * MIX_B, 2 * MIX_B + B_HEADS)

LAYER_PARAMS = ('w_ada', 'b_ada', 'norm_mix_g', 'w_in', 'a_mu', 'a_w0', 'a_w2', 'a_a0', 'a_a2', 'a_g2',
                'a_kk', 'a_ka', 'a_rk', 'a_lnw', 'a_lnb', 'b_conv_w', 'b_conv_b', 'b_ig_bias', 'b_fg_bias',
                'b_norm_g', 'w_out', 'norm_ffn_g', 'peer_wq', 'peer_keys', 'peer_u', 'peer_v')

kernel_name = 'hybrid_rwkv7_mlstm_peer_step'


def rmsnorm(x, g):
    xf = x.astype(jnp.float32)
    y = xf * lax.rsqrt(jnp.mean(xf * xf, axis=-1, keepdims=True) + NORM_EPS)
    return (y * g).astype(x.dtype)


def rwkv7_mix(z, shift_buf, wkv0, p):
    f32 = jnp.float32
    bsz, t_len, _ = z.shape
    prev = jnp.concatenate([shift_buf[:, None, :].astype(z.dtype), z[:, :-1]], axis=1)
    zs = z + (prev - z) * p['a_mu']
    r, k, v, xw, xa, xg = jnp.split(zs.astype(f32), A_SPLITS, axis=-1)
    w_log = -jax.nn.softplus(-(p['a_w0'] + jnp.tanh(xw) @ p['a_w2'])) - 0.5
    decay = jnp.exp(-jnp.exp(w_log))
    a = jax.nn.sigmoid(p['a_a0'] + xa @ p['a_a2'])
    g = jax.nn.sigmoid(xg) @ p['a_g2']
    heads = lambda t: t.reshape(bsz, t_len, A_HEADS, A_HEAD)
    kk = heads(k * p['a_kk'])
    kk = kk / jnp.maximum(jnp.linalg.norm(kk, axis=-1, keepdims=True), 1e-12)
    k = k * (1.0 + (a - 1.0) * p['a_ka'])
    rh, kh, vh, dh, ah = heads(r), heads(k), heads(v), heads(decay), heads(a)

    def step(S, inp):
        r_t, d_t, k_t, v_t, kk_t, a_t = inp
        sa = jnp.einsum('bhij,bhj->bhi', S, -kk_t)
        S = (S * d_t[:, :, None, :] + sa[..., None] * (kk_t * a_t)[:, :, None, :]
             + v_t[..., None] * k_t[:, :, None, :])
        return S, jnp.einsum('bhij,bhj->bhi', S, r_t)

    seq_first = lambda t: jnp.swapaxes(t, 0, 1)
    S_T, y = lax.scan(step, wkv0.astype(f32), tuple(seq_first(t) for t in (rh, dh, kh, vh, kk, ah)))
    y = seq_first(y)
    mu = jnp.mean(y, axis=-1, keepdims=True)
    var = jnp.mean(jnp.square(y - mu), axis=-1, keepdims=True)
    y = ((y - mu) * lax.rsqrt(var + A_GN_EPS)).reshape(bsz, t_len, MIX_A) * p['a_lnw'] + p['a_lnb']
    bonus = jnp.sum(rh * kh * p['a_rk'], axis=-1, keepdims=True) * vh
    y = (y + bonus.reshape(bsz, t_len, MIX_A)) * g
    return y, z[:, -1], S_T


def mlstm_chunkwise(q, k, v, ig, fg, C0, n0, m0):
    bsz, t_len = q.shape[:2]
    L = math.gcd(t_len, B_CHUNK)
    nc = t_len // L
    mask = jnp.tril(jnp.ones((L, L), dtype=bool))

    def chunks(t):
        t = t.reshape((bsz, nc, L) + t.shape[2:])
        return jnp.moveaxis(jnp.moveaxis(t, 1, 0), 2, 3)

    def step(carry, inp):
        C, n, m = carry
        qc, kc, vc, ic, fc = inp
        b = jnp.cumsum(jax.nn.log_sigmoid(fc), axis=-1)
        inter = b + m[..., None]
        logD = jnp.where(mask, b[..., :, None] - b[..., None, :] + ic[..., None, :], -jnp.inf)
        m_t = jnp.maximum(inter, jnp.max(logD, axis=-1))
        S = jnp.einsum('bhtd,bhsd->bhts', qc, kc) * jnp.exp(logD - m_t[..., None])
        w_inter = jnp.exp(inter - m_t)
        num = S @ vc + w_inter[..., None] * jnp.einsum('bhtd,bhdv->bhtv', qc, C)
        den = jnp.sum(S, axis=-1) + w_inter * jnp.einsum('bhtd,bhd->bht', qc, n)
        h = num / jnp.maximum(jnp.abs(den), jnp.exp(-m_t))[..., None]
        m_new = m_t[..., -1]
        ws = jnp.exp(b[..., -1:] - b + ic - m_new[..., None])
        dC = jnp.exp(b[..., -1] + m - m_new)
        C_new = dC[..., None, None] * C + jnp.einsum('bhs,bhsd,bhsv->bhdv', ws, kc, vc)
        n_new = dC[..., None] * n + jnp.einsum('bhs,bhsd->bhd', ws, kc)
        return (C_new, n_new, m_new), h

    (C, n, m), h = lax.scan(step, (C0, n0, m0), tuple(chunks(t) for t in (q, k, v, ig, fg)))
    h = jnp.transpose(h, (1, 0, 3, 2, 4)).reshape(bsz, t_len, B_HEADS, B_DV)
    return h, C, n, m


def mlstm_mix(z_qk, z_rest, conv_buf, C0, n0, m0, p):
    f32 = jnp.float32
    bsz, t_len, _ = z_qk.shape
    zp = jnp.concatenate([conv_buf.astype(z_qk.dtype), z_qk], axis=1)
    qk = lax.conv_general_dilated(zp, p['b_conv_w'][:, None, :].astype(zp.dtype), (1,), 'VALID',
                                  dimension_numbers=('NWC', 'WIO', 'NWC'),
                                  feature_group_count=B_QK_COLS) + p['b_conv_b']
    qk = jax.nn.silu(qk.astype(f32))
    q, k = jnp.split(qk, 2, axis=-1)
    v, o, ig, fg = jnp.split(z_rest.astype(f32), B_SPLITS, axis=-1)
    q = q.reshape(bsz, t_len, B_HEADS, B_DQK)
    k = k.reshape(bsz, t_len, B_HEADS, B_DQK) * (B_DQK ** -0.5)
    v = v.reshape(bsz, t_len, B_HEADS, B_DV)
    h, C, n, m = mlstm_chunkwise(q, k, v, ig + p['b_ig_bias'], fg + p['b_fg_bias'],
                                 C0.astype(f32), n0.astype(f32), m0.astype(f32))
    h = h * lax.rsqrt(jnp.mean(h * h, axis=-1, keepdims=True) + NORM_EPS)
    h = h.reshape(bsz, t_len, MIX_B) * p['b_norm_g'] * jax.nn.sigmoid(o)
    return h, zp[:, zp.shape[1] - (B_CONV - 1):], C, n, m


def peer_ffn(h, w_q, keys, u, v):
    bsz, t_len, dm = h.shape
    x = h.reshape(bsz * t_len, dm)
    ntok = x.shape[0]
    q = (x @ w_q).astype(jnp.float32).reshape(ntok, PEER_HEADS, 2, PEER_DKEY // 2)
    s = jnp.einsum('nhpd,phkd->nhpk', q, keys.astype(jnp.float32))
    s_top, i_top = lax.top_k(s, PEER_TOPK)
    cand_s = (s_top[:, :, 0, :, None] + s_top[:, :, 1, None, :]).reshape(ntok, PEER_HEADS, PEER_TOPK * PEER_TOPK)
    cand_i = (i_top[:, :, 0, :, None] * PEER_NKEYS + i_top[:, :, 1, None, :]).reshape(ntok, PEER_HEADS, PEER_TOPK * PEER_TOPK)
    best_s, pos = lax.top_k(cand_s, PEER_TOPK)
    idx = jnp.take_along_axis(cand_i, pos, axis=-1).reshape(ntok, PEER_HEADS * PEER_TOPK)
    gate = jax.nn.softmax(best_s, axis=-1).reshape(ntok, PEER_HEADS * PEER_TOPK).astype(x.dtype)
    nb = -(-ntok // PEER_BLOCK)
    pad = nb * PEER_BLOCK - ntok
    xb = jnp.pad(x, ((0, pad), (0, 0))).reshape(nb, PEER_BLOCK, dm)
    ib = jnp.pad(idx, ((0, pad), (0, 0))).reshape(nb, PEER_BLOCK, -1)
    gb = jnp.pad(gate, ((0, pad), (0, 0))).reshape(nb, PEER_BLOCK, -1)

    def block(args):
        xt, it, gt = args
        ue = jnp.take(u, it, axis=0)
        act = jax.nn.gelu(jnp.einsum('nd,ned->ne', xt, ue), approximate=False) * gt
        ve = jnp.take(v, it, axis=0)
        return jnp.einsum('ne,ned->nd', act, ve)

    out = lax.map(block, (xb, ib, gb)).reshape(nb * PEER_BLOCK, dm)[:ntok]
    return out.reshape(bsz, t_len, dm)


def trunk(x, c, states, P):
    shift_s, wkv_s, conv_s, C_s, n_s, m_s = states
    outs = ([], [], [], [], [], [])
    for l in range(DEPTH):
        p = {name: P[name][l] for name in LAYER_PARAMS}
        mod = (jax.nn.silu(c) @ p['w_ada'] + p['b_ada'])[:, None, :]
        sh_a, sc_a, g_a, sh_c, sc_c, g_c = jnp.split(mod, 6, axis=-1)
        h = rmsnorm(x, p['norm_mix_g']) * (1.0 + sc_a) + sh_a
        z = h @ p['w_in']
        z_a = z[..., :A_SHIFT_COLS]
        z_qk = z[..., A_SHIFT_COLS:A_SHIFT_COLS + B_QK_COLS]
        z_rest = z[..., A_SHIFT_COLS + B_QK_COLS:]
        y_a, new_shift, new_wkv = rwkv7_mix(z_a, shift_s[l], wkv_s[l], p)
        y_b, new_conv, new_C, new_n, new_m = mlstm_mix(z_qk, z_rest, conv_s[l], C_s[l], n_s[l], m_s[l], p)
        y_mix = jnp.concatenate([y_a, y_b], axis=-1).astype(x.dtype) @ p['w_out']
        x = x + g_a * y_mix
        h = rmsnorm(x, p['norm_ffn_g']) * (1.0 + sc_c) + sh_c
        x = x + g_c * peer_ffn(h, p['peer_wq'], p['peer_keys'], p['peer_u'], p['peer_v'])
        for lst, val in zip(outs, (new_shift, new_wkv, new_conv, new_C, new_n, new_m)):
            lst.append(val.astype(x.dtype))
    mod = (jax.nn.silu(c) @ P['w_ada_final'] + P['b_ada_final'])[:, None, :]
    shift, scale = jnp.split(mod, 2, axis=-1)
    y = rmsnorm(x, P['norm_final_g']) * (1.0 + scale) + shift
    return y, tuple(jnp.stack(o, axis=0) for o in outs)


def setup_inputs(seed: int = 0) -> dict:
    key = jax.random.key(seed)
    ks = iter(jax.random.split(key, 48))
    f32 = jnp.float32
    nrm = lambda shape, s=1.0: s * jax.random.normal(next(ks), shape, f32)
    uni = lambda shape, lo, hi: jax.random.uniform(next(ks), shape, f32, lo, hi)
    lin = lambda lo, hi, n: jnp.broadcast_to(jnp.linspace(lo, hi, n, dtype=f32), (DEPTH, n))
    D = D_MODEL
    return {
        'x_prompt': nrm((BATCH, SEQ, D)),
        'x_sample': nrm((DEC_BATCH, DEC_SEQ, D)),
        'c_prompt': nrm((BATCH, D)),
        'c_sample': nrm((DEC_BATCH, D)),
        'state_rwkv_shift': nrm((DEPTH, DEC_BATCH, A_SHIFT_COLS)),
        'state_rwkv_wkv': nrm((DEPTH, DEC_BATCH, A_HEADS, A_HEAD, A_HEAD), 0.3),
        'state_mlstm_conv': nrm((DEPTH, DEC_BATCH, B_CONV - 1, B_QK_COLS)),
        'state_mlstm_C': nrm((DEPTH, DEC_BATCH, B_HEADS, B_DQK, B_DV), 0.3),
        'state_mlstm_n': nrm((DEPTH, DEC_BATCH, B_HEADS, B_DQK), 0.3),
        'state_mlstm_m': nrm((DEPTH, DEC_BATCH, B_HEADS)),
        'w_ada': nrm((DEPTH, D, 6 * D), 0.5 * D ** -0.5),
        'b_ada': nrm((DEPTH, 6 * D), 0.02),
        'norm_mix_g': 1.0 + nrm((DEPTH, D), 0.02),
        'w_in': nrm((DEPTH, D, N_IN), D ** -0.5),
        'a_mu': uni((DEPTH, A_SHIFT_COLS), 0.0, 1.0),
        'a_w0': lin(-6.0, -1.0, MIX_A) + nrm((DEPTH, MIX_A), 0.1),
        'a_w2': nrm((DEPTH, A_DECAY_RANK, MIX_A), 0.5 * A_DECAY_RANK ** -0.5),
        'a_a0': nrm((DEPTH, MIX_A), 0.1),
        'a_a2': nrm((DEPTH, A_ICL_RANK, MIX_A), 0.5 * A_ICL_RANK ** -0.5),
        'a_g2': nrm((DEPTH, A_GATE_RANK, MIX_A), A_GATE_RANK ** -0.5),
        'a_kk': 0.85 + nrm((DEPTH, MIX_A), 0.02),
        'a_ka': 1.0 + nrm((DEPTH, MIX_A), 0.02),
        'a_rk': nrm((DEPTH, A_HEADS, A_HEAD), 0.1),
        'a_lnw': 1.0 + nrm((DEPTH, MIX_A), 0.02),
        'a_lnb': nrm((DEPTH, MIX_A), 0.02),
        'b_conv_w': nrm((DEPTH, B_CONV, B_QK_COLS), B_CONV ** -0.5),
        'b_conv_b': nrm((DEPTH, B_QK_COLS), 0.02),
        'b_ig_bias': nrm((DEPTH, B_HEADS), 0.1),
        'b_fg_bias': lin(3.0, 6.0, B_HEADS) + nrm((DEPTH, B_HEADS), 0.1),
        'b_norm_g': 1.0 + nrm((DEPTH, MIX_B), 0.02),
        'w_out': nrm((DEPTH, D, D), D ** -0.5),
        'norm_ffn_g': 1.0 + nrm((DEPTH, D), 0.02),
        'peer_wq': nrm((DEPTH, D, PEER_HEADS * PEER_DKEY), D ** -0.5),
        'peer_keys': nrm((DEPTH, 2, PEER_HEADS, PEER_NKEYS, PEER_DKEY // 2), (PEER_DKEY // 2) ** -0.5),
        'peer_u': nrm((DEPTH, PEER_NEXP, D), D ** -0.5),
        'peer_v': nrm((DEPTH, PEER_NEXP, D), 0.5),
        'w_ada_final': nrm((D, 2 * D), 0.5 * D ** -0.5),
        'b_ada_final': nrm((2 * D,), 0.02),
        'norm_final_g': 1.0 + nrm((D,), 0.02),
    }


def reference(x_prompt, x_sample, c_prompt, c_sample, state_rwkv_shift, state_rwkv_wkv, state_mlstm_conv,
              state_mlstm_C, state_mlstm_n, state_mlstm_m, w_ada, b_ada, norm_mix_g, w_in, a_mu, a_w0, a_w2,
              a_a0, a_a2, a_g2, a_kk, a_ka, a_rk, a_lnw, a_lnb, b_conv_w, b_conv_b, b_ig_bias, b_fg_bias,
              b_norm_g, w_out, norm_ffn_g, peer_wq, peer_keys, peer_u, peer_v, w_ada_final, b_ada_final,
              norm_final_g):
    P = dict(w_ada=w_ada, b_ada=b_ada, norm_mix_g=norm_mix_g, w_in=w_in, a_mu=a_mu, a_w0=a_w0, a_w2=a_w2,
             a_a0=a_a0, a_a2=a_a2, a_g2=a_g2, a_kk=a_kk, a_ka=a_ka, a_rk=a_rk, a_lnw=a_lnw, a_lnb=a_lnb,
             b_conv_w=b_conv_w, b_conv_b=b_conv_b, b_ig_bias=b_ig_bias, b_fg_bias=b_fg_bias,
             b_norm_g=b_norm_g, w_out=w_out, norm_ffn_g=norm_ffn_g, peer_wq=peer_wq, peer_keys=peer_keys,
             peer_u=peer_u, peer_v=peer_v, w_ada_final=w_ada_final, b_ada_final=b_ada_final,
             norm_final_g=norm_final_g)
    bp = x_prompt.shape[0]
    dt = x_prompt.dtype
    zero_states = (jnp.zeros((DEPTH, bp, A_SHIFT_COLS), dt),
                   jnp.zeros((DEPTH, bp, A_HEADS, A_HEAD, A_HEAD), dt),
                   jnp.zeros((DEPTH, bp, B_CONV - 1, B_QK_COLS), dt),
                   jnp.zeros((DEPTH, bp, B_HEADS, B_DQK, B_DV), dt),
                   jnp.zeros((DEPTH, bp, B_HEADS, B_DQK), dt),
                   jnp.zeros((DEPTH, bp, B_HEADS), dt))
    y_prompt, p_states = trunk(x_prompt, c_prompt, zero_states, P)
    y_sample, s_states = trunk(x_sample, c_sample,
                               (state_rwkv_shift, state_rwkv_wkv, state_mlstm_conv,
                                state_mlstm_C, state_mlstm_n, state_mlstm_m), P)
    p_shift, p_wkv, p_conv, p_C, p_n, p_m = p_states
    s_shift, s_wkv, s_conv, s_C, s_n, s_m = s_states
    return (y_prompt, y_sample, p_shift, p_wkv, p_conv, p_C, p_n, p_m,
            s_shift, s_wkv, s_conv, s_C, s_n, s_m)
```

```python
import functools

import jax
import jax.numpy as jnp
from jax import lax
from jax.experimental import pallas as pl
from jax.experimental.pallas import tpu as pltpu

f32 = jnp.float32
bf16 = jnp.bfloat16

D_MODEL = 2048
MIX_A = 1024
A_HEAD = 64
A_HEADS = 16
A_SHIFT_COLS = 3 * MIX_A + 64 + 64 + 128
A_GN_EPS = A_HEAD * 1e-5
B_QK_COLS = 1024
B_HEADS = 4
B_DV = 256
B_DQK = 128
B_CONV = 4
B_CHUNK = 64
Z_MAIN_COLS = A_SHIFT_COLS + B_QK_COLS + 2 * 1024
PEER_HEADS = 8
PEER_NKEYS = 128
PEER_TOPK = 16
NORM_EPS = 1e-6

LANES = 128
SUBLANES = 8
VMEM_LIMIT_BYTES = 56 * 1024 * 1024

TM = 512
TM_EW = 256
TM_ROUTE = 128
TE = 1024


def _params(*sem):
    return pltpu.CompilerParams(dimension_semantics=sem, vmem_limit_bytes=VMEM_LIMIT_BYTES)


def _sds(shape, dtype=f32):
    return jax.ShapeDtypeStruct(shape, dtype)


def _ada_kernel(c_ref, w_ref, b_ref, o_ref):
    c = c_ref[...]
    a = (c * jax.nn.sigmoid(c)).astype(bf16)
    o_ref[...] = jnp.dot(a, w_ref[...].astype(bf16), preferred_element_type=f32) + b_ref[...]


def _ada(c, w, b):
    m, n_out, tn = c.shape[0], w.shape[1], 1024
    return pl.pallas_call(
        _ada_kernel,
        grid=(n_out // tn,),
        in_specs=[pl.BlockSpec((m, D_MODEL), lambda j: (0, 0)),
                  pl.BlockSpec((D_MODEL, tn), lambda j: (0, j)),
                  pl.BlockSpec((1, tn), lambda j: (0, j))],
        out_specs=pl.BlockSpec((m, tn), lambda j: (0, j)),
        out_shape=_sds((m, n_out)),
        compiler_params=_params("arbitrary"),
        name="ada_mod",
    )(c, w, b.reshape(1, n_out))


def _mod3(m, bsz, t_len, tm):
    if t_len % tm == 0:
        per = t_len // tm
        return m.reshape(bsz, 1, D_MODEL), (lambda i: i // per)
    n = bsz * t_len
    assert tm % t_len == 0 and n % tm == 0
    return jnp.repeat(m, t_len, axis=0).reshape(n // tm, tm, D_MODEL), (lambda i: i)


def _mod_spec(arr, idx, ncols=D_MODEL, col=None):
    rows = arr.shape[1]
    if col is None:
        return pl.BlockSpec((1, rows, ncols), lambda i, *_: (idx(i), 0, 0))
    return pl.BlockSpec((1, rows, ncols), lambda i, j: (idx(i), 0, j))


def _nmm_kernel(*refs, has_gate, emit_hn):
    it = iter(refs)
    x_ref, g_ref, sc_ref, sh_ref, w_ref = (next(it) for _ in range(5))
    wg_ref = next(it) if has_gate else None
    o_ref = next(it)
    og_ref = next(it) if has_gate else None
    hn_ref = next(it) if emit_hn else None
    hs_ref = next(it)

    @pl.when(pl.program_id(1) == 0)
    def _():
        x = x_ref[...]
        y = x * lax.rsqrt(jnp.mean(x * x, axis=-1, keepdims=True) + NORM_EPS) * g_ref[...]
        h = (y * (1.0 + sc_ref[0]) + sh_ref[0]).astype(bf16)
        hs_ref[...] = h
        if emit_hn:
            hn_ref[...] = h
        if has_gate:
            og_ref[...] = jnp.dot(h, wg_ref[...].astype(bf16), preferred_element_type=f32)

    o_ref[...] = jnp.dot(hs_ref[...], w_ref[...].astype(bf16), preferred_element_type=f32)


def _norm_mod_matmul(x, g, sc3, sh3, idx, w, n_out, tn, w_gate=None, emit_hn=False, name="nmm"):
    n = x.shape[0]
    has_gate = w_gate is not None
    in_specs = [pl.BlockSpec((TM, D_MODEL), lambda i, j: (i, 0)),
                pl.BlockSpec((1, D_MODEL), lambda i, j: (0, 0)),
                _mod_spec(sc3, idx), _mod_spec(sh3, idx),
                pl.BlockSpec((D_MODEL, tn), lambda i, j: (0, j))]
    args = [x, g.reshape(1, D_MODEL), sc3, sh3, w]
    out_specs = [pl.BlockSpec((TM, tn), lambda i, j: (i, j))]
    out_shape = [_sds((n, n_out))]
    if has_gate:
        in_specs.append(pl.BlockSpec((D_MODEL, LANES), lambda i, j: (0, 0)))
        args.append(w_gate)
        out_specs.append(pl.BlockSpec((TM, LANES), lambda i, j: (i, 0)))
        out_shape.append(_sds((n, LANES)))
    if emit_hn:
        out_specs.append(pl.BlockSpec((TM, D_MODEL), lambda i, j: (i, 0)))
        out_shape.append(_sds((n, D_MODEL), bf16))
    return pl.pallas_call(
        functools.partial(_nmm_kernel, has_gate=has_gate, emit_hn=emit_hn),
        grid=(n // TM, n_out // tn),
        in_specs=in_specs, out_specs=out_specs, out_shape=out_shape,
        scratch_shapes=[pltpu.VMEM((TM, D_MODEL), bf16)],
        compiler_params=_params("arbitrary", "arbitrary"),
        name=name,
    )(*args)


def _split_dot(x, w):
    hi = x.astype(bf16)
    lo = (x - hi.astype(f32)).astype(bf16)
    return (jnp.dot(hi, w, preferred_element_type=f32) + jnp.dot(lo, w, preferred_element_type=f32))


def _head_sum(x, g1, g1t):
    return _split_dot(_split_dot(x, g1), g1t)


def _rwkv_prep_kernel(z_ref, prev_ref, mu_ref, w0_ref, wwa_ref, a0_ref, g2_ref, akk_ref, aka_ref, rk_ref,
                      g1_ref, g1t_ref, r_ref, k_ref, v_ref, d_ref, kk_ref, kka_ref, g_ref, bonus_ref):
    z = z_ref[...]
    zs = z + (prev_ref[...] - z) * mu_ref[...]
    r = zs[:, 0:MIX_A]
    k = zs[:, MIX_A:2 * MIX_A]
    v = zs[:, 2 * MIX_A:3 * MIX_A]
    xwa = zs[:, 3 * MIX_A:3 * MIX_A + 128]
    xg = zs[:, 3 * MIX_A + 128:3 * MIX_A + 256]
    lane = lax.broadcasted_iota(jnp.int32, xwa.shape, 1)
    xwa = jnp.where(lane < 64, jnp.tanh(xwa), xwa)
    lwa = jnp.dot(xwa.astype(bf16), wwa_ref[...], preferred_element_type=f32)
    u = w0_ref[...] + lwa[:, 0:MIX_A]
    w_log = -(jnp.maximum(-u, 0.0) + jnp.log(1.0 + jnp.exp(-jnp.abs(u)))) - 0.5
    d_ref[...] = jnp.exp(-jnp.exp(w_log))
    a = jax.nn.sigmoid(a0_ref[...] + lwa[:, MIX_A:2 * MIX_A])
    g_ref[...] = jnp.dot(jax.nn.sigmoid(xg).astype(bf16), g2_ref[...].astype(bf16), preferred_element_type=f32)
    g1, g1t = g1_ref[...], g1t_ref[...]
    kk = k * akk_ref[...]
    nrm = jnp.sqrt(_head_sum(kk * kk, g1, g1t))
    kk = kk / jnp.maximum(nrm, 1e-12)
    k = k * (1.0 + (a - 1.0) * aka_ref[...])
    r_ref[...] = r
    k_ref[...] = k
    v_ref[...] = v
    kk_ref[...] = kk
    kka_ref[...] = kk * a
    bonus_ref[...] = _head_sum(r * k * rk_ref[...], g1, g1t) * v


def _rwkv_prep(z, prev, p, wwa, g1, g1t):
    n = z.shape[0]
    row = lambda a: a.reshape(1, -1)
    tok = lambda w: pl.BlockSpec((TM_EW, w), lambda i: (i, 0))
    full = lambda a: pl.BlockSpec(a.shape, lambda i: (0,) * a.ndim)
    consts = [row(p['a_mu']), row(p['a_w0']), wwa, row(p['a_a0']), p['a_g2'], row(p['a_kk']), row(p['a_ka']),
              row(p['a_rk']), g1, g1t]
    return pl.pallas_call(
        _rwkv_prep_kernel,
        grid=(n // TM_EW,),
        in_specs=[tok(A_SHIFT_COLS), tok(A_SHIFT_COLS)] + [full(a) for a in consts],
        out_specs=[tok(MIX_A)] * 8,
        out_shape=[_sds((n, MIX_A))] * 8,
        compiler_params=_params("arbitrary"),
        name="rwkv_prep",
    )(z, prev, *consts)


def _rwkv_scan_kernel(r_ref, d_ref, k_ref, kk_ref, kka_ref, v_ref, s0_ref, y_ref, so_ref, *, n_rows, t_chunk):
    @pl.when(pl.program_id(1) == 0)
    def _():
        so_ref[...] = s0_ref[...]

    def step(t, carry):
        kk, d, kka, k, r = kk_ref[t], d_ref[t], kka_ref[t], k_ref[t], r_ref[t]
        for i in range(n_rows):
            s_i = so_ref[i]
            sa = -jnp.sum(s_i * kk, axis=0, keepdims=True)
            v_i = v_ref[t, pl.ds(i, 1), :]
            s_n = s_i * d + sa * kka + v_i * k
            so_ref[i] = s_n
            y_ref[t, pl.ds(i, 1), :] = jnp.sum(s_n * r, axis=0, keepdims=True)
        return carry

    lax.fori_loop(0, t_chunk, step, 0)


def _rwkv_scan(r, d, k, kk, kka, v, s0, t_chunk):
    t_len, _, ltot = r.shape
    n_rows = v.shape[1]
    jspec = pl.BlockSpec((t_chunk, A_HEAD, LANES), lambda l, t: (t, 0, l))
    vspec = pl.BlockSpec((t_chunk, n_rows, LANES), lambda l, t: (t, 0, l))
    sspec = pl.BlockSpec((n_rows, A_HEAD, LANES), lambda l, t: (0, 0, l))
    return pl.pallas_call(
        functools.partial(_rwkv_scan_kernel, n_rows=n_rows, t_chunk=t_chunk),
        grid=(ltot // LANES, t_len // t_chunk),
        in_specs=[jspec] * 5 + [vspec, sspec],
        out_specs=[vspec, sspec],
        out_shape=[_sds((t_len, n_rows, ltot)), _sds((n_rows, A_HEAD, ltot))],
        compiler_params=_params("arbitrary", "arbitrary"),
        name="rwkv_scan",
    )(r, d, k, kk, kka, v, s0)


def _rwkv_post_kernel(y_ref, bonus_ref, g_ref, lnw_ref, lnb_ref, g1_ref, g1t_ref, o_ref):
    y = y_ref[...]
    g1, g1t = g1_ref[...], g1t_ref[...]
    mu = _head_sum(y, g1, g1t) * (1.0 / A_HEAD)
    yc = y - mu
    var = _head_sum(yc * yc, g1, g1t) * (1.0 / A_HEAD)
    yn = yc * lax.rsqrt(var + A_GN_EPS) * lnw_ref[...] + lnb_ref[...]
    o_ref[...] = (yn + bonus_ref[...]) * g_ref[...]


def _rwkv_post(y, bonus, g, p, g1, g1t):
    n = y.shape[0]
    tok = pl.BlockSpec((TM_EW, MIX_A), lambda i: (i, 0))
    full = lambda a: pl.BlockSpec(a.shape, lambda i: (0,) * a.ndim)
    consts = [p['a_lnw'].reshape(1, -1), p['a_lnb'].reshape(1, -1), g1, g1t]
    return pl.pallas_call(
        _rwkv_post_kernel,
        grid=(n // TM_EW,),
        in_specs=[tok] * 3 + [full(a) for a in consts],
        out_specs=tok, out_shape=_sds((n, MIX_A)),
        compiler_params=_params("arbitrary"),
        name="rwkv_post",
    )(y, bonus, g, *consts)


def _rwkv_mix(z, bsz, t_len, shift_buf, wkv0, p, wwa, g1, g1t):
    n = bsz * t_len
    z3 = z.reshape(bsz, t_len, Z_MAIN_COLS)
    za = z3[:, :, :A_SHIFT_COLS]
    prev = jnp.concatenate([shift_buf[:, None, :], za[:, :-1]], axis=1).reshape(n, A_SHIFT_COLS)
    r, k, v, d, kk, kka, g, bonus = _rwkv_prep(z, prev, p, wwa, g1, g1t)

    bh = bsz * A_HEADS
    pack = LANES // bh if bh < LANES else 1
    n_rows = A_HEAD // pack

    def key_major(a):
        a = a.reshape(bsz, t_len, A_HEADS, A_HEAD).transpose(1, 3, 0, 2).reshape(t_len, A_HEAD, bh)
        return jnp.concatenate([a] * pack, axis=-1) if pack > 1 else a

    def val_major(a):
        a = a.reshape(bsz, t_len, A_HEADS, pack, n_rows).transpose(1, 4, 3, 0, 2)
        return a.reshape(t_len, n_rows, pack * bh)

    s0 = wkv0.reshape(bsz, A_HEADS, pack, n_rows, A_HEAD).transpose(3, 4, 2, 0, 1).reshape(n_rows, A_HEAD, pack * bh)
    t_chunk = min(t_len, 64)
    y, s_out = _rwkv_scan(key_major(r), key_major(d), key_major(k), key_major(kk), key_major(kka),
                          val_major(v), s0, t_chunk)
    y = y.reshape(t_len, n_rows, pack, bsz, A_HEADS).transpose(3, 0, 4, 2, 1).reshape(n, MIX_A)
    new_wkv = s_out.reshape(n_rows, A_HEAD, pack, bsz, A_HEADS).transpose(3, 4, 2, 0, 1)
    new_wkv = new_wkv.reshape(bsz, A_HEADS, A_HEAD, A_HEAD)
    y = _rwkv_post(y, bonus, g, p, g1, g1t)
    return y, za[:, -1], new_wkv


def _conv_kernel(z0_ref, z1_ref, z2_ref, z3_ref, cw_ref, cb_ref, o_ref):
    acc = z0_ref[...] * cw_ref[0:1, :] + z1_ref[...] * cw_ref[1:2, :]
    acc = acc + z2_ref[...] * cw_ref[2:3, :] + z3_ref[...] * cw_ref[3:4, :] + cb_ref[...]
    o_ref[...] = acc * jax.nn.sigmoid(acc)


def _conv(shifted, cw, cb):
    n = shifted[0].shape[0]
    tok = pl.BlockSpec((TM_EW, B_QK_COLS), lambda i: (i, 0))
    return pl.pallas_call(
        _conv_kernel,
        grid=(n // TM_EW,),
        in_specs=[tok] * 4 + [pl.BlockSpec((B_CONV, B_QK_COLS), lambda i: (0, 0)),
                              pl.BlockSpec((1, B_QK_COLS), lambda i: (0, 0))],
        out_specs=tok, out_shape=_sds((n, B_QK_COLS)),
        compiler_params=_params("arbitrary"),
        name="mlstm_conv",
    )(*shifted, cw, cb.reshape(1, -1))


def _log_sigmoid(x):
    return jnp.minimum(x, 0.0) - jnp.log(1.0 + jnp.exp(-jnp.abs(x)))


def _mlstm_kernel(*refs, chunk):
    (qk_ref, v0, v1, v2, v3, o0, o1, o2, o3, gc_ref, gr_ref, bias_r_ref, bias_c_ref, ng_ref,
     c0_ref, n0_ref, m0_ref, y_ref, c_ref, n_ref, m_ref) = refs
    v_refs, o_refs = (v0, v1, v2, v3), (o0, o1, o2, o3)

    @pl.when(pl.program_id(1) == 0)
    def _():
        c_ref[...] = c0_ref[...]
        n_ref[...] = n0_ref[...]
        m_ref[...] = m0_ref[...]

    gcol = gc_ref[...] + bias_r_ref[...]
    grow = gr_ref[0] + bias_c_ref[...]
    ti = lax.broadcasted_iota(jnp.int32, (chunk, chunk), 0)
    si = lax.broadcasted_iota(jnp.int32, (chunk, chunk), 1)
    tri = si <= ti
    nt = (((1,), (1,)), ((), ()))
    tn = (((0,), (0,)), ((), ()))
    for h in range(B_HEADS):
        q = qk_ref[:, h * B_DQK:(h + 1) * B_DQK]
        k = qk_ref[:, (B_HEADS + h) * B_DQK:(B_HEADS + h + 1) * B_DQK] * (B_DQK ** -0.5)
        v = v_refs[h][...]
        i_col, f_col = gcol[:, h:h + 1], gcol[:, B_HEADS + h:B_HEADS + h + 1]
        i_row, f_row = grow[h:h + 1, :], grow[B_HEADS + h:B_HEADS + h + 1, :]
        lf_col, lf_row = _log_sigmoid(f_col), _log_sigmoid(f_row)
        b_col = jnp.sum(jnp.where(tri, lf_row, 0.0), axis=1, keepdims=True)
        b_row = jnp.sum(jnp.where(ti <= si, lf_col, 0.0), axis=0, keepdims=True)
        m_prev = m_ref[0, h:h + 1, :]
        log_d = jnp.where(tri, b_col - b_row + i_row, -jnp.inf)
        inter = b_col + m_prev
        m_t = jnp.maximum(inter, jnp.max(log_d, axis=1, keepdims=True))
        qb, kb, vb = q.astype(bf16), k.astype(bf16), v.astype(bf16)
        s = lax.dot_general(qb, kb, nt, preferred_element_type=f32) * jnp.exp(log_d - m_t)
        w_inter = jnp.exp(inter - m_t)
        c_old = c_ref[0, h]
        n_old = n_ref[0, h:h + 1, :]
        num = (jnp.dot(s.astype(bf16), vb, preferred_element_type=f32)
               + w_inter * jnp.dot(qb, c_old.astype(bf16), preferred_element_type=f32))
        den = jnp.sum(s, axis=1, keepdims=True) + w_inter * jnp.sum(q * n_old, axis=1, keepdims=True)
        hh = num / jnp.maximum(jnp.abs(den), jnp.exp(-m_t))
        m_new = m_t[chunk - 1:chunk, :]
        b_last = b_col[chunk - 1:chunk, :]
        ws = jnp.exp(b_last - b_col + i_col - m_new)
        d_c = jnp.exp(b_last + m_prev - m_new)
        kw = k * ws
        c_ref[0, h] = d_c * c_old + lax.dot_general(kw.astype(bf16), vb, tn, preferred_element_type=f32)
        n_ref[0, h:h + 1, :] = d_c * n_old + jnp.sum(kw, axis=0, keepdims=True)
        m_ref[0, h:h + 1, :] = m_new
        hn = hh * lax.rsqrt(jnp.mean(hh * hh, axis=1, keepdims=True) + NORM_EPS)
        y_ref[:, h * B_DV:(h + 1) * B_DV] = (hn * ng_ref[:, h * B_DV:(h + 1) * B_DV]
                                              * jax.nn.sigmoid(o_refs[h][...]))


def _mlstm(qk, z, gates_col, gates_row, p, c0, n0, m0, bsz, n_chunks, chunk):
    n = qk.shape[0]
    v_col0 = (A_SHIFT_COLS + B_QK_COLS) // B_DV
    o_col0 = v_col0 + B_HEADS
    tok = lambda w: pl.BlockSpec((chunk, w), lambda b, c: (b * n_chunks + c, 0))
    zcol = lambda cb: pl.BlockSpec((chunk, B_DV), lambda b, c: (b * n_chunks + c, cb))
    full = lambda a: pl.BlockSpec(a.shape, lambda b, c: (0,) * a.ndim)
    bias = jnp.concatenate([p['b_ig_bias'], p['b_fg_bias']])
    bias_r = jnp.pad(bias, (0, LANES - 2 * B_HEADS)).reshape(1, LANES)
    bias_c = bias.reshape(2 * B_HEADS, 1)
    ng = p['b_norm_g'].reshape(1, -1)
    cspec = pl.BlockSpec((1, B_HEADS, B_DQK, B_DV), lambda b, c: (b, 0, 0, 0))
    nspec = pl.BlockSpec((1, B_HEADS, B_DQK), lambda b, c: (b, 0, 0))
    mspec = pl.BlockSpec((1, B_HEADS, 1), lambda b, c: (b, 0, 0))
    return pl.pallas_call(
        functools.partial(_mlstm_kernel, chunk=chunk),
        grid=(bsz, n_chunks),
        in_specs=[tok(B_QK_COLS)] + [zcol(v_col0 + h) for h in range(B_HEADS)]
                 + [zcol(o_col0 + h) for h in range(B_HEADS)]
                 + [tok(LANES), pl.BlockSpec((1, 2 * B_HEADS, chunk), lambda b, c: (b * n_chunks + c, 0, 0)),
                    full(bias_r), full(bias_c), full(ng), cspec, nspec, mspec],
        out_specs=[tok(B_HEADS * B_DV), cspec, nspec, mspec],
        out_shape=[_sds((n, B_HEADS * B_DV)), _sds(c0.shape), _sds(n0.shape), _sds(m0.shape)],
        compiler_params=_params("arbitrary", "arbitrary"),
        name="mlstm",
    )(qk, *([z] * (2 * B_HEADS)), gates_col, gates_row, bias_r, bias_c, ng, c0, n0, m0)


def _mlstm_mix(z, gates, bsz, t_len, conv_buf, c0, n0, m0, p):
    n = bsz * t_len
    z3 = z.reshape(bsz, t_len, Z_MAIN_COLS)
    zp = jnp.concatenate([conv_buf, z3[:, :, A_SHIFT_COLS:A_SHIFT_COLS + B_QK_COLS]], axis=1)
    shifted = [zp[:, w:w + t_len].reshape(n, B_QK_COLS) for w in range(B_CONV)]
    qk = _conv(shifted, p['b_conv_w'], p['b_conv_b'])
    chunk = B_CHUNK if t_len % B_CHUNK == 0 else SUBLANES
    t_pad = -(-t_len // chunk) * chunk
    g3 = gates.reshape(bsz, t_len, LANES)
    if t_pad != t_len:
        padw = ((0, 0), (0, t_pad - t_len), (0, 0))
        qk = jnp.pad(qk.reshape(bsz, t_len, -1), padw).reshape(bsz * t_pad, -1)
        z = jnp.pad(z3, padw).reshape(bsz * t_pad, -1)
        lane = jnp.arange(LANES)
        fill = jnp.where(lane < B_HEADS, -1e30, jnp.where(lane < 2 * B_HEADS, 1e30, 0.0)).astype(f32)
        g3 = jnp.concatenate([g3, jnp.broadcast_to(fill, (bsz, t_pad - t_len, LANES))], axis=1)
    n_chunks = t_pad // chunk
    gates_col = g3.reshape(bsz * t_pad, LANES)
    gates_row = g3[:, :, :2 * B_HEADS].reshape(bsz * n_chunks, chunk, 2 * B_HEADS).transpose(0, 2, 1)
    y, c_new, n_new, m_new = _mlstm(qk, z, gates_col, gates_row, p, c0, n0, m0.reshape(bsz, B_HEADS, 1),
                                    bsz, n_chunks, chunk)
    if t_pad != t_len:
        y = y.reshape(bsz, t_pad, -1)[:, :t_len].reshape(n, -1)
    return y, zp[:, t_len:], c_new, n_new, m_new.reshape(bsz, B_HEADS)


def _oproj_kernel(ya_ref, yb_ref, wt_ref, wb_ref, x_ref, ga_ref, o_ref):
    acc = jnp.dot(ya_ref[...].astype(bf16), wt_ref[...].astype(bf16), preferred_element_type=f32)
    acc = acc + jnp.dot(yb_ref[...].astype(bf16), wb_ref[...].astype(bf16), preferred_element_type=f32)
    o_ref[...] = x_ref[...] + ga_ref[0] * acc


def _oproj(ya, yb, w_out, x, ga3, idx):
    n, tn = x.shape[0], 512
    half = D_MODEL // 2
    return pl.pallas_call(
        _oproj_kernel,
        grid=(n // TM, D_MODEL // tn),
        in_specs=[pl.BlockSpec((TM, half), lambda i, j: (i, 0)), pl.BlockSpec((TM, half), lambda i, j: (i, 0)),
                  pl.BlockSpec((half, tn), lambda i, j: (0, j)), pl.BlockSpec((half, tn), lambda i, j: (1, j)),
                  pl.BlockSpec((TM, tn), lambda i, j: (i, j)), _mod_spec(ga3, idx, tn, col=True)],
        out_specs=pl.BlockSpec((TM, tn), lambda i, j: (i, j)),
        out_shape=_sds((n, D_MODEL)),
        compiler_params=_params("arbitrary", "arbitrary"),
        name="out_proj",
    )(ya, yb, w_out, w_out, x, ga3)


def _route_kernel(q_ref, keys_ref, cnt_ref, e1_ref, r2_ref, e2_ref, v1_scr, v2_scr):
    q = q_ref[...]
    tm = q.shape[0]
    neg_inf = -jnp.inf
    iota_k = lax.broadcasted_iota(jnp.int32, (PEER_NKEYS, tm), 0).astype(f32)
    nt = (((1,), (1,)), ((), ()))
    ranks, es = [], []
    for half, vscr in enumerate((v1_scr, v2_scr)):
        qp = q[:, half * 128:(half + 1) * 128].astype(bf16)
        s = lax.dot_general(keys_ref[half, 0].astype(bf16), qp, nt, preferred_element_type=f32)
        rank = jnp.full((PEER_NKEYS, tm), float(PEER_TOPK), f32)
        work = s
        for a in range(PEER_TOPK):
            m = jnp.max(work, axis=0, keepdims=True)
            idx = jnp.min(jnp.where(work == m, iota_k, float(PEER_NKEYS)), axis=0, keepdims=True)
            sel = iota_k == idx
            rank = jnp.where(sel, float(a), rank)
            work = jnp.where(sel, neg_inf, work)
            vscr[a:a + 1, :] = m
        ranks.append(rank)
        es.append(jnp.exp(s - vscr[0:1, :]))
    v1, v2 = v1_scr[...], v2_scr[...]
    row8 = lax.broadcasted_iota(jnp.int32, (SUBLANES, tm), 0).astype(f32)
    row16 = lax.broadcasted_iota(jnp.int32, (PEER_TOPK, tm), 0).astype(f32)
    pieces, poss = [v1[0:1, :] + v2], [row16]
    for a in range(1, 8):
        nb = PEER_TOPK // (a + 1)
        pieces.append(jnp.where(row8 < float(nb), v1[a:a + 1, :] + v2[0:SUBLANES, :], neg_inf))
        poss.append(row8 + float(a * PEER_TOPK))
    pieces.append(v1[8:16, :] + v2[0:1, :])
    poss.append((row8 + 8.0) * float(PEER_TOPK))
    cand = jnp.concatenate(pieces, axis=0)
    pos = jnp.concatenate(poss, axis=0)
    cnt16 = jnp.zeros((PEER_TOPK, tm), f32)
    zsum = None
    best0 = None
    for j in range(PEER_TOPK):
        m = jnp.max(cand, axis=0, keepdims=True)
        pj = jnp.min(jnp.where(cand == m, pos, 1e9), axis=0, keepdims=True)
        cand = jnp.where(pos == pj, neg_inf, cand)
        if j == 0:
            best0 = m
            zsum = jnp.ones_like(m)
        else:
            zsum = zsum + jnp.exp(m - best0)
        a_j = jnp.floor(pj * (1.0 / PEER_TOPK))
        cnt16 = cnt16 + jnp.where(row16 == a_j, 1.0, 0.0)
    cnt_i = jnp.zeros((PEER_NKEYS, tm), f32)
    for a in range(PEER_TOPK):
        cnt_i = jnp.where(ranks[0] == float(a), cnt16[a:a + 1, :], cnt_i)
    cnt_ref[0] = cnt_i
    e1_ref[0] = es[0] / zsum
    r2_ref[0] = ranks[1]
    e2_ref[0] = es[1]


def _route(q, keys):
    n = q.shape[0]
    ospec = pl.BlockSpec((1, PEER_NKEYS, TM_ROUTE), lambda i, h: (h, 0, i))
    oshape = _sds((PEER_HEADS, PEER_NKEYS, n))
    return pl.pallas_call(
        _route_kernel,
        grid=(n // TM_ROUTE, PEER_HEADS),
        in_specs=[pl.BlockSpec((TM_ROUTE, 256), lambda i, h: (i, h)),
                  pl.BlockSpec((2, 1, PEER_NKEYS, 128), lambda i, h: (0, h, 0, 0))],
        out_specs=[ospec] * 4, out_shape=[oshape] * 4,
        scratch_shapes=[pltpu.VMEM((PEER_TOPK, TM_ROUTE), f32), pltpu.VMEM((PEER_TOPK, TM_ROUTE), f32)],
        compiler_params=_params("arbitrary", "arbitrary"),
        name="peer_route",
    )(q, keys)


def _dense_kernel(hn_ref, u_ref, vt_ref, cnt_ref, e1_ref, r2_ref, e2_ref, o_ref, xt_scr, acc_scr, act_scr,
                  *, n_chunks):
    c = pl.program_id(1)

    @pl.when(c == 0)
    def _():
        xt_scr[...] = hn_ref[...].T
        acc_scr[...] = jnp.zeros_like(acc_scr)

    ht = jnp.dot(u_ref[...], xt_scr[...], preferred_element_type=f32)
    for s in range(TE // PEER_NKEYS):
        hs = ht[s * PEER_NKEYS:(s + 1) * PEER_NKEYS, :]
        gate = None
        for h in range(PEER_HEADS):
            term = jnp.where(r2_ref[h] < cnt_ref[h, s:s + 1, :], e2_ref[h] * e1_ref[h, s:s + 1, :], 0.0)
            gate = term if gate is None else gate + term
        act = 0.5 * hs * (1.0 + lax.erf(hs * (2.0 ** -0.5))) * gate
        act_scr[s * PEER_NKEYS:(s + 1) * PEER_NKEYS, :] = act.astype(bf16)
    acc_scr[...] += jnp.dot(vt_ref[...], act_scr[...], preferred_element_type=f32)

    @pl.when(c == n_chunks - 1)
    def _():
        o_ref[...] = acc_scr[...].T


def _peer_dense(hn, u_bf, vt_bf, cnt, e1, r2, e2):
    n = hn.shape[0]
    n_exp = u_bf.shape[0]
    n_chunks = n_exp // TE
    sub = TE // PEER_NKEYS
    return pl.pallas_call(
        functools.partial(_dense_kernel, n_chunks=n_chunks),
        grid=(n // TM, n_chunks),
        in_specs=[pl.BlockSpec((TM, D_MODEL), lambda i, c: (i, 0)),
                  pl.BlockSpec((TE, D_MODEL), lambda i, c: (c, 0)),
                  pl.BlockSpec((D_MODEL, TE), lambda i, c: (0, c)),
                  pl.BlockSpec((PEER_HEADS, sub, TM), lambda i, c: (0, c, i)),
                  pl.BlockSpec((PEER_HEADS, sub, TM), lambda i, c: (0, c, i)),
                  pl.BlockSpec((PEER_HEADS, PEER_NKEYS, TM), lambda i, c: (0, 0, i)),
                  pl.BlockSpec((PEER_HEADS, PEER_NKEYS, TM), lambda i, c: (0, 0, i))],
        out_specs=pl.BlockSpec((TM, D_MODEL), lambda i, c: (i, 0)),
        out_shape=_sds((n, D_MODEL)),
        scratch_shapes=[pltpu.VMEM((D_MODEL, TM), bf16), pltpu.VMEM((D_MODEL, TM), f32),
                        pltpu.VMEM((TE, TM), bf16)],
        compiler_params=_params("arbitrary", "arbitrary"),
        name="peer_dense",
    )(hn, u_bf, vt_bf, cnt, e1, r2, e2)


def _final_kernel(x_ref, f_ref, gc_ref, g_ref, sc_ref, sh_ref, o_ref):
    x = x_ref[...] + gc_ref[0] * f_ref[...]
    y = x * lax.rsqrt(jnp.mean(x * x, axis=-1, keepdims=True) + NORM_EPS) * g_ref[...]
    o_ref[...] = y * (1.0 + sc_ref[0]) + sh_ref[0]


def _final(x, ffn, gc3, g, sc3, sh3, idx):
    n = x.shape[0]
    tok = pl.BlockSpec((TM, D_MODEL), lambda i: (i, 0))
    return pl.pallas_call(
        _final_kernel,
        grid=(n // TM,),
        in_specs=[tok, tok, _mod_spec(gc3, idx), pl.BlockSpec((1, D_MODEL), lambda i: (0, 0)),
                  _mod_spec(sc3, idx), _mod_spec(sh3, idx)],
        out_specs=tok, out_shape=_sds((n, D_MODEL)),
        compiler_params=_params("arbitrary"),
        name="final_norm",
    )(x, ffn, gc3, g.reshape(1, D_MODEL), sc3, sh3)


def _trunk(x, mod, mod_f, states, p, shared):
    bsz, t_len, _ = x.shape
    n = bsz * t_len
    shift_s, wkv_s, conv_s, c_s, n_s, m_s = states
    xf = x.reshape(n, D_MODEL)
    m3 = [_mod3(m, bsz, t_len, TM) for m in jnp.split(mod, 6, axis=-1)]
    idx = m3[0][1]
    sh_a, sc_a, g_a, sh_c, sc_c, g_c = (a for a, _ in m3)
    (shift_f, _), (scale_f, _) = (_mod3(m, bsz, t_len, TM) for m in jnp.split(mod_f, 2, axis=-1))

    z, gates = _norm_mod_matmul(xf, p['norm_mix_g'], sc_a, sh_a, idx, p['w_in'], Z_MAIN_COLS, 640,
                                w_gate=shared['w_gate'], name="in_proj")
    y_a, new_shift, new_wkv = _rwkv_mix(z, bsz, t_len, shift_s, wkv_s, p, shared['wwa'], shared['g1'],
                                        shared['g1t'])
    y_b, new_conv, new_c, new_n, new_m = _mlstm_mix(z, gates, bsz, t_len, conv_s, c_s, n_s, m_s, p)
    x1 = _oproj(y_a, y_b, p['w_out'], xf, g_a, idx)
    q, hn = _norm_mod_matmul(x1, p['norm_ffn_g'], sc_c, sh_c, idx, p['peer_wq'], D_MODEL, 512, emit_hn=True,
                             name="peer_query")
    cnt, e1, r2, e2 = _route(q, p['peer_keys'])
    ffn = _peer_dense(hn, shared['u_bf'], shared['vt_bf'], cnt, e1, r2, e2)
    y = _final(x1, ffn, g_c, p['norm_final_g'], scale_f, shift_f, idx)
    return y.reshape(bsz, t_len, D_MODEL), (new_shift, new_wkv, new_conv, new_c, new_n, new_m)


def kernel(x_prompt, x_sample, c_prompt, c_sample, state_rwkv_shift, state_rwkv_wkv, state_mlstm_conv,
           state_mlstm_C, state_mlstm_n, state_mlstm_m, w_ada, b_ada, norm_mix_g, w_in, a_mu, a_w0, a_w2,
           a_a0, a_a2, a_g2, a_kk, a_ka, a_rk, a_lnw, a_lnb, b_conv_w, b_conv_b, b_ig_bias, b_fg_bias,
           b_norm_g, w_out, norm_ffn_g, peer_wq, peer_keys, peer_u, peer_v, w_ada_final, b_ada_final,
           norm_final_g):
    layer = dict(w_ada=w_ada, b_ada=b_ada, norm_mix_g=norm_mix_g, w_in=w_in, a_mu=a_mu, a_w0=a_w0, a_w2=a_w2,
                 a_a0=a_a0, a_a2=a_a2, a_g2=a_g2, a_kk=a_kk, a_ka=a_ka, a_rk=a_rk, a_lnw=a_lnw, a_lnb=a_lnb,
                 b_conv_w=b_conv_w, b_conv_b=b_conv_b, b_ig_bias=b_ig_bias, b_fg_bias=b_fg_bias,
                 b_norm_g=b_norm_g, w_out=w_out, norm_ffn_g=norm_ffn_g, peer_wq=peer_wq, peer_keys=peer_keys,
                 peer_u=peer_u, peer_v=peer_v)
    assert w_ada.shape[0] == 1, "single-layer trunk"
    p = {k: v[0] for k, v in layer.items()}
    p['norm_final_g'] = norm_final_g
    bp, bs = x_prompt.shape[0], x_sample.shape[0]

    c_all = jnp.concatenate([c_prompt, c_sample], axis=0)
    c_all = jnp.pad(c_all, ((0, -c_all.shape[0] % SUBLANES), (0, 0)))
    mod = _ada(c_all, p['w_ada'], p['b_ada'])
    mod_f = _ada(c_all, w_ada_final, b_ada_final)

    head_of = jnp.arange(MIX_A) // A_HEAD
    g1 = (head_of[:, None] == jnp.arange(LANES)[None, :]).astype(bf16)
    zero = jnp.zeros((64, MIX_A), f32)
    wwa = jnp.concatenate([jnp.concatenate([p['a_w2'], zero], axis=1),
                           jnp.concatenate([zero, p['a_a2']], axis=1)], axis=0).astype(bf16)
    shared = dict(
        g1=g1, g1t=g1.T, wwa=wwa,
        w_gate=jnp.pad(p['w_in'][:, Z_MAIN_COLS:], ((0, 0), (0, LANES - 2 * B_HEADS))),
        u_bf=p['peer_u'].astype(bf16), vt_bf=p['peer_v'].T.astype(bf16))

    dt = x_prompt.dtype
    zero_states = (jnp.zeros((bp, A_SHIFT_COLS), dt), jnp.zeros((bp, A_HEADS, A_HEAD, A_HEAD), dt),
                   jnp.zeros((bp, B_CONV - 1, B_QK_COLS), dt), jnp.zeros((bp, B_HEADS, B_DQK, B_DV), dt),
                   jnp.zeros((bp, B_HEADS, B_DQK), dt), jnp.zeros((bp, B_HEADS), dt))
    y_prompt, p_states = _trunk(x_prompt, mod[:bp], mod_f[:bp], zero_states, p, shared)
    s_states_in = (state_rwkv_shift[0], state_rwkv_wkv[0], state_mlstm_conv[0], state_mlstm_C[0],
                   state_mlstm_n[0], state_mlstm_m[0])
    y_sample, s_states = _trunk(x_sample, mod[bp:bp + bs], mod_f[bp:bp + bs], s_states_in, p, shared)
    return (y_prompt, y_sample) + tuple(s[None] for s in p_states) + tuple(s[None] for s in s_states)
```

```python
import functools

import jax
import jax.numpy as jnp
from jax import lax
from jax.experimental import pallas as pl
from jax.experimental.pallas import tpu as pltpu

f32 = jnp.float32
bf16 = jnp.bfloat16

D_MODEL = 2048
MIX_A = 1024
A_HEAD = 64
A_HEADS = 16
A_SHIFT_COLS = 3 * MIX_A + 64 + 64 + 128
A_GN_EPS = A_HEAD * 1e-5
B_QK_COLS = 1024
B_HEADS = 4
B_DV = 256
B_DQK = 128
B_CONV = 4
B_CHUNK = 64
Z_MAIN_COLS = A_SHIFT_COLS + B_QK_COLS + 2 * 1024
PEER_HEADS = 8
PEER_NKEYS = 128
PEER_TOPK = 16
NORM_EPS = 1e-6

LANES = 128
SUBLANES = 8
VMEM_LIMIT_BYTES = 56 * 1024 * 1024

TM = 512
TM_EW = 256
TM_ROUTE = 256
TE = 1024
TE_SUB = 256


def _params(*sem):
    return pltpu.CompilerParams(dimension_semantics=sem, vmem_limit_bytes=VMEM_LIMIT_BYTES)


def _sds(shape, dtype=f32):
    return jax.ShapeDtypeStruct(shape, dtype)


def _ada_kernel(c_ref, w_ref, b_ref, o_ref):
    c = c_ref[...]
    a = (c * jax.nn.sigmoid(c)).astype(bf16)
    o_ref[...] = jnp.dot(a, w_ref[...].astype(bf16), preferred_element_type=f32) + b_ref[...]


def _ada(c, w, b):
    m, n_out, tn = c.shape[0], w.shape[1], 1024
    return pl.pallas_call(
        _ada_kernel,
        grid=(n_out // tn,),
        in_specs=[pl.BlockSpec((m, D_MODEL), lambda j: (0, 0)),
                  pl.BlockSpec((D_MODEL, tn), lambda j: (0, j)),
                  pl.BlockSpec((1, tn), lambda j: (0, j))],
        out_specs=pl.BlockSpec((m, tn), lambda j: (0, j)),
        out_shape=_sds((m, n_out)),
        compiler_params=_params("arbitrary"),
        name="ada_mod",
    )(c, w, b.reshape(1, n_out))


def _mod3(m, bsz, t_len, tm):
    if t_len % tm == 0:
        per = t_len // tm
        return m.reshape(bsz, 1, D_MODEL), (lambda i: i // per)
    n = bsz * t_len
    assert tm % t_len == 0 and n % tm == 0
    return jnp.repeat(m, t_len, axis=0).reshape(n // tm, tm, D_MODEL), (lambda i: i)


def _mod_spec(arr, idx):
    return pl.BlockSpec((1, arr.shape[1], D_MODEL), lambda i, *_: (idx(i), 0, 0))


def _nmm_kernel(*refs, has_gate, emit_hn):
    it = iter(refs)
    x_ref, g_ref, sc_ref, sh_ref, w_ref = (next(it) for _ in range(5))
    wg_ref = next(it) if has_gate else None
    o_ref = next(it)
    og_ref = next(it) if has_gate else None
    hn_ref = next(it) if emit_hn else None
    hs_ref = next(it)

    @pl.when(pl.program_id(1) == 0)
    def _():
        x = x_ref[...]
        y = x * lax.rsqrt(jnp.mean(x * x, axis=-1, keepdims=True) + NORM_EPS) * g_ref[...]
        h = (y * (1.0 + sc_ref[0]) + sh_ref[0]).astype(bf16)
        hs_ref[...] = h
        if emit_hn:
            hn_ref[...] = h
        if has_gate:
            og_ref[...] = jnp.dot(h, wg_ref[...], preferred_element_type=f32)

    o_ref[...] = jnp.dot(hs_ref[...], w_ref[...], preferred_element_type=f32)


def _norm_mod_matmul(x, g, sc3, sh3, idx, w_bf, n_out, tn, w_gate=None, emit_hn=False, name="nmm"):
    n = x.shape[0]
    has_gate = w_gate is not None
    in_specs = [pl.BlockSpec((TM, D_MODEL), lambda i, j: (i, 0)),
                pl.BlockSpec((1, D_MODEL), lambda i, j: (0, 0)),
                _mod_spec(sc3, idx), _mod_spec(sh3, idx),
                pl.BlockSpec((D_MODEL, tn), lambda i, j: (0, j))]
    args = [x, g.reshape(1, D_MODEL), sc3, sh3, w_bf]
    out_specs = [pl.BlockSpec((TM, tn), lambda i, j: (i, j))]
    out_shape = [_sds((n, n_out))]
    if has_gate:
        in_specs.append(pl.BlockSpec((D_MODEL, LANES), lambda i, j: (0, 0)))
        args.append(w_gate)
        out_specs.append(pl.BlockSpec((TM, LANES), lambda i, j: (i, 0)))
        out_shape.append(_sds((n, LANES)))
    if emit_hn:
        out_specs.append(pl.BlockSpec((TM, D_MODEL), lambda i, j: (i, 0)))
        out_shape.append(_sds((n, D_MODEL), bf16))
    return pl.pallas_call(
        functools.partial(_nmm_kernel, has_gate=has_gate, emit_hn=emit_hn),
        grid=(n // TM, n_out // tn),
        in_specs=in_specs, out_specs=out_specs, out_shape=out_shape,
        scratch_shapes=[pltpu.VMEM((TM, D_MODEL), bf16)],
        compiler_params=_params("arbitrary", "arbitrary"),
        name=name,
    )(*args)


def _split_dot(x, w):
    hi = x.astype(bf16)
    lo = (x - hi.astype(f32)).astype(bf16)
    return (jnp.dot(hi, w, preferred_element_type=f32) + jnp.dot(lo, w, preferred_element_type=f32))


def _head_sum(x, g1, g1t):
    return _split_dot(_split_dot(x, g1), g1t)


def _rwkv_prep_kernel(z_ref, first_ref, mu_ref, w0_ref, wwa_ref, a0_ref, g2_ref, akk_ref, aka_ref, rk_ref,
                      g1_ref, g1t_ref, r_ref, k_ref, v_ref, d_ref, kk_ref, kka_ref, g_ref, bonus_ref,
                      *scratch, t_len, carry_mode):
    z = z_ref[...]
    tm = z.shape[0]
    row = lax.broadcasted_iota(jnp.int32, z.shape, 0)
    rolled = pltpu.roll(z, 1, axis=0)
    if carry_mode:
        carry_ref, = scratch
        first = jnp.where(pl.program_id(1) == 0, first_ref[0], carry_ref[0:1, :])
        prev = jnp.where(row == 0, first, rolled)
        carry_ref[0:1, :] = z[tm - 1:tm, :]
    else:
        prev = jnp.where(lax.rem(row, t_len) == 0, first_ref[...], rolled)
    zs = z + (prev - z) * mu_ref[...]
    r = zs[:, 0:MIX_A]
    k = zs[:, MIX_A:2 * MIX_A]
    v = zs[:, 2 * MIX_A:3 * MIX_A]
    xwa = zs[:, 3 * MIX_A:3 * MIX_A + 128]
    xg = zs[:, 3 * MIX_A + 128:3 * MIX_A + 256]
    lane = lax.broadcasted_iota(jnp.int32, xwa.shape, 1)
    xwa = jnp.where(lane < 64, jnp.tanh(xwa), xwa)
    lwa = jnp.dot(xwa.astype(bf16), wwa_ref[...], preferred_element_type=f32)
    u = w0_ref[...] + lwa[:, 0:MIX_A]
    w_log = -(jnp.maximum(-u, 0.0) + jnp.log(1.0 + jnp.exp(-jnp.abs(u)))) - 0.5
    d_ref[...] = jnp.exp(-jnp.exp(w_log))
    a = jax.nn.sigmoid(a0_ref[...] + lwa[:, MIX_A:2 * MIX_A])
    g_ref[...] = jnp.dot(jax.nn.sigmoid(xg).astype(bf16), g2_ref[...].astype(bf16), preferred_element_type=f32)
    g1, g1t = g1_ref[...], g1t_ref[...]
    kk = k * akk_ref[...]
    nrm = jnp.sqrt(_head_sum(kk * kk, g1, g1t))
    kk = kk / jnp.maximum(nrm, 1e-12)
    k = k * (1.0 + (a - 1.0) * aka_ref[...])
    r_ref[...] = r
    k_ref[...] = k
    v_ref[...] = v
    kk_ref[...] = kk
    kka_ref[...] = kk * a
    bonus_ref[...] = _head_sum(r * k * rk_ref[...], g1, g1t) * v


def _rwkv_prep(z, shift_buf, bsz, t_len, p, wwa, g1, g1t):
    n = bsz * t_len
    row = lambda a: a.reshape(1, -1)
    consts = [row(p['a_mu']), row(p['a_w0']), wwa, row(p['a_a0']), p['a_g2'], row(p['a_kk']), row(p['a_ka']),
              row(p['a_rk']), g1, g1t]
    carry_mode = t_len % TM_EW == 0
    if carry_mode:
        per = t_len // TM_EW
        grid = (bsz, per)
        tok = lambda w: pl.BlockSpec((TM_EW, w), lambda b, i: (b * per + i, 0))
        first = shift_buf.reshape(bsz, 1, A_SHIFT_COLS)
        first_spec = pl.BlockSpec((1, 1, A_SHIFT_COLS), lambda b, i: (b, 0, 0))
        scratch = [pltpu.VMEM((SUBLANES, A_SHIFT_COLS), f32)]
        sem = ("arbitrary", "arbitrary")
    else:
        assert TM_EW % t_len == 0 and n % TM_EW == 0
        grid = (n // TM_EW,)
        tok = lambda w: pl.BlockSpec((TM_EW, w), lambda i: (i, 0))
        first = jnp.repeat(shift_buf, t_len, axis=0)
        first_spec = tok(A_SHIFT_COLS)
        scratch = []
        sem = ("arbitrary",)
    full = lambda a: pl.BlockSpec(a.shape, lambda *_: (0,) * a.ndim)
    return pl.pallas_call(
        functools.partial(_rwkv_prep_kernel, t_len=t_len, carry_mode=carry_mode),
        grid=grid,
        in_specs=[tok(A_SHIFT_COLS), first_spec] + [full(a) for a in consts],
        out_specs=[tok(MIX_A)] * 8,
        out_shape=[_sds((n, MIX_A))] * 8,
        scratch_shapes=scratch,
        compiler_params=_params(*sem),
        name="rwkv_prep",
    )(z, first, *consts)


def _rwkv_scan_kernel(r_ref, d_ref, k_ref, kk_ref, kka_ref, v_ref, s0_ref, y_ref, so_ref, *wide, n_rows, t_chunk,
                      dup):
    @pl.when(pl.program_id(1) == 0)
    def _():
        so_ref[...] = s0_ref[...]

    key_refs = (kk_ref, d_ref, kka_ref, k_ref, r_ref)
    if dup > 1:
        for ref, w_ref in zip(key_refs, wide):
            x = ref[...]
            w_ref[...] = jnp.concatenate([x] * dup, axis=-1)
        key_refs = wide

    def step(t, carry):
        kk, d, kka, k, r = (ref[t] for ref in key_refs)
        for i in range(n_rows):
            s_i = so_ref[i]
            sa = -jnp.sum(s_i * kk, axis=0, keepdims=True)
            v_i = v_ref[t, pl.ds(i, 1), :]
            s_n = s_i * d + sa * kka + v_i * k
            so_ref[i] = s_n
            y_ref[t, pl.ds(i, 1), :] = jnp.sum(s_n * r, axis=0, keepdims=True)
        return carry

    lax.fori_loop(0, t_chunk, step, 0)


def _rwkv_scan(r, d, k, kk, kka, v, s0, t_chunk):
    t_len, n_rows, ltot = v.shape
    dup = ltot // r.shape[2]
    kl = LANES // dup
    jspec = pl.BlockSpec((t_chunk, A_HEAD, kl), lambda l, t: (t, 0, l))
    vspec = pl.BlockSpec((t_chunk, n_rows, LANES), lambda l, t: (t, 0, l))
    sspec = pl.BlockSpec((n_rows, A_HEAD, LANES), lambda l, t: (0, 0, l))
    return pl.pallas_call(
        functools.partial(_rwkv_scan_kernel, n_rows=n_rows, t_chunk=t_chunk, dup=dup),
        grid=(ltot // LANES, t_len // t_chunk),
        in_specs=[jspec] * 5 + [vspec, sspec],
        out_specs=[vspec, sspec],
        out_shape=[_sds((t_len, n_rows, ltot)), _sds((n_rows, A_HEAD, ltot))],
        scratch_shapes=[pltpu.VMEM((t_chunk, A_HEAD, LANES), f32)] * (5 if dup > 1 else 0),
        compiler_params=_params("arbitrary", "arbitrary"),
        name="rwkv_scan",
    )(r, d, k, kk, kka, v, s0)


def _rwkv_post_kernel(y_ref, bonus_ref, g_ref, lnw_ref, lnb_ref, g1_ref, g1t_ref, o_ref):
    y = y_ref[...]
    g1, g1t = g1_ref[...], g1t_ref[...]
    mu = _head_sum(y, g1, g1t) * (1.0 / A_HEAD)
    yc = y - mu
    var = _head_sum(yc * yc, g1, g1t) * (1.0 / A_HEAD)
    yn = yc * lax.rsqrt(var + A_GN_EPS) * lnw_ref[...] + lnb_ref[...]
    o_ref[...] = (yn + bonus_ref[...]) * g_ref[...]


def _rwkv_post(y, bonus, g, p, g1, g1t):
    n = y.shape[0]
    tok = pl.BlockSpec((TM_EW, MIX_A), lambda i: (i, 0))
    full = lambda a: pl.BlockSpec(a.shape, lambda i: (0,) * a.ndim)
    consts = [p['a_lnw'].reshape(1, -1), p['a_lnb'].reshape(1, -1), g1, g1t]
    return pl.pallas_call(
        _rwkv_post_kernel,
        grid=(n // TM_EW,),
        in_specs=[tok] * 3 + [full(a) for a in consts],
        out_specs=tok, out_shape=_sds((n, MIX_A)),
        compiler_params=_params("arbitrary"),
        name="rwkv_post",
    )(y, bonus, g, *consts)


def _rwkv_mix(z, bsz, t_len, shift_buf, wkv0, p, wwa, g1, g1t):
    n = bsz * t_len
    r, k, v, d, kk, kka, g, bonus = _rwkv_prep(z, shift_buf, bsz, t_len, p, wwa, g1, g1t)

    bh = bsz * A_HEADS
    pack = LANES // bh if bh < LANES else 1
    n_rows = A_HEAD // pack

    def key_major(a):
        return a.reshape(bsz, t_len, A_HEADS, A_HEAD).transpose(1, 3, 0, 2).reshape(t_len, A_HEAD, bh)

    def val_major(a):
        a = a.reshape(bsz, t_len, A_HEADS, pack, n_rows).transpose(1, 4, 3, 0, 2)
        return a.reshape(t_len, n_rows, pack * bh)

    s0 = wkv0.reshape(bsz, A_HEADS, pack, n_rows, A_HEAD).transpose(3, 4, 2, 0, 1).reshape(n_rows, A_HEAD, pack * bh)
    t_chunk = min(t_len, 64)
    y, s_out = _rwkv_scan(key_major(r), key_major(d), key_major(k), key_major(kk), key_major(kka),
                          val_major(v), s0, t_chunk)
    y = y.reshape(t_len, n_rows, pack, bsz, A_HEADS).transpose(3, 0, 4, 2, 1).reshape(n, MIX_A)
    new_wkv = s_out.reshape(n_rows, A_HEAD, pack, bsz, A_HEADS).transpose(3, 4, 2, 0, 1)
    new_wkv = new_wkv.reshape(bsz, A_HEADS, A_HEAD, A_HEAD)
    y = _rwkv_post(y, bonus, g, p, g1, g1t)
    new_shift = z.reshape(bsz, t_len, Z_MAIN_COLS)[:, -1, :A_SHIFT_COLS]
    return y, new_shift, new_wkv


def _log_sigmoid(x):
    return jnp.minimum(x, 0.0) - jnp.log(1.0 + jnp.exp(-jnp.abs(x)))


def _mlstm_kernel(*refs, chunk):
    zq_refs = refs[0:2 * B_HEADS]
    v_refs = refs[2 * B_HEADS:3 * B_HEADS]
    o_refs = refs[3 * B_HEADS:4 * B_HEADS]
    (cw_ref, cb_ref, cbuf_ref, gc_ref, gr_ref, bias_r_ref, bias_c_ref, ng_ref, c0_ref, n0_ref, m0_ref,
     y_ref, c_ref, n_ref, m_ref, tail_ref) = refs[4 * B_HEADS:]

    @pl.when(pl.program_id(1) == 0)
    def _():
        c_ref[...] = c0_ref[...]
        n_ref[...] = n0_ref[...]
        m_ref[...] = m0_ref[...]
        tail_ref[...] = jnp.zeros_like(tail_ref)
        tail_ref[SUBLANES - (B_CONV - 1):SUBLANES, :] = cbuf_ref[0]

    def conv_silu(cb):
        cols = slice(cb * B_DQK, (cb + 1) * B_DQK)
        blk = zq_refs[cb][...]
        zp = jnp.concatenate([tail_ref[:, cols], blk], axis=0)
        acc = cb_ref[:, cols]
        for w in range(B_CONV):
            off = SUBLANES - (B_CONV - 1) + w
            acc = acc + zp[off:off + chunk, :] * cw_ref[w:w + 1, cols]
        tail_ref[SUBLANES - (B_CONV - 1):SUBLANES, cols] = blk[chunk - (B_CONV - 1):chunk, :]
        return acc * jax.nn.sigmoid(acc)

    gcol = gc_ref[...] + bias_r_ref[...]
    grow = gr_ref[0] + bias_c_ref[...]
    ti = lax.broadcasted_iota(jnp.int32, (chunk, chunk), 0)
    si = lax.broadcasted_iota(jnp.int32, (chunk, chunk), 1)
    tri = si <= ti
    nt = (((1,), (1,)), ((), ()))
    tn = (((0,), (0,)), ((), ()))
    for h in range(B_HEADS):
        q = conv_silu(h)
        k = conv_silu(B_HEADS + h) * (B_DQK ** -0.5)
        v = v_refs[h][...]
        i_col, f_col = gcol[:, h:h + 1], gcol[:, B_HEADS + h:B_HEADS + h + 1]
        i_row, f_row = grow[h:h + 1, :], grow[B_HEADS + h:B_HEADS + h + 1, :]
        lf_col, lf_row = _log_sigmoid(f_col), _log_sigmoid(f_row)
        b_col = jnp.sum(jnp.where(tri, lf_row, 0.0), axis=1, keepdims=True)
        b_row = jnp.sum(jnp.where(ti <= si, lf_col, 0.0), axis=0, keepdims=True)
        m_prev = m_ref[0, h:h + 1, :]
        log_d = jnp.where(tri, b_col - b_row + i_row, -jnp.inf)
        inter = b_col + m_prev
        m_t = jnp.maximum(inter, jnp.max(log_d, axis=1, keepdims=True))
        qb, kb, vb = q.astype(bf16), k.astype(bf16), v.astype(bf16)
        s = lax.dot_general(qb, kb, nt, preferred_element_type=f32) * jnp.exp(log_d - m_t)
        w_inter = jnp.exp(inter - m_t)
        c_old = c_ref[0, h]
        n_old = n_ref[0, h:h + 1, :]
        num = (jnp.dot(s.astype(bf16), vb, preferred_element_type=f32)
               + w_inter * jnp.dot(qb, c_old.astype(bf16), preferred_element_type=f32))
        den = jnp.sum(s, axis=1, keepdims=True) + w_inter * jnp.sum(q * n_old, axis=1, keepdims=True)
        hh = num / jnp.maximum(jnp.abs(den), jnp.exp(-m_t))
        m_new = m_t[chunk - 1:chunk, :]
        b_last = b_col[chunk - 1:chunk, :]
        ws = jnp.exp(b_last - b_col + i_col - m_new)
        d_c = jnp.exp(b_last + m_prev - m_new)
        kw = k * ws
        c_ref[0, h] = d_c * c_old + lax.dot_general(kw.astype(bf16), vb, tn, preferred_element_type=f32)
        n_ref[0, h:h + 1, :] = d_c * n_old + jnp.sum(kw, axis=0, keepdims=True)
        m_ref[0, h:h + 1, :] = m_new
        hn = hh * lax.rsqrt(jnp.mean(hh * hh, axis=1, keepdims=True) + NORM_EPS)
        y_ref[:, h * B_DV:(h + 1) * B_DV] = (hn * ng_ref[:, h * B_DV:(h + 1) * B_DV]
                                              * jax.nn.sigmoid(o_refs[h][...]))


def _mlstm(z, conv_buf, gates_col, gates_row, p, c0, n0, m0, bsz, n_chunks, chunk):
    n = z.shape[0]
    qk_col0 = A_SHIFT_COLS // B_DQK
    v_col0 = (A_SHIFT_COLS + B_QK_COLS) // B_DV
    o_col0 = v_col0 + B_HEADS
    tok = lambda w: pl.BlockSpec((chunk, w), lambda b, c: (b * n_chunks + c, 0))
    zcol = lambda w, cb: pl.BlockSpec((chunk, w), lambda b, c: (b * n_chunks + c, cb))
    full = lambda a: pl.BlockSpec(a.shape, lambda b, c: (0,) * a.ndim)
    bias = jnp.concatenate([p['b_ig_bias'], p['b_fg_bias']])
    bias_r = jnp.pad(bias, (0, LANES - 2 * B_HEADS)).reshape(1, LANES)
    bias_c = bias.reshape(2 * B_HEADS, 1)
    ng = p['b_norm_g'].reshape(1, -1)
    cw, cb = p['b_conv_w'], p['b_conv_b'].reshape(1, -1)
    cspec = pl.BlockSpec((1, B_HEADS, B_DQK, B_DV), lambda b, c: (b, 0, 0, 0))
    nspec = pl.BlockSpec((1, B_HEADS, B_DQK), lambda b, c: (b, 0, 0))
    mspec = pl.BlockSpec((1, B_HEADS, 1), lambda b, c: (b, 0, 0))
    return pl.pallas_call(
        functools.partial(_mlstm_kernel, chunk=chunk),
        grid=(bsz, n_chunks),
        in_specs=[zcol(B_DQK, qk_col0 + j) for j in range(2 * B_HEADS)]
                 + [zcol(B_DV, v_col0 + h) for h in range(B_HEADS)]
                 + [zcol(B_DV, o_col0 + h) for h in range(B_HEADS)]
                 + [full(cw), full(cb), pl.BlockSpec((1, B_CONV - 1, B_QK_COLS), lambda b, c: (b, 0, 0)),
                    tok(LANES), pl.BlockSpec((1, 2 * B_HEADS, chunk), lambda b, c: (b * n_chunks + c, 0, 0)),
                    full(bias_r), full(bias_c), full(ng), cspec, nspec, mspec],
        out_specs=[tok(B_HEADS * B_DV), cspec, nspec, mspec],
        out_shape=[_sds((n, B_HEADS * B_DV)), _sds(c0.shape), _sds(n0.shape), _sds(m0.shape)],
        scratch_shapes=[pltpu.VMEM((SUBLANES, B_QK_COLS), f32)],
        compiler_params=_params("arbitrary", "arbitrary"),
        name="mlstm",
    )(*([z] * (4 * B_HEADS)), cw, cb, conv_buf, gates_col, gates_row, bias_r, bias_c, ng, c0, n0, m0)


def _mlstm_mix(z, gates, bsz, t_len, conv_buf, c0, n0, m0, p):
    n = bsz * t_len
    z3 = z.reshape(bsz, t_len, Z_MAIN_COLS)
    zqk = z3[:, :, A_SHIFT_COLS:A_SHIFT_COLS + B_QK_COLS]
    new_conv = jnp.concatenate([conv_buf, zqk[:, max(t_len - (B_CONV - 1), 0):]], axis=1)[:, -(B_CONV - 1):]
    chunk = B_CHUNK if t_len % B_CHUNK == 0 else SUBLANES
    t_pad = -(-t_len // chunk) * chunk
    g3 = gates.reshape(bsz, t_len, LANES)
    if t_pad != t_len:
        z = jnp.pad(z3, ((0, 0), (0, t_pad - t_len), (0, 0))).reshape(bsz * t_pad, -1)
        lane = jnp.arange(LANES)
        fill = jnp.where(lane < B_HEADS, -1e30, jnp.where(lane < 2 * B_HEADS, 1e30, 0.0)).astype(f32)
        g3 = jnp.concatenate([g3, jnp.broadcast_to(fill, (bsz, t_pad - t_len, LANES))], axis=1)
    n_chunks = t_pad // chunk
    gates_col = g3.reshape(bsz * t_pad, LANES)
    gates_row = g3[:, :, :2 * B_HEADS].reshape(bsz * n_chunks, chunk, 2 * B_HEADS).transpose(0, 2, 1)
    y, c_new, n_new, m_new = _mlstm(z, conv_buf, gates_col, gates_row, p, c0, n0, m0.reshape(bsz, B_HEADS, 1),
                                    bsz, n_chunks, chunk)
    if t_pad != t_len:
        y = y.reshape(bsz, t_pad, -1)[:, :t_len].reshape(n, -1)
    return y, new_conv, c_new, n_new, m_new.reshape(bsz, B_HEADS)


def _oproj_kernel(ya_ref, yb_ref, wt_ref, wb_ref, x_ref, ga_ref, o_ref):
    acc = jnp.dot(ya_ref[...].astype(bf16), wt_ref[...], preferred_element_type=f32)
    acc = acc + jnp.dot(yb_ref[...].astype(bf16), wb_ref[...], preferred_element_type=f32)
    o_ref[...] = x_ref[...] + ga_ref[0] * acc


def _oproj(ya, yb, w_out_bf, x, ga3, idx):
    n = x.shape[0]
    half = D_MODEL // 2
    tok = lambda w: pl.BlockSpec((TM, w), lambda i: (i, 0))
    return pl.pallas_call(
        _oproj_kernel,
        grid=(n // TM,),
        in_specs=[tok(half), tok(half),
                  pl.BlockSpec((half, D_MODEL), lambda i: (0, 0)), pl.BlockSpec((half, D_MODEL), lambda i: (1, 0)),
                  tok(D_MODEL), _mod_spec(ga3, idx)],
        out_specs=tok(D_MODEL),
        out_shape=_sds((n, D_MODEL)),
        compiler_params=_params("arbitrary"),
        name="out_proj",
    )(ya, yb, w_out_bf, w_out_bf, x, ga3)


def _route_kernel(q_ref, keys_ref, cnt_ref, e1_ref, r2_ref, e2_ref, v1_scr, v2_scr):
    q = q_ref[...]
    tm = q.shape[0]
    neg_inf = -jnp.inf
    iota_k = lax.broadcasted_iota(jnp.int32, (PEER_NKEYS, tm), 0).astype(f32)
    nt = (((1,), (1,)), ((), ()))
    ranks, es = [], []
    for half, vscr in enumerate((v1_scr, v2_scr)):
        qp = q[:, half * 128:(half + 1) * 128].astype(bf16)
        s = lax.dot_general(keys_ref[half, 0].astype(bf16), qp, nt, preferred_element_type=f32)
        rank = jnp.full((PEER_NKEYS, tm), float(PEER_TOPK), f32)
        work = s
        for a in range(PEER_TOPK):
            m = jnp.max(work, axis=0, keepdims=True)
            idx = jnp.min(jnp.where(work == m, iota_k, float(PEER_NKEYS)), axis=0, keepdims=True)
            sel = iota_k == idx
            rank = jnp.where(sel, float(a), rank)
            work = jnp.where(sel, neg_inf, work)
            vscr[a:a + 1, :] = m
        ranks.append(rank)
        es.append(jnp.exp(s - vscr[0:1, :]))
    v1, v2 = v1_scr[...], v2_scr[...]
    row8 = lax.broadcasted_iota(jnp.int32, (SUBLANES, tm), 0).astype(f32)
    row16 = lax.broadcasted_iota(jnp.int32, (PEER_TOPK, tm), 0).astype(f32)
    pieces, poss = [v1[0:1, :] + v2], [row16]
    for a in range(1, 8):
        nb = PEER_TOPK // (a + 1)
        pieces.append(jnp.where(row8 < float(nb), v1[a:a + 1, :] + v2[0:SUBLANES, :], neg_inf))
        poss.append(row8 + float(a * PEER_TOPK))
    pieces.append(v1[8:16, :] + v2[0:1, :])
    poss.append((row8 + 8.0) * float(PEER_TOPK))
    cand = jnp.concatenate(pieces, axis=0)
    pos = jnp.concatenate(poss, axis=0)
    cnt16 = jnp.zeros((PEER_TOPK, tm), f32)
    zsum = None
    best0 = None
    for j in range(PEER_TOPK):
        m = jnp.max(cand, axis=0, keepdims=True)
        pj = jnp.min(jnp.where(cand == m, pos, 1e9), axis=0, keepdims=True)
        cand = jnp.where(pos == pj, neg_inf, cand)
        if j == 0:
            best0 = m
            zsum = jnp.ones_like(m)
        else:
            zsum = zsum + jnp.exp(m - best0)
        a_j = jnp.floor(pj * (1.0 / PEER_TOPK))
        cnt16 = cnt16 + jnp.where(row16 == a_j, 1.0, 0.0)
    cnt_i = jnp.zeros((PEER_NKEYS, tm), f32)
    for a in range(PEER_TOPK):
        cnt_i = jnp.where(ranks[0] == float(a), cnt16[a:a + 1, :], cnt_i)
    cnt_ref[0] = cnt_i
    e1_ref[0] = es[0] / zsum
    r2_ref[0] = ranks[1].astype(bf16)
    e2_ref[0] = es[1].astype(bf16)


def _route(q, keys):
    n = q.shape[0]
    ospec = pl.BlockSpec((1, PEER_NKEYS, TM_ROUTE), lambda i, h: (h, 0, i))
    oshape = (PEER_HEADS, PEER_NKEYS, n)
    return pl.pallas_call(
        _route_kernel,
        grid=(n // TM_ROUTE, PEER_HEADS),
        in_specs=[pl.BlockSpec((TM_ROUTE, 256), lambda i, h: (i, h)),
                  pl.BlockSpec((2, 1, PEER_NKEYS, 128), lambda i, h: (0, h, 0, 0))],
        out_specs=[ospec] * 4,
        out_shape=[_sds(oshape), _sds(oshape), _sds(oshape, bf16), _sds(oshape, bf16)],
        scratch_shapes=[pltpu.VMEM((PEER_TOPK, TM_ROUTE), f32), pltpu.VMEM((PEER_TOPK, TM_ROUTE), f32)],
        compiler_params=_params("arbitrary", "arbitrary"),
        name="peer_route",
    )(q, keys)


def _dense_kernel(hn_ref, u_ref, vt_ref, cnt_ref, e1_ref, r2_ref, e2_ref, o_ref, xt_scr, acc_scr, *, n_chunks):
    c = pl.program_id(1)

    @pl.when(c == 0)
    def _():
        xt_scr[...] = hn_ref[...].T
        acc_scr[...] = jnp.zeros_like(acc_scr)

    ht = jnp.dot(u_ref[...], xt_scr[...], preferred_element_type=f32)
    zero = jnp.zeros((), bf16)
    total = None
    for s in range(TE // TE_SUB):
        parts = []
        for row in range(s * (TE_SUB // PEER_NKEYS), (s + 1) * (TE_SUB // PEER_NKEYS)):
            hs = ht[row * PEER_NKEYS:(row + 1) * PEER_NKEYS, :]
            gate = None
            for h in range(PEER_HEADS):
                cnt_row = cnt_ref[h, row:row + 1, :].astype(bf16)
                e1_row = e1_ref[h, row:row + 1, :].astype(bf16)
                term = jnp.where(r2_ref[h] < cnt_row, e2_ref[h] * e1_row, zero)
                gate = term if gate is None else gate + term
            gelu = 0.5 * hs * (1.0 + lax.erf(hs * (2.0 ** -0.5)))
            parts.append(gelu.astype(bf16) * gate)
        act = jnp.concatenate(parts, axis=0)
        part = jnp.dot(vt_ref[:, s * TE_SUB:(s + 1) * TE_SUB], act, preferred_element_type=f32)
        total = part if total is None else part + total
    acc_scr[...] += total

    @pl.when(c == n_chunks - 1)
    def _():
        o_ref[...] = acc_scr[...].T


def _peer_dense(hn, u_bf, vt_bf, cnt, e1, r2, e2):
    n = hn.shape[0]
    n_exp = u_bf.shape[0]
    n_chunks = n_exp // TE
    sub = TE // PEER_NKEYS
    return pl.pallas_call(
        functools.partial(_dense_kernel, n_chunks=n_chunks),
        grid=(n // TM, n_chunks),
        in_specs=[pl.BlockSpec((TM, D_MODEL), lambda i, c: (i, 0)),
                  pl.BlockSpec((TE, D_MODEL), lambda i, c: (c, 0)),
                  pl.BlockSpec((D_MODEL, TE), lambda i, c: (0, c)),
                  pl.BlockSpec((PEER_HEADS, sub, TM), lambda i, c: (0, c, i)),
                  pl.BlockSpec((PEER_HEADS, sub, TM), lambda i, c: (0, c, i)),
                  pl.BlockSpec((PEER_HEADS, PEER_NKEYS, TM), lambda i, c: (0, 0, i)),
                  pl.BlockSpec((PEER_HEADS, PEER_NKEYS, TM), lambda i, c: (0, 0, i))],
        out_specs=pl.BlockSpec((TM, D_MODEL), lambda i, c: (i, 0)),
        out_shape=_sds((n, D_MODEL)),
        scratch_shapes=[pltpu.VMEM((D_MODEL, TM), bf16), pltpu.VMEM((D_MODEL, TM), f32)],
        compiler_params=_params("arbitrary", "arbitrary"),
        name="peer_dense",
    )(hn, u_bf, vt_bf, cnt, e1, r2, e2)


def _final_kernel(x_ref, f_ref, gc_ref, g_ref, sc_ref, sh_ref, o_ref):
    x = x_ref[...] + gc_ref[0] * f_ref[...]
    y = x * lax.rsqrt(jnp.mean(x * x, axis=-1, keepdims=True) + NORM_EPS) * g_ref[...]
    o_ref[...] = y * (1.0 + sc_ref[0]) + sh_ref[0]


def _final(x, ffn, gc3, g, sc3, sh3, idx):
    n = x.shape[0]
    tok = pl.BlockSpec((TM, D_MODEL), lambda i: (i, 0))
    return pl.pallas_call(
        _final_kernel,
        grid=(n // TM,),
        in_specs=[tok, tok, _mod_spec(gc3, idx), pl.BlockSpec((1, D_MODEL), lambda i: (0, 0)),
                  _mod_spec(sc3, idx), _mod_spec(sh3, idx)],
        out_specs=tok, out_shape=_sds((n, D_MODEL)),
        compiler_params=_params("arbitrary"),
        name="final_norm",
    )(x, ffn, gc3, g.reshape(1, D_MODEL), sc3, sh3)


def _trunk(x, mod, mod_f, states, p, shared):
    bsz, t_len, _ = x.shape
    n = bsz * t_len
    shift_s, wkv_s, conv_s, c_s, n_s, m_s = states
    xf = x.reshape(n, D_MODEL)
    m3 = [_mod3(m, bsz, t_len, TM) for m in jnp.split(mod, 6, axis=-1)]
    idx = m3[0][1]
    sh_a, sc_a, g_a, sh_c, sc_c, g_c = (a for a, _ in m3)
    (shift_f, _), (scale_f, _) = (_mod3(m, bsz, t_len, TM) for m in jnp.split(mod_f, 2, axis=-1))

    z, gates = _norm_mod_matmul(xf, p['norm_mix_g'], sc_a, sh_a, idx, shared['w_in_bf'], Z_MAIN_COLS, 1280,
                                w_gate=shared['w_gate'], name="in_proj")
    y_a, new_shift, new_wkv = _rwkv_mix(z, bsz, t_len, shift_s, wkv_s, p, shared['wwa'], shared['g1'],
                                        shared['g1t'])
    y_b, new_conv, new_c, new_n, new_m = _mlstm_mix(z, gates, bsz, t_len, conv_s, c_s, n_s, m_s, p)
    x1 = _oproj(y_a, y_b, shared['w_out_bf'], xf, g_a, idx)
    q, hn = _norm_mod_matmul(x1, p['norm_ffn_g'], sc_c, sh_c, idx, shared['wq_bf'], D_MODEL, D_MODEL,
                             emit_hn=True, name="peer_query")
    cnt, e1, r2, e2 = _route(q, p['peer_keys'])
    ffn = _peer_dense(hn, shared['u_bf'], shared['vt_bf'], cnt, e1, r2, e2)
    y = _final(x1, ffn, g_c, p['norm_final_g'], scale_f, shift_f, idx)
    return y.reshape(bsz, t_len, D_MODEL), (new_shift, new_wkv, new_conv, new_c, new_n, new_m)


def kernel(x_prompt, x_sample, c_prompt, c_sample, state_rwkv_shift, state_rwkv_wkv, state_mlstm_conv,
           state_mlstm_C, state_mlstm_n, state_mlstm_m, w_ada, b_ada, norm_mix_g, w_in, a_mu, a_w0, a_w2,
           a_a0, a_a2, a_g2, a_kk, a_ka, a_rk, a_lnw, a_lnb, b_conv_w, b_conv_b, b_ig_bias, b_fg_bias,
           b_norm_g, w_out, norm_ffn_g, peer_wq, peer_keys, peer_u, peer_v, w_ada_final, b_ada_final,
           norm_final_g):
    layer = dict(w_ada=w_ada, b_ada=b_ada, norm_mix_g=norm_mix_g, w_in=w_in, a_mu=a_mu, a_w0=a_w0, a_w2=a_w2,
                 a_a0=a_a0, a_a2=a_a2, a_g2=a_g2, a_kk=a_kk, a_ka=a_ka, a_rk=a_rk, a_lnw=a_lnw, a_lnb=a_lnb,
                 b_conv_w=b_conv_w, b_conv_b=b_conv_b, b_ig_bias=b_ig_bias, b_fg_bias=b_fg_bias,
                 b_norm_g=b_norm_g, w_out=w_out, norm_ffn_g=norm_ffn_g, peer_wq=peer_wq, peer_keys=peer_keys,
                 peer_u=peer_u, peer_v=peer_v)
    assert w_ada.shape[0] == 1, "single-layer trunk"
    p = {k: v[0] for k, v in layer.items()}
    p['norm_final_g'] = norm_final_g
    bp, bs = x_prompt.shape[0], x_sample.shape[0]

    c_all = jnp.concatenate([c_prompt, c_sample], axis=0)
    c_all = jnp.pad(c_all, ((0, -c_all.shape[0] % SUBLANES), (0, 0)))
    mod = _ada(c_all, p['w_ada'], p['b_ada'])
    mod_f = _ada(c_all, w_ada_final, b_ada_final)

    head_of = jnp.arange(MIX_A) // A_HEAD
    g1 = (head_of[:, None] == jnp.arange(LANES)[None, :]).astype(bf16)
    zero = jnp.zeros((64, MIX_A), f32)
    wwa = jnp.concatenate([jnp.concatenate([p['a_w2'], zero], axis=1),
                           jnp.concatenate([zero, p['a_a2']], axis=1)], axis=0).astype(bf16)
    w_in_bf = p['w_in'].astype(bf16)
    shared = dict(
        g1=g1, g1t=g1.T, wwa=wwa, w_in_bf=w_in_bf,
        w_gate=jnp.pad(w_in_bf[:, Z_MAIN_COLS:], ((0, 0), (0, LANES - 2 * B_HEADS))),
        w_out_bf=p['w_out'].astype(bf16), wq_bf=p['peer_wq'].astype(bf16),
        u_bf=p['peer_u'].astype(bf16), vt_bf=p['peer_v'].T.astype(bf16))

    dt = x_prompt.dtype
    zero_states = (jnp.zeros((bp, A_SHIFT_COLS), dt), jnp.zeros((bp, A_HEADS, A_HEAD, A_HEAD), dt),
                   jnp.zeros((bp, B_CONV - 1, B_QK_COLS), dt), jnp.zeros((bp, B_HEADS, B_DQK, B_DV), dt),
                   jnp.zeros((bp, B_HEADS, B_DQK), dt), jnp.zeros((bp, B_HEADS), dt))
    y_prompt, p_states = _trunk(x_prompt, mod[:bp], mod_f[:bp], zero_states, p, shared)
    s_states_in = (state_rwkv_shift[0], state_rwkv_wkv[0], state_mlstm_conv[0], state_mlstm_C[0],
                   state_mlstm_n[0], state_mlstm_m[0])
    y_sample, s_states = _trunk(x_sample, mod[bp:bp + bs], mod_f[bp:bp + bs], s_states_in, p, shared)
    return (y_prompt, y_sample) + tuple(s[None] for s in p_states) + tuple(s[None] for s in s_states)
```

```python
import functools

import jax
import jax.numpy as jnp
from jax import lax
from jax.experimental import pallas as pl
from jax.experimental.pallas import tpu as pltpu

f32 = jnp.float32
bf16 = jnp.bfloat16

D_MODEL = 2048
MIX_A = 1024
A_HEAD = 64
A_HEADS = 16
A_SHIFT_COLS = 3 * MIX_A + 64 + 64 + 128
A_GN_EPS = A_HEAD * 1e-5
B_QK_COLS = 1024
B_HEADS = 4
B_DV = 256
B_DQK = 128
B_CONV = 4
B_CHUNK = 64
Z_MAIN_COLS = A_SHIFT_COLS + B_QK_COLS + 2 * 1024
PEER_HEADS = 8
PEER_NKEYS = 128
PEER_TOPK = 16
NORM_EPS = 1e-6

LANES = 128
SUBLANES = 8
VMEM_LIMIT_BYTES = 56 * 1024 * 1024

TM = 512
TM_EW = 256
TM_ROUTE = 256
TE = 1024
TE_HALF = TE // 2


def _params(*sem):
    return pltpu.CompilerParams(dimension_semantics=sem, vmem_limit_bytes=VMEM_LIMIT_BYTES)


def _sds(shape, dtype=f32):
    return jax.ShapeDtypeStruct(shape, dtype)


def _ada_kernel(c_ref, w_ref, b_ref, o_ref):
    c = c_ref[...]
    a = (c * jax.nn.sigmoid(c)).astype(bf16)
    o_ref[...] = jnp.dot(a, w_ref[...].astype(bf16), preferred_element_type=f32) + b_ref[...]


def _ada(c, w, b):
    m, n_out, tn = c.shape[0], w.shape[1], 1024
    return pl.pallas_call(
        _ada_kernel,
        grid=(n_out // tn,),
        in_specs=[pl.BlockSpec((m, D_MODEL), lambda j: (0, 0)),
                  pl.BlockSpec((D_MODEL, tn), lambda j: (0, j)),
                  pl.BlockSpec((1, tn), lambda j: (0, j))],
        out_specs=pl.BlockSpec((m, tn), lambda j: (0, j)),
        out_shape=_sds((m, n_out)),
        compiler_params=_params("arbitrary"),
        name="ada_mod",
    )(c, w, b.reshape(1, n_out))


def _mod3(m, bsz, t_len, tm):
    if t_len % tm == 0:
        per = t_len // tm
        return m.reshape(bsz, 1, D_MODEL), (lambda i: i // per)
    n = bsz * t_len
    assert tm % t_len == 0 and n % tm == 0
    return jnp.repeat(m, t_len, axis=0).reshape(n // tm, tm, D_MODEL), (lambda i: i)


def _mod_spec(arr, idx):
    return pl.BlockSpec((1, arr.shape[1], D_MODEL), lambda i, *_: (idx(i), 0, 0))


def _nmm_kernel(*refs, has_gate, emit_hn):
    it = iter(refs)
    x_ref, g_ref, sc_ref, sh_ref, w_ref = (next(it) for _ in range(5))
    wg_ref = next(it) if has_gate else None
    o_ref = next(it)
    og_ref = next(it) if has_gate else None
    hn_ref = next(it) if emit_hn else None
    hs_ref = next(it)

    @pl.when(pl.program_id(1) == 0)
    def _():
        x = x_ref[...]
        y = x * lax.rsqrt(jnp.mean(x * x, axis=-1, keepdims=True) + NORM_EPS) * g_ref[...]
        h = (y * (1.0 + sc_ref[0]) + sh_ref[0]).astype(bf16)
        hs_ref[...] = h
        if emit_hn:
            hn_ref[...] = h
        if has_gate:
            og_ref[...] = jnp.dot(h, wg_ref[...], preferred_element_type=f32)

    o_ref[...] = jnp.dot(hs_ref[...], w_ref[...], preferred_element_type=f32)


def _norm_mod_matmul(x, g, sc3, sh3, idx, w_bf, n_out, tn, w_gate=None, emit_hn=False, name="nmm"):
    n = x.shape[0]
    has_gate = w_gate is not None
    in_specs = [pl.BlockSpec((TM, D_MODEL), lambda i, j: (i, 0)),
                pl.BlockSpec((1, D_MODEL), lambda i, j: (0, 0)),
                _mod_spec(sc3, idx), _mod_spec(sh3, idx),
                pl.BlockSpec((D_MODEL, tn), lambda i, j: (0, j))]
    args = [x, g.reshape(1, D_MODEL), sc3, sh3, w_bf]
    out_specs = [pl.BlockSpec((TM, tn), lambda i, j: (i, j))]
    out_shape = [_sds((n, n_out))]
    if has_gate:
        in_specs.append(pl.BlockSpec((D_MODEL, LANES), lambda i, j: (0, 0)))
        args.append(w_gate)
        out_specs.append(pl.BlockSpec((TM, LANES), lambda i, j: (i, 0)))
        out_shape.append(_sds((n, LANES)))
    if emit_hn:
        out_specs.append(pl.BlockSpec((TM, D_MODEL), lambda i, j: (i, 0)))
        out_shape.append(_sds((n, D_MODEL), bf16))
    return pl.pallas_call(
        functools.partial(_nmm_kernel, has_gate=has_gate, emit_hn=emit_hn),
        grid=(n // TM, n_out // tn),
        in_specs=in_specs, out_specs=out_specs, out_shape=out_shape,
        scratch_shapes=[pltpu.VMEM((TM, D_MODEL), bf16)],
        compiler_params=_params("arbitrary", "arbitrary"),
        name=name,
    )(*args)


def _split_dot(x, w):
    hi = x.astype(bf16)
    lo = (x - hi.astype(f32)).astype(bf16)
    return (jnp.dot(hi, w, preferred_element_type=f32) + jnp.dot(lo, w, preferred_element_type=f32))


def _head_sum(x, g1, g1t):
    return _split_dot(_split_dot(x, g1), g1t)


def _rwkv_prep_kernel(z_ref, first_ref, mu_ref, w0_ref, wwa_ref, a0_ref, g2_ref, akk_ref, aka_ref, rk_ref,
                      g1_ref, g1t_ref, r_ref, k_ref, v_ref, d_ref, kk_ref, kka_ref, g_ref, bonus_ref,
                      *scratch, t_len, carry_mode):
    z = z_ref[...]
    tm = z.shape[0]
    row = lax.broadcasted_iota(jnp.int32, z.shape, 0)
    rolled = pltpu.roll(z, 1, axis=0)
    if carry_mode:
        carry_ref, = scratch
        first = jnp.where(pl.program_id(1) == 0, first_ref[0], carry_ref[0:1, :])
        prev = jnp.where(row == 0, first, rolled)
        carry_ref[0:1, :] = z[tm - 1:tm, :]
    else:
        prev = jnp.where(lax.rem(row, t_len) == 0, first_ref[...], rolled)
    zs = z + (prev - z) * mu_ref[...]
    r = zs[:, 0:MIX_A]
    k = zs[:, MIX_A:2 * MIX_A]
    v = zs[:, 2 * MIX_A:3 * MIX_A]
    xwa = zs[:, 3 * MIX_A:3 * MIX_A + 128]
    xg = zs[:, 3 * MIX_A + 128:3 * MIX_A + 256]
    lane = lax.broadcasted_iota(jnp.int32, xwa.shape, 1)
    xwa = jnp.where(lane < 64, jnp.tanh(xwa), xwa)
    lwa = jnp.dot(xwa.astype(bf16), wwa_ref[...], preferred_element_type=f32)
    u = w0_ref[...] + lwa[:, 0:MIX_A]
    w_log = -(jnp.maximum(-u, 0.0) + jnp.log(1.0 + jnp.exp(-jnp.abs(u)))) - 0.5
    d_ref[...] = jnp.exp(-jnp.exp(w_log))
    a = jax.nn.sigmoid(a0_ref[...] + lwa[:, MIX_A:2 * MIX_A])
    g_ref[...] = jnp.dot(jax.nn.sigmoid(xg).astype(bf16), g2_ref[...].astype(bf16), preferred_element_type=f32)
    g1, g1t = g1_ref[...], g1t_ref[...]
    kk = k * akk_ref[...]
    nrm = jnp.sqrt(_head_sum(kk * kk, g1, g1t))
    kk = kk / jnp.maximum(nrm, 1e-12)
    k = k * (1.0 + (a - 1.0) * aka_ref[...])
    r_ref[...] = r
    k_ref[...] = k
    v_ref[...] = v
    kk_ref[...] = kk
    kka_ref[...] = kk * a
    bonus_ref[...] = _head_sum(r * k * rk_ref[...], g1, g1t) * v


def _rwkv_prep(z, shift_buf, bsz, t_len, p, wwa, g1, g1t):
    n = bsz * t_len
    row = lambda a: a.reshape(1, -1)
    consts = [row(p['a_mu']), row(p['a_w0']), wwa, row(p['a_a0']), p['a_g2'], row(p['a_kk']), row(p['a_ka']),
              row(p['a_rk']), g1, g1t]
    carry_mode = t_len % TM_EW == 0
    if carry_mode:
        per = t_len // TM_EW
        grid = (bsz, per)
        tok = lambda w: pl.BlockSpec((TM_EW, w), lambda b, i: (b * per + i, 0))
        first = shift_buf.reshape(bsz, 1, A_SHIFT_COLS)
        first_spec = pl.BlockSpec((1, 1, A_SHIFT_COLS), lambda b, i: (b, 0, 0))
        scratch = [pltpu.VMEM((SUBLANES, A_SHIFT_COLS), f32)]
        sem = ("arbitrary", "arbitrary")
    else:
        assert TM_EW % t_len == 0 and n % TM_EW == 0
        grid = (n // TM_EW,)
        tok = lambda w: pl.BlockSpec((TM_EW, w), lambda i: (i, 0))
        first = jnp.repeat(shift_buf, t_len, axis=0)
        first_spec = tok(A_SHIFT_COLS)
        scratch = []
        sem = ("arbitrary",)
    full = lambda a: pl.BlockSpec(a.shape, lambda *_: (0,) * a.ndim)
    return pl.pallas_call(
        functools.partial(_rwkv_prep_kernel, t_len=t_len, carry_mode=carry_mode),
        grid=grid,
        in_specs=[tok(A_SHIFT_COLS), first_spec] + [full(a) for a in consts],
        out_specs=[tok(MIX_A)] * 8,
        out_shape=[_sds((n, MIX_A))] * 8,
        scratch_shapes=scratch,
        compiler_params=_params(*sem),
        name="rwkv_prep",
    )(z, first, *consts)


SCAN_CHUNK = 64
RELAYOUT_T = 128


def _rwkv_scan_kernel(r_ref, d_ref, k_ref, kk_ref, kka_ref, v_ref, s0_ref, y_ref, so_ref, *, n_rows, t_chunk):
    @pl.when(pl.program_id(1) == 0)
    def _():
        so_ref[...] = s0_ref[...]

    def step(t, carry):
        kk, d, kka, k, r = kk_ref[t], d_ref[t], kka_ref[t], k_ref[t], r_ref[t]
        for i in range(n_rows):
            s_i = so_ref[i]
            sa = -jnp.sum(s_i * kk, axis=0, keepdims=True)
            v_i = v_ref[t, pl.ds(i, 1), :]
            s_n = s_i * d + sa * kka + v_i * k
            so_ref[i] = s_n
            y_ref[t, pl.ds(i, 1), :] = jnp.sum(s_n * r, axis=0, keepdims=True)
        return carry

    lax.fori_loop(0, t_chunk, step, 0)


def _rwkv_scan(r, d, k, kk, kka, v, s0, t_chunk):
    t_len, n_rows, ltot = v.shape
    jspec = pl.BlockSpec((t_chunk, A_HEAD, LANES), lambda l, t: (t, 0, l))
    vspec = pl.BlockSpec((t_chunk, n_rows, LANES), lambda l, t: (t, 0, l))
    sspec = pl.BlockSpec((n_rows, A_HEAD, LANES), lambda l, t: (0, 0, l))
    return pl.pallas_call(
        functools.partial(_rwkv_scan_kernel, n_rows=n_rows, t_chunk=t_chunk),
        grid=(ltot // LANES, t_len // t_chunk),
        in_specs=[jspec] * 5 + [vspec, sspec],
        out_specs=[vspec, sspec],
        out_shape=[_sds(v.shape), _sds((n_rows, A_HEAD, ltot))],
        compiler_params=_params("arbitrary", "arbitrary"),
        name="rwkv_scan",
    )(r, d, k, kk, kka, v, s0)


def _to_scan_kernel(*refs, bsz, pack, val_mode):
    x_refs, o_ref, pt_ref = refs[:bsz], refs[bsz], refs[bsz + 1]
    for b in range(bsz):
        pt_ref[b] = x_refs[b][...].T
    n_idx = A_HEAD // pack if val_mode else A_HEAD
    for j in range(n_idx):
        if val_mode:
            rows = [pt_ref[b, pl.ds(g * n_idx + j, A_HEADS, stride=A_HEAD), :]
                    for g in range(pack) for b in range(bsz)]
        else:
            rows = [pt_ref[b, pl.ds(j, A_HEADS, stride=A_HEAD), :] for b in range(bsz)] * pack
        o_ref[:, j, :] = jnp.concatenate(rows, axis=0).T


def _to_scan_layout(a, bsz, t_len, pack, val_mode):
    per = t_len // RELAYOUT_T
    n_idx = A_HEAD // pack if val_mode else A_HEAD
    return pl.pallas_call(
        functools.partial(_to_scan_kernel, bsz=bsz, pack=pack, val_mode=val_mode),
        grid=(per,),
        in_specs=[pl.BlockSpec((RELAYOUT_T, MIX_A), lambda tb, b=b: (b * per + tb, 0)) for b in range(bsz)],
        out_specs=pl.BlockSpec((RELAYOUT_T, n_idx, LANES), lambda tb: (tb, 0, 0)),
        out_shape=_sds((t_len, n_idx, LANES)),
        scratch_shapes=[pltpu.VMEM((bsz, MIX_A, RELAYOUT_T), f32)],
        compiler_params=_params("arbitrary"),
        name="to_scan_val" if val_mode else "to_scan_key",
    )(*([a] * bsz))


def _from_scan_kernel(y_ref, o_ref, pt_ref, *, bsz, pack):
    n_idx = A_HEAD // pack
    for j in range(n_idx):
        tt = y_ref[:, j, :].T
        for g in range(pack):
            for b in range(bsz):
                lo = (g * bsz + b) * A_HEADS
                pt_ref[b, pl.ds(g * n_idx + j, A_HEADS, stride=A_HEAD), :] = tt[lo:lo + A_HEADS, :]
    for b in range(bsz):
        o_ref[b] = pt_ref[b].T


def _from_scan_layout(y, bsz, t_len, pack):
    per = t_len // RELAYOUT_T
    n_idx = A_HEAD // pack
    out = pl.pallas_call(
        functools.partial(_from_scan_kernel, bsz=bsz, pack=pack),
        grid=(per,),
        in_specs=[pl.BlockSpec((RELAYOUT_T, n_idx, LANES), lambda tb: (tb, 0, 0))],
        out_specs=pl.BlockSpec((bsz, RELAYOUT_T, MIX_A), lambda tb: (0, tb, 0)),
        out_shape=_sds((bsz, t_len, MIX_A)),
        scratch_shapes=[pltpu.VMEM((bsz, MIX_A, RELAYOUT_T), f32)],
        compiler_params=_params("arbitrary"),
        name="from_scan",
    )(y)
    return out.reshape(bsz * t_len, MIX_A)


def _rwkv_post_kernel(y_ref, bonus_ref, g_ref, lnw_ref, lnb_ref, g1_ref, g1t_ref, o_ref):
    y = y_ref[...]
    g1, g1t = g1_ref[...], g1t_ref[...]
    mu = _head_sum(y, g1, g1t) * (1.0 / A_HEAD)
    yc = y - mu
    var = _head_sum(yc * yc, g1, g1t) * (1.0 / A_HEAD)
    yn = yc * lax.rsqrt(var + A_GN_EPS) * lnw_ref[...] + lnb_ref[...]
    o_ref[...] = (yn + bonus_ref[...]) * g_ref[...]


def _rwkv_post(y, bonus, g, p, g1, g1t):
    n = y.shape[0]
    tok = pl.BlockSpec((TM_EW, MIX_A), lambda i: (i, 0))
    full = lambda a: pl.BlockSpec(a.shape, lambda i: (0,) * a.ndim)
    consts = [p['a_lnw'].reshape(1, -1), p['a_lnb'].reshape(1, -1), g1, g1t]
    return pl.pallas_call(
        _rwkv_post_kernel,
        grid=(n // TM_EW,),
        in_specs=[tok] * 3 + [full(a) for a in consts],
        out_specs=tok, out_shape=_sds((n, MIX_A)),
        compiler_params=_params("arbitrary"),
        name="rwkv_post",
    )(y, bonus, g, *consts)


def _rwkv_mix(z, bsz, t_len, shift_buf, wkv0, p, wwa, g1, g1t):
    n = bsz * t_len
    r, k, v, d, kk, kka, g, bonus = _rwkv_prep(z, shift_buf, bsz, t_len, p, wwa, g1, g1t)

    bh = bsz * A_HEADS
    pack = LANES // bh if bh < LANES else 1
    n_rows = A_HEAD // pack

    s0 = wkv0.reshape(bsz, A_HEADS, pack, n_rows, A_HEAD).transpose(3, 4, 2, 0, 1).reshape(n_rows, A_HEAD, pack * bh)
    if pack * bh == LANES and t_len % RELAYOUT_T == 0:
        keys = [_to_scan_layout(a, bsz, t_len, pack, False) for a in (r, d, k, kk, kka)]
        y, s_out = _rwkv_scan(*keys, _to_scan_layout(v, bsz, t_len, pack, True), s0, SCAN_CHUNK)
        y = _from_scan_layout(y, bsz, t_len, pack)
    else:
        assert pack == 1 and t_len <= SCAN_CHUNK

        def scan_rows(a):
            return a.reshape(bsz, t_len, A_HEADS, A_HEAD).transpose(1, 3, 0, 2).reshape(t_len, A_HEAD, bh)

        y, s_out = _rwkv_scan(*(scan_rows(a) for a in (r, d, k, kk, kka, v)), s0, t_len)
        y = y.reshape(t_len, A_HEAD, bsz, A_HEADS).transpose(2, 0, 3, 1).reshape(n, MIX_A)
    new_wkv = s_out.reshape(n_rows, A_HEAD, pack, bsz, A_HEADS).transpose(3, 4, 2, 0, 1)
    new_wkv = new_wkv.reshape(bsz, A_HEADS, A_HEAD, A_HEAD)
    y = _rwkv_post(y, bonus, g, p, g1, g1t)
    new_shift = z.reshape(bsz, t_len, Z_MAIN_COLS)[:, -1, :A_SHIFT_COLS]
    return y, new_shift, new_wkv


def _log_sigmoid(x):
    return jnp.minimum(x, 0.0) - jnp.log(1.0 + jnp.exp(-jnp.abs(x)))


def _mlstm_kernel(*refs, chunk, bb):
    zq_refs = refs[0:2 * B_HEADS]
    v_refs = refs[2 * B_HEADS:3 * B_HEADS]
    o_refs = refs[3 * B_HEADS:4 * B_HEADS]
    (cw_ref, cb_ref, cbuf_ref, gc_ref, gr_ref, bias_r_ref, bias_c_ref, ng_ref, c0_ref, n0_ref, m0_ref,
     y_ref, c_ref, n_ref, m_ref, tail_ref) = refs[4 * B_HEADS:]
    n_tail = B_CONV - 1

    @pl.when(pl.program_id(1) == 0)
    def _():
        c_ref[...] = c0_ref[...]
        n_ref[...] = n0_ref[...]
        m_ref[...] = m0_ref[...]
        tail_ref[...] = jnp.zeros_like(tail_ref)
        for bi in range(bb):
            tail_ref[(bi + 1) * SUBLANES - n_tail:(bi + 1) * SUBLANES, :] = cbuf_ref[bi]

    def conv_silu(cb, bi):
        cols = slice(cb * B_DQK, (cb + 1) * B_DQK)
        blk = zq_refs[cb][bi * chunk:(bi + 1) * chunk, :]
        zp = jnp.concatenate([tail_ref[bi * SUBLANES:(bi + 1) * SUBLANES, cols], blk], axis=0)
        acc = cb_ref[:, cols]
        for w in range(B_CONV):
            off = SUBLANES - n_tail + w
            acc = acc + zp[off:off + chunk, :] * cw_ref[w:w + 1, cols]
        tail_ref[(bi + 1) * SUBLANES - n_tail:(bi + 1) * SUBLANES, cols] = blk[chunk - n_tail:chunk, :]
        return acc * jax.nn.sigmoid(acc)

    ti = lax.broadcasted_iota(jnp.int32, (chunk, chunk), 0)
    si = lax.broadcasted_iota(jnp.int32, (chunk, chunk), 1)
    tri = si <= ti
    nt = (((1,), (1,)), ((), ()))
    tn = (((0,), (0,)), ((), ()))
    for bi in range(bb):
        rows = slice(bi * chunk, (bi + 1) * chunk)
        gcol = gc_ref[rows, :] + bias_r_ref[...]
        grow = gr_ref[bi] + bias_c_ref[...]
        for h in range(B_HEADS):
            q = conv_silu(h, bi)
            k = conv_silu(B_HEADS + h, bi) * (B_DQK ** -0.5)
            v = v_refs[h][rows, :]
            i_col, f_col = gcol[:, h:h + 1], gcol[:, B_HEADS + h:B_HEADS + h + 1]
            i_row, f_row = grow[h:h + 1, :], grow[B_HEADS + h:B_HEADS + h + 1, :]
            lf_col, lf_row = _log_sigmoid(f_col), _log_sigmoid(f_row)
            b_col = jnp.sum(jnp.where(tri, lf_row, 0.0), axis=1, keepdims=True)
            b_row = jnp.sum(jnp.where(ti <= si, lf_col, 0.0), axis=0, keepdims=True)
            m_prev = m_ref[bi, h:h + 1, :]
            log_d = jnp.where(tri, b_col - b_row + i_row, -jnp.inf)
            inter = b_col + m_prev
            m_t = jnp.maximum(inter, jnp.max(log_d, axis=1, keepdims=True))
            qb, kb, vb = q.astype(bf16), k.astype(bf16), v.astype(bf16)
            s = lax.dot_general(qb, kb, nt, preferred_element_type=f32) * jnp.exp(log_d - m_t)
            w_inter = jnp.exp(inter - m_t)
            c_old = c_ref[bi, h]
            n_old = n_ref[bi, h:h + 1, :]
            num = (jnp.dot(s.astype(bf16), vb, preferred_element_type=f32)
                   + w_inter * jnp.dot(qb, c_old.astype(bf16), preferred_element_type=f32))
            den = jnp.sum(s, axis=1, keepdims=True) + w_inter * jnp.sum(q * n_old, axis=1, keepdims=True)
            hh = num / jnp.maximum(jnp.abs(den), jnp.exp(-m_t))
            m_new = m_t[chunk - 1:chunk, :]
            b_last = b_col[chunk - 1:chunk, :]
            ws = jnp.exp(b_last - b_col + i_col - m_new)
            d_c = jnp.exp(b_last + m_prev - m_new)
            kw = k * ws
            c_ref[bi, h] = d_c * c_old + lax.dot_general(kw.astype(bf16), vb, tn, preferred_element_type=f32)
            n_ref[bi, h:h + 1, :] = d_c * n_old + jnp.sum(kw, axis=0, keepdims=True)
            m_ref[bi, h:h + 1, :] = m_new
            hn = hh * lax.rsqrt(jnp.mean(hh * hh, axis=1, keepdims=True) + NORM_EPS)
            y_ref[rows, h * B_DV:(h + 1) * B_DV] = (hn * ng_ref[:, h * B_DV:(h + 1) * B_DV]
                                                    * jax.nn.sigmoid(o_refs[h][rows, :]))


MLSTM_SHORT_BATCH = 4


def _mlstm(z, conv_buf, gates_col, gates_row, p, c0, n0, m0, bsz, n_chunks, chunk):
    n = z.shape[0]
    bb = MLSTM_SHORT_BATCH if (n_chunks == 1 and bsz % MLSTM_SHORT_BATCH == 0) else 1
    qk_col0 = A_SHIFT_COLS // B_DQK
    v_col0 = (A_SHIFT_COLS + B_QK_COLS) // B_DV
    o_col0 = v_col0 + B_HEADS
    tok = lambda w: pl.BlockSpec((bb * chunk, w), lambda b, c: (b * n_chunks + c, 0))
    zcol = lambda w, cb: pl.BlockSpec((bb * chunk, w), lambda b, c: (b * n_chunks + c, cb))
    full = lambda a: pl.BlockSpec(a.shape, lambda b, c: (0,) * a.ndim)
    bias = jnp.concatenate([p['b_ig_bias'], p['b_fg_bias']])
    bias_r = jnp.pad(bias, (0, LANES - 2 * B_HEADS)).reshape(1, LANES)
    bias_c = bias.reshape(2 * B_HEADS, 1)
    ng = p['b_norm_g'].reshape(1, -1)
    cw, cb = p['b_conv_w'], p['b_conv_b'].reshape(1, -1)
    cspec = pl.BlockSpec((bb, B_HEADS, B_DQK, B_DV), lambda b, c: (b, 0, 0, 0))
    nspec = pl.BlockSpec((bb, B_HEADS, B_DQK), lambda b, c: (b, 0, 0))
    mspec = pl.BlockSpec((bb, B_HEADS, 1), lambda b, c: (b, 0, 0))
    return pl.pallas_call(
        functools.partial(_mlstm_kernel, chunk=chunk, bb=bb),
        grid=(bsz // bb, n_chunks),
        in_specs=[zcol(B_DQK, qk_col0 + j) for j in range(2 * B_HEADS)]
                 + [zcol(B_DV, v_col0 + h) for h in range(B_HEADS)]
                 + [zcol(B_DV, o_col0 + h) for h in range(B_HEADS)]
                 + [full(cw), full(cb), pl.BlockSpec((bb, B_CONV - 1, B_QK_COLS), lambda b, c: (b, 0, 0)),
                    tok(LANES), pl.BlockSpec((bb, 2 * B_HEADS, chunk), lambda b, c: (b * n_chunks + c, 0, 0)),
                    full(bias_r), full(bias_c), full(ng), cspec, nspec, mspec],
        out_specs=[tok(B_HEADS * B_DV), cspec, nspec, mspec],
        out_shape=[_sds((n, B_HEADS * B_DV)), _sds(c0.shape), _sds(n0.shape), _sds(m0.shape)],
        scratch_shapes=[pltpu.VMEM((bb * SUBLANES, B_QK_COLS), f32)],
        compiler_params=_params("arbitrary", "arbitrary"),
        name="mlstm",
    )(*([z] * (4 * B_HEADS)), cw, cb, conv_buf, gates_col, gates_row, bias_r, bias_c, ng, c0, n0, m0)


def _mlstm_mix(z, gates, bsz, t_len, conv_buf, c0, n0, m0, p):
    n = bsz * t_len
    z3 = z.reshape(bsz, t_len, Z_MAIN_COLS)
    zqk = z3[:, :, A_SHIFT_COLS:A_SHIFT_COLS + B_QK_COLS]
    new_conv = jnp.concatenate([conv_buf, zqk[:, max(t_len - (B_CONV - 1), 0):]], axis=1)[:, -(B_CONV - 1):]
    chunk = B_CHUNK if t_len % B_CHUNK == 0 else SUBLANES
    t_pad = -(-t_len // chunk) * chunk
    g3 = gates.reshape(bsz, t_len, LANES)
    if t_pad != t_len:
        z = jnp.pad(z3, ((0, 0), (0, t_pad - t_len), (0, 0))).reshape(bsz * t_pad, -1)
        lane = jnp.arange(LANES)
        fill = jnp.where(lane < B_HEADS, -1e30, jnp.where(lane < 2 * B_HEADS, 1e30, 0.0)).astype(f32)
        g3 = jnp.concatenate([g3, jnp.broadcast_to(fill, (bsz, t_pad - t_len, LANES))], axis=1)
    n_chunks = t_pad // chunk
    gates_col = g3.reshape(bsz * t_pad, LANES)
    gates_row = g3[:, :, :2 * B_HEADS].reshape(bsz * n_chunks, chunk, 2 * B_HEADS).transpose(0, 2, 1)
    y, c_new, n_new, m_new = _mlstm(z, conv_buf, gates_col, gates_row, p, c0, n0, m0.reshape(bsz, B_HEADS, 1),
                                    bsz, n_chunks, chunk)
    if t_pad != t_len:
        y = y.reshape(bsz, t_pad, -1)[:, :t_len].reshape(n, -1)
    return y, new_conv, c_new, n_new, m_new.reshape(bsz, B_HEADS)


def _oproj_kernel(ya_ref, yb_ref, wt_ref, wb_ref, x_ref, ga_ref, o_ref):
    acc = jnp.dot(ya_ref[...].astype(bf16), wt_ref[...], preferred_element_type=f32)
    acc = acc + jnp.dot(yb_ref[...].astype(bf16), wb_ref[...], preferred_element_type=f32)
    o_ref[...] = x_ref[...] + ga_ref[0] * acc


def _oproj(ya, yb, w_out_bf, x, ga3, idx):
    n = x.shape[0]
    half = D_MODEL // 2
    tok = lambda w: pl.BlockSpec((TM, w), lambda i: (i, 0))
    return pl.pallas_call(
        _oproj_kernel,
        grid=(n // TM,),
        in_specs=[tok(half), tok(half),
                  pl.BlockSpec((half, D_MODEL), lambda i: (0, 0)), pl.BlockSpec((half, D_MODEL), lambda i: (1, 0)),
                  tok(D_MODEL), _mod_spec(ga3, idx)],
        out_specs=tok(D_MODEL),
        out_shape=_sds((n, D_MODEL)),
        compiler_params=_params("arbitrary"),
        name="out_proj",
    )(ya, yb, w_out_bf, w_out_bf, x, ga3)


def _route_kernel(q_ref, keys_ref, cnt_ref, e1_ref, r2_ref, e2_ref, v1_scr, v2_scr):
    q = q_ref[...]
    tm = q.shape[0]
    neg_inf = -jnp.inf
    iota_k = lax.broadcasted_iota(jnp.int32, (PEER_NKEYS, tm), 0).astype(f32)
    nt = (((1,), (1,)), ((), ()))
    ranks, es = [], []
    for half, vscr in enumerate((v1_scr, v2_scr)):
        qp = q[:, half * 128:(half + 1) * 128].astype(bf16)
        s = lax.dot_general(keys_ref[half, 0].astype(bf16), qp, nt, preferred_element_type=f32)
        rank = jnp.full((PEER_NKEYS, tm), float(PEER_TOPK), f32)
        work = s
        for a in range(PEER_TOPK):
            m = jnp.max(work, axis=0, keepdims=True)
            idx = jnp.min(jnp.where(work == m, iota_k, float(PEER_NKEYS)), axis=0, keepdims=True)
            sel = iota_k == idx
            rank = jnp.where(sel, float(a), rank)
            work = jnp.where(sel, neg_inf, work)
            vscr[a:a + 1, :] = m
        ranks.append(rank)
        es.append(jnp.exp(s - vscr[0:1, :]))
    v1, v2 = v1_scr[...], v2_scr[...]
    row8 = lax.broadcasted_iota(jnp.int32, (SUBLANES, tm), 0).astype(f32)
    row16 = lax.broadcasted_iota(jnp.int32, (PEER_TOPK, tm), 0).astype(f32)
    pieces, poss = [v1[0:1, :] + v2], [row16]
    for a in range(1, 8):
        nb = PEER_TOPK // (a + 1)
        pieces.append(jnp.where(row8 < float(nb), v1[a:a + 1, :] + v2[0:SUBLANES, :], neg_inf))
        poss.append(row8 + float(a * PEER_TOPK))
    pieces.append(v1[8:16, :] + v2[0:1, :])
    poss.append((row8 + 8.0) * float(PEER_TOPK))
    cand = jnp.concatenate(pieces, axis=0)
    pos = jnp.concatenate(poss, axis=0)
    cnt16 = jnp.zeros((PEER_TOPK, tm), f32)
    zsum = None
    best0 = None
    for j in range(PEER_TOPK):
        m = jnp.max(cand, axis=0, keepdims=True)
        pj = jnp.min(jnp.where(cand == m, pos, 1e9), axis=0, keepdims=True)
        cand = jnp.where(pos == pj, neg_inf, cand)
        if j == 0:
            best0 = m
            zsum = jnp.ones_like(m)
        else:
            zsum = zsum + jnp.exp(m - best0)
        a_j = jnp.floor(pj * (1.0 / PEER_TOPK))
        cnt16 = cnt16 + jnp.where(row16 == a_j, 1.0, 0.0)
    cnt_i = jnp.zeros((PEER_NKEYS, tm), f32)
    for a in range(PEER_TOPK):
        cnt_i = jnp.where(ranks[0] == float(a), cnt16[a:a + 1, :], cnt_i)
    cnt_ref[0] = cnt_i
    e1_ref[0] = es[0] / zsum
    r2_ref[0] = ranks[1].astype(bf16)
    e2_ref[0] = es[1].astype(bf16)


def _route(q, keys):
    n = q.shape[0]
    ospec = pl.BlockSpec((1, PEER_NKEYS, TM_ROUTE), lambda i, h: (h, 0, i))
    oshape = (PEER_HEADS, PEER_NKEYS, n)
    return pl.pallas_call(
        _route_kernel,
        grid=(n // TM_ROUTE, PEER_HEADS),
        in_specs=[pl.BlockSpec((TM_ROUTE, 256), lambda i, h: (i, h)),
                  pl.BlockSpec((2, 1, PEER_NKEYS, 128), lambda i, h: (0, h, 0, 0))],
        out_specs=[ospec] * 4,
        out_shape=[_sds(oshape), _sds(oshape), _sds(oshape, bf16), _sds(oshape, bf16)],
        scratch_shapes=[pltpu.VMEM((PEER_TOPK, TM_ROUTE), f32), pltpu.VMEM((PEER_TOPK, TM_ROUTE), f32)],
        compiler_params=_params("arbitrary", "arbitrary"),
        name="peer_route",
    )(q, keys)


def _gated_act(h, cnt_ref, e1_ref, r2_ref, e2_ref, row0):
    zero = jnp.zeros((), bf16)
    parts = []
    for q in range(TE_HALF // PEER_NKEYS):
        row = row0 + q
        hs = h[q * PEER_NKEYS:(q + 1) * PEER_NKEYS, :]
        gate = None
        for hd in range(PEER_HEADS):
            cnt_row = cnt_ref[hd, row:row + 1, :].astype(bf16)
            e1_row = e1_ref[hd, row:row + 1, :].astype(bf16)
            term = jnp.where(r2_ref[hd] < cnt_row, e2_ref[hd] * e1_row, zero)
            gate = term if gate is None else gate + term
        gelu = 0.5 * hs * (1.0 + lax.erf(hs * (2.0 ** -0.5)))
        parts.append(gelu.astype(bf16) * gate)
    return jnp.concatenate(parts, axis=0)


def _dense_kernel(hn_ref, u_ref, vt_ref, vtp_ref, cnt_ref, e1_ref, cntp_ref, e1p_ref, r2_ref, e2_ref, o_ref,
                  xt_scr, acc_scr, hb_scr, *, n_chunks):
    c = pl.program_id(1)

    @pl.when(c == 0)
    def _():
        xt_scr[...] = hn_ref[...].T
        acc_scr[...] = jnp.zeros_like(acc_scr)
        hb_scr[...] = jnp.zeros_like(hb_scr)

    sub = TE_HALF // PEER_NKEYS
    xt = xt_scr[...]
    h_a = jnp.dot(u_ref[0:TE_HALF, :], xt, preferred_element_type=f32)
    act_bp = _gated_act(hb_scr[...], cntp_ref, e1p_ref, r2_ref, e2_ref, sub)
    part = jnp.dot(vtp_ref[...], act_bp, preferred_element_type=f32)
    hb_scr[...] = jnp.dot(u_ref[TE_HALF:TE, :], xt, preferred_element_type=f32)
    act_a = _gated_act(h_a, cnt_ref, e1_ref, r2_ref, e2_ref, 0)
    part = jnp.dot(vt_ref[:, 0:TE_HALF], act_a, preferred_element_type=f32) + part
    acc_scr[...] += part

    @pl.when(c == n_chunks - 1)
    def _():
        act_b = _gated_act(hb_scr[...], cnt_ref, e1_ref, r2_ref, e2_ref, sub)
        total = acc_scr[...] + jnp.dot(vt_ref[:, TE_HALF:TE], act_b, preferred_element_type=f32)
        o_ref[...] = total.T


def _peer_dense(hn, u_bf, vt_bf, cnt, e1, r2, e2):
    n = hn.shape[0]
    n_exp = u_bf.shape[0]
    n_chunks = n_exp // TE
    sub = TE // PEER_NKEYS
    prev = lambda c: jnp.maximum(c - 1, 0)
    return pl.pallas_call(
        functools.partial(_dense_kernel, n_chunks=n_chunks),
        grid=(n // TM, n_chunks),
        in_specs=[pl.BlockSpec((TM, D_MODEL), lambda i, c: (i, 0)),
                  pl.BlockSpec((TE, D_MODEL), lambda i, c: (c, 0)),
                  pl.BlockSpec((D_MODEL, TE), lambda i, c: (0, c)),
                  pl.BlockSpec((D_MODEL, TE_HALF), lambda i, c: (0, jnp.maximum(2 * c - 1, 0))),
                  pl.BlockSpec((PEER_HEADS, sub, TM), lambda i, c: (0, c, i)),
                  pl.BlockSpec((PEER_HEADS, sub, TM), lambda i, c: (0, c, i)),
                  pl.BlockSpec((PEER_HEADS, sub, TM), lambda i, c: (0, prev(c), i)),
                  pl.BlockSpec((PEER_HEADS, sub, TM), lambda i, c: (0, prev(c), i)),
                  pl.BlockSpec((PEER_HEADS, PEER_NKEYS, TM), lambda i, c: (0, 0, i)),
                  pl.BlockSpec((PEER_HEADS, PEER_NKEYS, TM), lambda i, c: (0, 0, i))],
        out_specs=pl.BlockSpec((TM, D_MODEL), lambda i, c: (i, 0)),
        out_shape=_sds((n, D_MODEL)),
        scratch_shapes=[pltpu.VMEM((D_MODEL, TM), bf16), pltpu.VMEM((D_MODEL, TM), f32),
                        pltpu.VMEM((TE_HALF, TM), f32)],
        compiler_params=_params("arbitrary", "arbitrary"),
        name="peer_dense",
    )(hn, u_bf, vt_bf, vt_bf, cnt, e1, cnt, e1, r2, e2)


def _final_kernel(x_ref, f_ref, gc_ref, g_ref, sc_ref, sh_ref, o_ref):
    x = x_ref[...] + gc_ref[0] * f_ref[...]
    y = x * lax.rsqrt(jnp.mean(x * x, axis=-1, keepdims=True) + NORM_EPS) * g_ref[...]
    o_ref[...] = y * (1.0 + sc_ref[0]) + sh_ref[0]


def _final(x, ffn, gc3, g, sc3, sh3, idx):
    n = x.shape[0]
    tok = pl.BlockSpec((TM, D_MODEL), lambda i: (i, 0))
    return pl.pallas_call(
        _final_kernel,
        grid=(n // TM,),
        in_specs=[tok, tok, _mod_spec(gc3, idx), pl.BlockSpec((1, D_MODEL), lambda i: (0, 0)),
                  _mod_spec(sc3, idx), _mod_spec(sh3, idx)],
        out_specs=tok, out_shape=_sds((n, D_MODEL)),
        compiler_params=_params("arbitrary"),
        name="final_norm",
    )(x, ffn, gc3, g.reshape(1, D_MODEL), sc3, sh3)


def _trunk(x, mod, mod_f, states, p, shared):
    bsz, t_len, _ = x.shape
    n = bsz * t_len
    shift_s, wkv_s, conv_s, c_s, n_s, m_s = states
    xf = x.reshape(n, D_MODEL)
    m3 = [_mod3(m, bsz, t_len, TM) for m in jnp.split(mod, 6, axis=-1)]
    idx = m3[0][1]
    sh_a, sc_a, g_a, sh_c, sc_c, g_c = (a for a, _ in m3)
    (shift_f, _), (scale_f, _) = (_mod3(m, bsz, t_len, TM) for m in jnp.split(mod_f, 2, axis=-1))

    z, gates = _norm_mod_matmul(xf, p['norm_mix_g'], sc_a, sh_a, idx, shared['w_in_bf'], Z_MAIN_COLS, 1280,
                                w_gate=shared['w_gate'], name="in_proj")
    y_a, new_shift, new_wkv = _rwkv_mix(z, bsz, t_len, shift_s, wkv_s, p, shared['wwa'], shared['g1'],
                                        shared['g1t'])
    y_b, new_conv, new_c, new_n, new_m = _mlstm_mix(z, gates, bsz, t_len, conv_s, c_s, n_s, m_s, p)
    x1 = _oproj(y_a, y_b, shared['w_out_bf'], xf, g_a, idx)
    q, hn = _norm_mod_matmul(x1, p['norm_ffn_g'], sc_c, sh_c, idx, shared['wq_bf'], D_MODEL, D_MODEL,
                             emit_hn=True, name="peer_query")
    cnt, e1, r2, e2 = _route(q, p['peer_keys'])
    ffn = _peer_dense(hn, shared['u_bf'], shared['vt_bf'], cnt, e1, r2, e2)
    y = _final(x1, ffn, g_c, p['norm_final_g'], scale_f, shift_f, idx)
    return y.reshape(bsz, t_len, D_MODEL), (new_shift, new_wkv, new_conv, new_c, new_n, new_m)


def kernel(x_prompt, x_sample, c_prompt, c_sample, state_rwkv_shift, state_rwkv_wkv, state_mlstm_conv,
           state_mlstm_C, state_mlstm_n, state_mlstm_m, w_ada, b_ada, norm_mix_g, w_in, a_mu, a_w0, a_w2,
           a_a0, a_a2, a_g2, a_kk, a_ka, a_rk, a_lnw, a_lnb, b_conv_w, b_conv_b, b_ig_bias, b_fg_bias,
           b_norm_g, w_out, norm_ffn_g, peer_wq, peer_keys, peer_u, peer_v, w_ada_final, b_ada_final,
           norm_final_g):
    layer = dict(w_ada=w_ada, b_ada=b_ada, norm_mix_g=norm_mix_g, w_in=w_in, a_mu=a_mu, a_w0=a_w0, a_w2=a_w2,
                 a_a0=a_a0, a_a2=a_a2, a_g2=a_g2, a_kk=a_kk, a_ka=a_ka, a_rk=a_rk, a_lnw=a_lnw, a_lnb=a_lnb,
                 b_conv_w=b_conv_w, b_conv_b=b_conv_b, b_ig_bias=b_ig_bias, b_fg_bias=b_fg_bias,
                 b_norm_g=b_norm_g, w_out=w_out, norm_ffn_g=norm_ffn_g, peer_wq=peer_wq, peer_keys=peer_keys,
                 peer_u=peer_u, peer_v=peer_v)
    assert w_ada.shape[0] == 1, "single-layer trunk"
    p = {k: v[0] for k, v in layer.items()}
    p['norm_final_g'] = norm_final_g
    bp, bs = x_prompt.shape[0], x_sample.shape[0]

    c_all = jnp.concatenate([c_prompt, c_sample], axis=0)
    c_all = jnp.pad(c_all, ((0, -c_all.shape[0] % SUBLANES), (0, 0)))
    mod = _ada(c_all, p['w_ada'], p['b_ada'])
    mod_f = _ada(c_all, w_ada_final, b_ada_final)

    head_of = jnp.arange(MIX_A) // A_HEAD
    g1 = (head_of[:, None] == jnp.arange(LANES)[None, :]).astype(bf16)
    zero = jnp.zeros((64, MIX_A), f32)
    wwa = jnp.concatenate([jnp.concatenate([p['a_w2'], zero], axis=1),
                           jnp.concatenate([zero, p['a_a2']], axis=1)], axis=0).astype(bf16)
    w_in_bf = p['w_in'].astype(bf16)
    shared = dict(
        g1=g1, g1t=g1.T, wwa=wwa, w_in_bf=w_in_bf,
        w_gate=jnp.pad(w_in_bf[:, Z_MAIN_COLS:], ((0, 0), (0, LANES - 2 * B_HEADS))),
        w_out_bf=p['w_out'].astype(bf16), wq_bf=p['peer_wq'].astype(bf16),
        u_bf=p['peer_u'].astype(bf16), vt_bf=p['peer_v'].T.astype(bf16))

    dt = x_prompt.dtype
    zero_states = (jnp.zeros((bp, A_SHIFT_COLS), dt), jnp.zeros((bp, A_HEADS, A_HEAD, A_HEAD), dt),
                   jnp.zeros((bp, B_CONV - 1, B_QK_COLS), dt), jnp.zeros((bp, B_HEADS, B_DQK, B_DV), dt),
                   jnp.zeros((bp, B_HEADS, B_DQK), dt), jnp.zeros((bp, B_HEADS), dt))
    y_prompt, p_states = _trunk(x_prompt, mod[:bp], mod_f[:bp], zero_states, p, shared)
    s_states_in = (state_rwkv_shift[0], state_rwkv_wkv[0], state_mlstm_conv[0], state_mlstm_C[0],
                   state_mlstm_n[0], state_mlstm_m[0])
    y_sample, s_states = _trunk(x_sample, mod[bp:bp + bs], mod_f[bp:bp + bs], s_states_in, p, shared)
    return (y_prompt, y_sample) + tuple(s[None] for s in p_states) + tuple(s[None] for s in s_states)
```

```python
import functools

import jax
import jax.numpy as jnp
from jax import lax
from jax.experimental import pallas as pl
from jax.experimental.pallas import tpu as pltpu

f32 = jnp.float32
bf16 = jnp.bfloat16

D_MODEL = 2048
MIX_A = 1024
A_HEAD = 64
A_HEADS = 16
A_SHIFT_COLS = 3 * MIX_A + 64 + 64 + 128
A_GN_EPS = A_HEAD * 1e-5
B_QK_COLS = 1024
B_HEADS = 4
B_DV = 256
B_DQK = 128
B_CONV = 4
B_CHUNK = 64
Z_MAIN_COLS = A_SHIFT_COLS + B_QK_COLS + 2 * 1024
PEER_HEADS = 8
PEER_NKEYS = 128
PEER_TOPK = 16
NORM_EPS = 1e-6

LANES = 128
SUBLANES = 8
VMEM_LIMIT_BYTES = 56 * 1024 * 1024

TM = 512
TM_EW = 256
TM_ROUTE = 512
TE = 1024
TE_HALF = TE // 2


def _params(*sem):
    return pltpu.CompilerParams(dimension_semantics=sem, vmem_limit_bytes=VMEM_LIMIT_BYTES)


def _sds(shape, dtype=f32):
    return jax.ShapeDtypeStruct(shape, dtype)


def _ada_kernel(c_ref, w_ref, b_ref, o_ref):
    c = c_ref[...]
    a = (c * jax.nn.sigmoid(c)).astype(bf16)
    o_ref[...] = jnp.dot(a, w_ref[...].astype(bf16), preferred_element_type=f32) + b_ref[...]


def _ada(c, w, b):
    m, n_out, tn = c.shape[0], w.shape[1], 1024
    return pl.pallas_call(
        _ada_kernel,
        grid=(n_out // tn,),
        in_specs=[pl.BlockSpec((m, D_MODEL), lambda j: (0, 0)),
                  pl.BlockSpec((D_MODEL, tn), lambda j: (0, j)),
                  pl.BlockSpec((1, tn), lambda j: (0, j))],
        out_specs=pl.BlockSpec((m, tn), lambda j: (0, j)),
        out_shape=_sds((m, n_out)),
        compiler_params=_params("arbitrary"),
        name="ada_mod",
    )(c, w, b.reshape(1, n_out))


def _mod3(m, bsz, t_len, tm):
    if t_len % tm == 0:
        per = t_len // tm
        return m.reshape(bsz, 1, D_MODEL), (lambda i: i // per)
    n = bsz * t_len
    assert tm % t_len == 0 and n % tm == 0
    return jnp.repeat(m, t_len, axis=0).reshape(n // tm, tm, D_MODEL), (lambda i: i)


def _mod_spec(arr, idx):
    return pl.BlockSpec((1, arr.shape[1], D_MODEL), lambda i, *_: (idx(i), 0, 0))


def _nmm_kernel(*refs, has_gate, emit_hn):
    it = iter(refs)
    x_ref, g_ref, sc_ref, sh_ref, w_ref = (next(it) for _ in range(5))
    wg_ref = next(it) if has_gate else None
    o_ref = next(it)
    og_ref = next(it) if has_gate else None
    hn_ref = next(it) if emit_hn else None
    hs_ref = next(it)

    @pl.when(pl.program_id(1) == 0)
    def _():
        x = x_ref[...]
        y = x * lax.rsqrt(jnp.mean(x * x, axis=-1, keepdims=True) + NORM_EPS) * g_ref[...]
        h = (y * (1.0 + sc_ref[0]) + sh_ref[0]).astype(bf16)
        hs_ref[...] = h
        if emit_hn:
            hn_ref[...] = h
        if has_gate:
            og_ref[...] = jnp.dot(h, wg_ref[...], preferred_element_type=f32)

    o_ref[...] = jnp.dot(hs_ref[...], w_ref[...], preferred_element_type=f32)


def _norm_mod_matmul(x, g, sc3, sh3, idx, w_bf, n_out, tn, w_gate=None, emit_hn=False, name="nmm"):
    n = x.shape[0]
    has_gate = w_gate is not None
    in_specs = [pl.BlockSpec((TM, D_MODEL), lambda i, j: (i, 0)),
                pl.BlockSpec((1, D_MODEL), lambda i, j: (0, 0)),
                _mod_spec(sc3, idx), _mod_spec(sh3, idx),
                pl.BlockSpec((D_MODEL, tn), lambda i, j: (0, j))]
    args = [x, g.reshape(1, D_MODEL), sc3, sh3, w_bf]
    out_specs = [pl.BlockSpec((TM, tn), lambda i, j: (i, j))]
    out_shape = [_sds((n, n_out))]
    if has_gate:
        in_specs.append(pl.BlockSpec((D_MODEL, LANES), lambda i, j: (0, 0)))
        args.append(w_gate)
        out_specs.append(pl.BlockSpec((TM, LANES), lambda i, j: (i, 0)))
        out_shape.append(_sds((n, LANES)))
    if emit_hn:
        out_specs.append(pl.BlockSpec((TM, D_MODEL), lambda i, j: (i, 0)))
        out_shape.append(_sds((n, D_MODEL), bf16))
    return pl.pallas_call(
        functools.partial(_nmm_kernel, has_gate=has_gate, emit_hn=emit_hn),
        grid=(n // TM, n_out // tn),
        in_specs=in_specs, out_specs=out_specs, out_shape=out_shape,
        scratch_shapes=[pltpu.VMEM((TM, D_MODEL), bf16)],
        compiler_params=_params("arbitrary", "arbitrary"),
        name=name,
    )(*args)


def _split_dot(x, w):
    hi = x.astype(bf16)
    lo = (x - hi.astype(f32)).astype(bf16)
    return (jnp.dot(hi, w, preferred_element_type=f32) + jnp.dot(lo, w, preferred_element_type=f32))


def _head_sum(x, g1, g1t):
    return _split_dot(_split_dot(x, g1), g1t)


def _rwkv_prep_kernel(z_ref, first_ref, mu_ref, w0_ref, wwa_ref, a0_ref, g2_ref, akk_ref, aka_ref, rk_ref,
                      g1_ref, g1t_ref, r_ref, k_ref, v_ref, d_ref, kk_ref, kka_ref, g_ref, bonus_ref,
                      *scratch, t_len, carry_mode):
    z = z_ref[...]
    tm = z.shape[0]
    row = lax.broadcasted_iota(jnp.int32, z.shape, 0)
    rolled = pltpu.roll(z, 1, axis=0)
    if carry_mode:
        carry_ref, = scratch
        first = jnp.where(pl.program_id(1) == 0, first_ref[0], carry_ref[0:1, :])
        prev = jnp.where(row == 0, first, rolled)
        carry_ref[0:1, :] = z[tm - 1:tm, :]
    else:
        prev = jnp.where(lax.rem(row, t_len) == 0, first_ref[...], rolled)
    zs = z + (prev - z) * mu_ref[...]
    r = zs[:, 0:MIX_A]
    k = zs[:, MIX_A:2 * MIX_A]
    v = zs[:, 2 * MIX_A:3 * MIX_A]
    xwa = zs[:, 3 * MIX_A:3 * MIX_A + 128]
    xg = zs[:, 3 * MIX_A + 128:3 * MIX_A + 256]
    lane = lax.broadcasted_iota(jnp.int32, xwa.shape, 1)
    xwa = jnp.where(lane < 64, jnp.tanh(xwa), xwa)
    lwa = jnp.dot(xwa.astype(bf16), wwa_ref[...], preferred_element_type=f32)
    u = w0_ref[...] + lwa[:, 0:MIX_A]
    w_log = -(jnp.maximum(-u, 0.0) + jnp.log(1.0 + jnp.exp(-jnp.abs(u)))) - 0.5
    d_ref[...] = jnp.exp(-jnp.exp(w_log))
    a = jax.nn.sigmoid(a0_ref[...] + lwa[:, MIX_A:2 * MIX_A])
    g_ref[...] = jnp.dot(jax.nn.sigmoid(xg).astype(bf16), g2_ref[...].astype(bf16), preferred_element_type=f32)
    g1, g1t = g1_ref[...], g1t_ref[...]
    kk = k * akk_ref[...]
    nrm = jnp.sqrt(_head_sum(kk * kk, g1, g1t))
    kk = kk / jnp.maximum(nrm, 1e-12)
    k = k * (1.0 + (a - 1.0) * aka_ref[...])
    r_ref[...] = r
    k_ref[...] = k
    v_ref[...] = v
    kk_ref[...] = kk
    kka_ref[...] = kk * a
    bonus_ref[...] = _head_sum(r * k * rk_ref[...], g1, g1t) * v


def _rwkv_prep(z, shift_buf, bsz, t_len, p, wwa, g1, g1t):
    n = bsz * t_len
    row = lambda a: a.reshape(1, -1)
    consts = [row(p['a_mu']), row(p['a_w0']), wwa, row(p['a_a0']), p['a_g2'], row(p['a_kk']), row(p['a_ka']),
              row(p['a_rk']), g1, g1t]
    carry_mode = t_len % TM_EW == 0
    if carry_mode:
        per = t_len // TM_EW
        grid = (bsz, per)
        tok = lambda w: pl.BlockSpec((TM_EW, w), lambda b, i: (b * per + i, 0))
        first = shift_buf.reshape(bsz, 1, A_SHIFT_COLS)
        first_spec = pl.BlockSpec((1, 1, A_SHIFT_COLS), lambda b, i: (b, 0, 0))
        scratch = [pltpu.VMEM((SUBLANES, A_SHIFT_COLS), f32)]
        sem = ("arbitrary", "arbitrary")
    else:
        assert TM_EW % t_len == 0 and n % TM_EW == 0
        grid = (n // TM_EW,)
        tok = lambda w: pl.BlockSpec((TM_EW, w), lambda i: (i, 0))
        first = jnp.repeat(shift_buf, t_len, axis=0)
        first_spec = tok(A_SHIFT_COLS)
        scratch = []
        sem = ("arbitrary",)
    full = lambda a: pl.BlockSpec(a.shape, lambda *_: (0,) * a.ndim)
    return pl.pallas_call(
        functools.partial(_rwkv_prep_kernel, t_len=t_len, carry_mode=carry_mode),
        grid=grid,
        in_specs=[tok(A_SHIFT_COLS), first_spec] + [full(a) for a in consts],
        out_specs=[tok(MIX_A)] * 8,
        out_shape=[_sds((n, MIX_A))] * 8,
        scratch_shapes=scratch,
        compiler_params=_params(*sem),
        name="rwkv_prep",
    )(z, first, *consts)


SCAN_CHUNK = 64
RELAYOUT_T = 128


def _rwkv_scan_kernel(r_ref, d_ref, k_ref, kk_ref, kka_ref, v_ref, s0_ref, y_ref, so_ref, *, n_rows, t_chunk):
    @pl.when(pl.program_id(1) == 0)
    def _():
        so_ref[...] = s0_ref[...]

    def step(t, carry):
        kk, d, kka, k, r = kk_ref[t], d_ref[t], kka_ref[t], k_ref[t], r_ref[t]
        for i in range(n_rows):
            s_i = so_ref[i]
            sa = -jnp.sum(s_i * kk, axis=0, keepdims=True)
            v_i = v_ref[t, pl.ds(i, 1), :]
            s_n = s_i * d + sa * kka + v_i * k
            so_ref[i] = s_n
            y_ref[t, pl.ds(i, 1), :] = jnp.sum(s_n * r, axis=0, keepdims=True)
        return carry

    lax.fori_loop(0, t_chunk, step, 0)


def _rwkv_scan(r, d, k, kk, kka, v, s0, t_chunk):
    t_len, n_rows, ltot = v.shape
    jspec = pl.BlockSpec((t_chunk, A_HEAD, LANES), lambda l, t: (t, 0, l))
    vspec = pl.BlockSpec((t_chunk, n_rows, LANES), lambda l, t: (t, 0, l))
    sspec = pl.BlockSpec((n_rows, A_HEAD, LANES), lambda l, t: (0, 0, l))
    return pl.pallas_call(
        functools.partial(_rwkv_scan_kernel, n_rows=n_rows, t_chunk=t_chunk),
        grid=(ltot // LANES, t_len // t_chunk),
        in_specs=[jspec] * 5 + [vspec, sspec],
        out_specs=[vspec, sspec],
        out_shape=[_sds(v.shape), _sds((n_rows, A_HEAD, ltot))],
        compiler_params=_params("arbitrary", "arbitrary"),
        name="rwkv_scan",
    )(r, d, k, kk, kka, v, s0)


def _to_scan_kernel(*refs, bsz, pack, val_mode):
    x_refs, o_ref, pt_ref = refs[:bsz], refs[bsz], refs[bsz + 1]
    for b in range(bsz):
        pt_ref[b] = x_refs[b][...].T
    n_idx = A_HEAD // pack if val_mode else A_HEAD
    for j0 in range(0, n_idx, SUBLANES):
        tiles = []
        for j in range(j0, j0 + SUBLANES):
            if val_mode:
                rows = [pt_ref[b, pl.ds(g * n_idx + j, A_HEADS, stride=A_HEAD), :]
                        for g in range(pack) for b in range(bsz)]
            else:
                rows = [pt_ref[b, pl.ds(j, A_HEADS, stride=A_HEAD), :] for b in range(bsz)] * pack
            tiles.append(jnp.concatenate(rows, axis=0).T)
        o_ref[:, j0:j0 + SUBLANES, :] = jnp.swapaxes(jnp.stack(tiles, axis=0), 0, 1)


def _to_scan_layout(a, bsz, t_len, pack, val_mode):
    per = t_len // RELAYOUT_T
    n_idx = A_HEAD // pack if val_mode else A_HEAD
    return pl.pallas_call(
        functools.partial(_to_scan_kernel, bsz=bsz, pack=pack, val_mode=val_mode),
        grid=(per,),
        in_specs=[pl.BlockSpec((RELAYOUT_T, MIX_A), lambda tb, b=b: (b * per + tb, 0)) for b in range(bsz)],
        out_specs=pl.BlockSpec((RELAYOUT_T, n_idx, LANES), lambda tb: (tb, 0, 0)),
        out_shape=_sds((t_len, n_idx, LANES)),
        scratch_shapes=[pltpu.VMEM((bsz, MIX_A, RELAYOUT_T), f32)],
        compiler_params=_params("arbitrary"),
        name="to_scan_val" if val_mode else "to_scan_key",
    )(*([a] * bsz))


def _from_scan_kernel(y_ref, o_ref, pt_ref, *, bsz, pack):
    n_idx = A_HEAD // pack
    for j0 in range(0, n_idx, SUBLANES):
        tiles = jnp.swapaxes(y_ref[:, j0:j0 + SUBLANES, :], 0, 1)
        for j in range(j0, j0 + SUBLANES):
            tt = tiles[j - j0].T
            for g in range(pack):
                for b in range(bsz):
                    lo = (g * bsz + b) * A_HEADS
                    pt_ref[b, pl.ds(g * n_idx + j, A_HEADS, stride=A_HEAD), :] = tt[lo:lo + A_HEADS, :]
    for b in range(bsz):
        o_ref[b] = pt_ref[b].T


def _from_scan_layout(y, bsz, t_len, pack):
    per = t_len // RELAYOUT_T
    n_idx = A_HEAD // pack
    out = pl.pallas_call(
        functools.partial(_from_scan_kernel, bsz=bsz, pack=pack),
        grid=(per,),
        in_specs=[pl.BlockSpec((RELAYOUT_T, n_idx, LANES), lambda tb: (tb, 0, 0))],
        out_specs=pl.BlockSpec((bsz, RELAYOUT_T, MIX_A), lambda tb: (0, tb, 0)),
        out_shape=_sds((bsz, t_len, MIX_A)),
        scratch_shapes=[pltpu.VMEM((bsz, MIX_A, RELAYOUT_T), f32)],
        compiler_params=_params("arbitrary"),
        name="from_scan",
    )(y)
    return out.reshape(bsz * t_len, MIX_A)


def _rwkv_post_kernel(y_ref, bonus_ref, g_ref, lnw_ref, lnb_ref, g1_ref, g1t_ref, o_ref):
    y = y_ref[...]
    g1, g1t = g1_ref[...], g1t_ref[...]
    mu = _head_sum(y, g1, g1t) * (1.0 / A_HEAD)
    yc = y - mu
    var = _head_sum(yc * yc, g1, g1t) * (1.0 / A_HEAD)
    yn = yc * lax.rsqrt(var + A_GN_EPS) * lnw_ref[...] + lnb_ref[...]
    o_ref[...] = (yn + bonus_ref[...]) * g_ref[...]


def _rwkv_post(y, bonus, g, p, g1, g1t):
    n = y.shape[0]
    tok = pl.BlockSpec((TM_EW, MIX_A), lambda i: (i, 0))
    full = lambda a: pl.BlockSpec(a.shape, lambda i: (0,) * a.ndim)
    consts = [p['a_lnw'].reshape(1, -1), p['a_lnb'].reshape(1, -1), g1, g1t]
    return pl.pallas_call(
        _rwkv_post_kernel,
        grid=(n // TM_EW,),
        in_specs=[tok] * 3 + [full(a) for a in consts],
        out_specs=tok, out_shape=_sds((n, MIX_A)),
        compiler_params=_params("arbitrary"),
        name="rwkv_post",
    )(y, bonus, g, *consts)


def _rwkv_mix(z, bsz, t_len, shift_buf, wkv0, p, wwa, g1, g1t):
    n = bsz * t_len
    r, k, v, d, kk, kka, g, bonus = _rwkv_prep(z, shift_buf, bsz, t_len, p, wwa, g1, g1t)

    bh = bsz * A_HEADS
    pack = LANES // bh if bh < LANES else 1
    n_rows = A_HEAD // pack

    s0 = wkv0.reshape(bsz, A_HEADS, pack, n_rows, A_HEAD).transpose(3, 4, 2, 0, 1).reshape(n_rows, A_HEAD, pack * bh)
    if pack * bh == LANES and t_len % RELAYOUT_T == 0:
        keys = [_to_scan_layout(a, bsz, t_len, pack, False) for a in (r, d, k, kk, kka)]
        y, s_out = _rwkv_scan(*keys, _to_scan_layout(v, bsz, t_len, pack, True), s0, SCAN_CHUNK)
        y = _from_scan_layout(y, bsz, t_len, pack)
    else:
        assert pack == 1 and t_len <= SCAN_CHUNK

        def scan_rows(a):
            return a.reshape(bsz, t_len, A_HEADS, A_HEAD).transpose(1, 3, 0, 2).reshape(t_len, A_HEAD, bh)

        y, s_out = _rwkv_scan(*(scan_rows(a) for a in (r, d, k, kk, kka, v)), s0, t_len)
        y = y.reshape(t_len, A_HEAD, bsz, A_HEADS).transpose(2, 0, 3, 1).reshape(n, MIX_A)
    new_wkv = s_out.reshape(n_rows, A_HEAD, pack, bsz, A_HEADS).transpose(3, 4, 2, 0, 1)
    new_wkv = new_wkv.reshape(bsz, A_HEADS, A_HEAD, A_HEAD)
    y = _rwkv_post(y, bonus, g, p, g1, g1t)
    new_shift = z.reshape(bsz, t_len, Z_MAIN_COLS)[:, -1, :A_SHIFT_COLS]
    return y, new_shift, new_wkv


def _log_sigmoid(x):
    return jnp.minimum(x, 0.0) - jnp.log(1.0 + jnp.exp(-jnp.abs(x)))


def _mlstm_kernel(*refs, chunk, bb):
    zq_refs = refs[0:2 * B_HEADS]
    v_refs = refs[2 * B_HEADS:3 * B_HEADS]
    o_refs = refs[3 * B_HEADS:4 * B_HEADS]
    (cw_ref, cb_ref, cbuf_ref, gc_ref, gr_ref, bias_r_ref, bias_c_ref, ng_ref, c0_ref, n0_ref, m0_ref,
     y_ref, c_ref, n_ref, m_ref, tail_ref) = refs[4 * B_HEADS:]
    n_tail = B_CONV - 1

    @pl.when(pl.program_id(1) == 0)
    def _():
        c_ref[...] = c0_ref[...]
        n_ref[...] = n0_ref[...]
        m_ref[...] = m0_ref[...]
        tail_ref[...] = jnp.zeros_like(tail_ref)
        for bi in range(bb):
            tail_ref[(bi + 1) * SUBLANES - n_tail:(bi + 1) * SUBLANES, :] = cbuf_ref[bi]

    def conv_silu(cb, bi):
        cols = slice(cb * B_DQK, (cb + 1) * B_DQK)
        blk = zq_refs[cb][bi]
        zp = jnp.concatenate([tail_ref[bi * SUBLANES:(bi + 1) * SUBLANES, cols], blk], axis=0)
        acc = cb_ref[:, cols]
        for w in range(B_CONV):
            off = SUBLANES - n_tail + w
            acc = acc + zp[off:off + chunk, :] * cw_ref[w:w + 1, cols]
        tail_ref[(bi + 1) * SUBLANES - n_tail:(bi + 1) * SUBLANES, cols] = blk[chunk - n_tail:chunk, :]
        return acc * jax.nn.sigmoid(acc)

    ti = lax.broadcasted_iota(jnp.int32, (chunk, chunk), 0)
    si = lax.broadcasted_iota(jnp.int32, (chunk, chunk), 1)
    tri = si <= ti
    nt = (((1,), (1,)), ((), ()))
    tn = (((0,), (0,)), ((), ()))
    def head_chain(bi, h):
        gcol = gc_ref[bi] + bias_r_ref[...]
        grow = gr_ref[bi, 0] + bias_c_ref[...]
        q = conv_silu(h, bi)
        k = conv_silu(B_HEADS + h, bi) * (B_DQK ** -0.5)
        v = v_refs[h][bi]
        yield
        i_col, f_col = gcol[:, h:h + 1], gcol[:, B_HEADS + h:B_HEADS + h + 1]
        i_row, f_row = grow[h:h + 1, :], grow[B_HEADS + h:B_HEADS + h + 1, :]
        lf_col, lf_row = _log_sigmoid(f_col), _log_sigmoid(f_row)
        b_col = jnp.sum(jnp.where(tri, lf_row, 0.0), axis=1, keepdims=True)
        b_row = jnp.sum(jnp.where(ti <= si, lf_col, 0.0), axis=0, keepdims=True)
        m_prev = m_ref[bi, h:h + 1, :]
        yield
        log_d = jnp.where(tri, b_col - b_row + i_row, -jnp.inf)
        inter = b_col + m_prev
        m_t = jnp.maximum(inter, jnp.max(log_d, axis=1, keepdims=True))
        qb, kb, vb = q.astype(bf16), k.astype(bf16), v.astype(bf16)
        yield
        s = lax.dot_general(qb, kb, nt, preferred_element_type=f32) * jnp.exp(log_d - m_t)
        w_inter = jnp.exp(inter - m_t)
        c_old = c_ref[bi, h]
        n_old = n_ref[bi, h:h + 1, :]
        yield
        num = (jnp.dot(s.astype(bf16), vb, preferred_element_type=f32)
               + w_inter * jnp.dot(qb, c_old.astype(bf16), preferred_element_type=f32))
        den = jnp.sum(s, axis=1, keepdims=True) + w_inter * jnp.sum(q * n_old, axis=1, keepdims=True)
        yield
        hh = num / jnp.maximum(jnp.abs(den), jnp.exp(-m_t))
        m_new = m_t[chunk - 1:chunk, :]
        b_last = b_col[chunk - 1:chunk, :]
        ws = jnp.exp(b_last - b_col + i_col - m_new)
        d_c = jnp.exp(b_last + m_prev - m_new)
        kw = k * ws
        yield
        c_ref[bi, h] = d_c * c_old + lax.dot_general(kw.astype(bf16), vb, tn, preferred_element_type=f32)
        n_ref[bi, h:h + 1, :] = d_c * n_old + jnp.sum(kw, axis=0, keepdims=True)
        m_ref[bi, h:h + 1, :] = m_new
        yield
        hn = hh * lax.rsqrt(jnp.mean(hh * hh, axis=1, keepdims=True) + NORM_EPS)
        y_ref[bi, :, h * B_DV:(h + 1) * B_DV] = (hn * ng_ref[:, h * B_DV:(h + 1) * B_DV]
                                                 * jax.nn.sigmoid(o_refs[h][bi]))

    _interleave(*[head_chain(bi, h) for bi in range(bb) for h in range(B_HEADS)])


def _interleave(*gens):
    live = list(gens)
    while live:
        for gen in list(live):
            try:
                next(gen)
            except StopIteration:
                live.remove(gen)


MLSTM_BATCH = 4


def _mlstm(z, conv_buf, gates_col, gates_row, p, c0, n0, m0, bsz, n_chunks, chunk):
    t_pad = n_chunks * chunk
    bb = MLSTM_BATCH if bsz % MLSTM_BATCH == 0 else 1
    qk_col0 = A_SHIFT_COLS // B_DQK
    v_col0 = (A_SHIFT_COLS + B_QK_COLS) // B_DV
    o_col0 = v_col0 + B_HEADS
    tok = lambda w: pl.BlockSpec((bb, chunk, w), lambda b, c: (b, c, 0))
    zcol = lambda w, cb: pl.BlockSpec((bb, chunk, w), lambda b, c: (b, c, cb))
    full = lambda a: pl.BlockSpec(a.shape, lambda b, c: (0,) * a.ndim)
    bias = jnp.concatenate([p['b_ig_bias'], p['b_fg_bias']])
    bias_r = jnp.pad(bias, (0, LANES - 2 * B_HEADS)).reshape(1, LANES)
    bias_c = bias.reshape(2 * B_HEADS, 1)
    ng = p['b_norm_g'].reshape(1, -1)
    cw, cb = p['b_conv_w'], p['b_conv_b'].reshape(1, -1)
    cspec = pl.BlockSpec((bb, B_HEADS, B_DQK, B_DV), lambda b, c: (b, 0, 0, 0))
    nspec = pl.BlockSpec((bb, B_HEADS, B_DQK), lambda b, c: (b, 0, 0))
    mspec = pl.BlockSpec((bb, B_HEADS, 1), lambda b, c: (b, 0, 0))
    return pl.pallas_call(
        functools.partial(_mlstm_kernel, chunk=chunk, bb=bb),
        grid=(bsz // bb, n_chunks),
        in_specs=[zcol(B_DQK, qk_col0 + j) for j in range(2 * B_HEADS)]
                 + [zcol(B_DV, v_col0 + h) for h in range(B_HEADS)]
                 + [zcol(B_DV, o_col0 + h) for h in range(B_HEADS)]
                 + [full(cw), full(cb), pl.BlockSpec((bb, B_CONV - 1, B_QK_COLS), lambda b, c: (b, 0, 0)),
                    tok(LANES), pl.BlockSpec((bb, 1, 2 * B_HEADS, chunk), lambda b, c: (b, c, 0, 0)),
                    full(bias_r), full(bias_c), full(ng), cspec, nspec, mspec],
        out_specs=[tok(B_HEADS * B_DV), cspec, nspec, mspec],
        out_shape=[_sds((bsz, t_pad, B_HEADS * B_DV)), _sds(c0.shape), _sds(n0.shape), _sds(m0.shape)],
        scratch_shapes=[pltpu.VMEM((bb * SUBLANES, B_QK_COLS), f32)],
        compiler_params=_params("arbitrary", "arbitrary"),
        name="mlstm",
    )(*([z.reshape(bsz, t_pad, -1)] * (4 * B_HEADS)), cw, cb, conv_buf, gates_col, gates_row, bias_r, bias_c, ng,
      c0, n0, m0)


def _mlstm_mix(z, gates, bsz, t_len, conv_buf, c0, n0, m0, p):
    n = bsz * t_len
    z3 = z.reshape(bsz, t_len, Z_MAIN_COLS)
    zqk = z3[:, :, A_SHIFT_COLS:A_SHIFT_COLS + B_QK_COLS]
    new_conv = jnp.concatenate([conv_buf, zqk[:, max(t_len - (B_CONV - 1), 0):]], axis=1)[:, -(B_CONV - 1):]
    chunk = B_CHUNK if t_len % B_CHUNK == 0 else SUBLANES
    t_pad = -(-t_len // chunk) * chunk
    g3 = gates.reshape(bsz, t_len, LANES)
    if t_pad != t_len:
        z = jnp.pad(z3, ((0, 0), (0, t_pad - t_len), (0, 0))).reshape(bsz * t_pad, -1)
        lane = jnp.arange(LANES)
        fill = jnp.where(lane < B_HEADS, -1e30, jnp.where(lane < 2 * B_HEADS, 1e30, 0.0)).astype(f32)
        g3 = jnp.concatenate([g3, jnp.broadcast_to(fill, (bsz, t_pad - t_len, LANES))], axis=1)
    n_chunks = t_pad // chunk
    gates_col = g3
    gates_row = g3[:, :, :2 * B_HEADS].reshape(bsz, n_chunks, chunk, 2 * B_HEADS).transpose(0, 1, 3, 2)
    y, c_new, n_new, m_new = _mlstm(z, conv_buf, gates_col, gates_row, p, c0, n0, m0.reshape(bsz, B_HEADS, 1),
                                    bsz, n_chunks, chunk)
    y = y[:, :t_len].reshape(n, -1)
    return y, new_conv, c_new, n_new, m_new.reshape(bsz, B_HEADS)


def _oproj_kernel(ya_ref, yb_ref, wt_ref, wb_ref, x_ref, ga_ref, o_ref):
    acc = jnp.dot(ya_ref[...].astype(bf16), wt_ref[...], preferred_element_type=f32)
    acc = acc + jnp.dot(yb_ref[...].astype(bf16), wb_ref[...], preferred_element_type=f32)
    o_ref[...] = x_ref[...] + ga_ref[0] * acc


def _oproj(ya, yb, w_out_bf, x, ga3, idx):
    n = x.shape[0]
    half = D_MODEL // 2
    tok = lambda w: pl.BlockSpec((TM, w), lambda i: (i, 0))
    return pl.pallas_call(
        _oproj_kernel,
        grid=(n // TM,),
        in_specs=[tok(half), tok(half),
                  pl.BlockSpec((half, D_MODEL), lambda i: (0, 0)), pl.BlockSpec((half, D_MODEL), lambda i: (1, 0)),
                  tok(D_MODEL), _mod_spec(ga3, idx)],
        out_specs=tok(D_MODEL),
        out_shape=_sds((n, D_MODEL)),
        compiler_params=_params("arbitrary"),
        name="out_proj",
    )(ya, yb, w_out_bf, w_out_bf, x, ga3)


def _route_kernel(q_ref, keys_ref, cnt_ref, e1_ref, r2_ref, e2_ref, v1_scr, v2_scr):
    q = q_ref[...]
    tm = q.shape[0]
    neg_inf = -jnp.inf
    iota_k = lax.broadcasted_iota(jnp.int32, (PEER_NKEYS, tm), 0).astype(f32)
    nt = (((1,), (1,)), ((), ()))
    ranks, es = [], []
    for half, vscr in enumerate((v1_scr, v2_scr)):
        qp = q[:, half * 128:(half + 1) * 128].astype(bf16)
        s = lax.dot_general(keys_ref[half, 0].astype(bf16), qp, nt, preferred_element_type=f32)
        rank = jnp.full((PEER_NKEYS, tm), float(PEER_TOPK), f32)
        work = s
        for a in range(PEER_TOPK):
            m = jnp.max(work, axis=0, keepdims=True)
            idx = jnp.min(jnp.where(work == m, iota_k, float(PEER_NKEYS)), axis=0, keepdims=True)
            sel = iota_k == idx
            rank = jnp.where(sel, float(a), rank)
            work = jnp.where(sel, neg_inf, work)
            vscr[a:a + 1, :] = m
        ranks.append(rank)
        es.append(jnp.exp(s - vscr[0:1, :]))
    v1, v2 = v1_scr[...], v2_scr[...]
    row8 = lax.broadcasted_iota(jnp.int32, (SUBLANES, tm), 0).astype(f32)
    row16 = lax.broadcasted_iota(jnp.int32, (PEER_TOPK, tm), 0).astype(f32)
    pieces, poss = [v1[0:1, :] + v2], [row16]
    for a in range(1, 8):
        nb = PEER_TOPK // (a + 1)
        pieces.append(jnp.where(row8 < float(nb), v1[a:a + 1, :] + v2[0:SUBLANES, :], neg_inf))
        poss.append(row8 + float(a * PEER_TOPK))
    pieces.append(v1[8:16, :] + v2[0:1, :])
    poss.append((row8 + 8.0) * float(PEER_TOPK))
    cand = jnp.concatenate(pieces, axis=0)
    pos = jnp.concatenate(poss, axis=0)
    cnt16 = jnp.zeros((PEER_TOPK, tm), f32)
    zsum = None
    best0 = None
    for j in range(PEER_TOPK):
        m = jnp.max(cand, axis=0, keepdims=True)
        pj = jnp.min(jnp.where(cand == m, pos, 1e9), axis=0, keepdims=True)
        cand = jnp.where(pos == pj, neg_inf, cand)
        if j == 0:
            best0 = m
            zsum = jnp.ones_like(m)
        else:
            zsum = zsum + jnp.exp(m - best0)
        a_j = jnp.floor(pj * (1.0 / PEER_TOPK))
        cnt16 = cnt16 + jnp.where(row16 == a_j, 1.0, 0.0)
    cnt_i = jnp.zeros((PEER_NKEYS, tm), f32)
    for a in range(PEER_TOPK):
        cnt_i = jnp.where(ranks[0] == float(a), cnt16[a:a + 1, :], cnt_i)
    cnt_ref[0] = cnt_i
    e1_ref[0] = es[0] / zsum
    r2_ref[0] = ranks[1].astype(bf16)
    e2_ref[0] = es[1].astype(bf16)


def _route(q, keys):
    n = q.shape[0]
    ospec = pl.BlockSpec((1, PEER_NKEYS, TM_ROUTE), lambda i, h: (h, 0, i))
    oshape = (PEER_HEADS, PEER_NKEYS, n)
    return pl.pallas_call(
        _route_kernel,
        grid=(n // TM_ROUTE, PEER_HEADS),
        in_specs=[pl.BlockSpec((TM_ROUTE, 256), lambda i, h: (i, h)),
                  pl.BlockSpec((2, 1, PEER_NKEYS, 128), lambda i, h: (0, h, 0, 0))],
        out_specs=[ospec] * 4,
        out_shape=[_sds(oshape), _sds(oshape), _sds(oshape, bf16), _sds(oshape, bf16)],
        scratch_shapes=[pltpu.VMEM((PEER_TOPK, TM_ROUTE), f32), pltpu.VMEM((PEER_TOPK, TM_ROUTE), f32)],
        compiler_params=_params("arbitrary", "arbitrary"),
        name="peer_route",
    )(q, keys)


def _gated_act(h, cnt_ref, e1_ref, r2_ref, e2_ref, row0):
    zero = jnp.zeros((), bf16)
    parts = []
    for q in range(TE_HALF // PEER_NKEYS):
        row = row0 + q
        hs = h[q * PEER_NKEYS:(q + 1) * PEER_NKEYS, :]
        gate = None
        for hd in range(PEER_HEADS):
            cnt_row = cnt_ref[hd, row:row + 1, :].astype(bf16)
            e1_row = e1_ref[hd, row:row + 1, :].astype(bf16)
            term = jnp.where(r2_ref[hd] < cnt_row, e2_ref[hd] * e1_row, zero)
            gate = term if gate is None else gate + term
        gelu = 0.5 * hs * (1.0 + lax.erf(hs * (2.0 ** -0.5)))
        parts.append(gelu.astype(bf16) * gate)
    return jnp.concatenate(parts, axis=0)


def _dense_kernel(hn_ref, u_ref, vt_ref, vtp_ref, cnt_ref, e1_ref, cntp_ref, e1p_ref, r2_ref, e2_ref, o_ref,
                  xt_scr, acc_scr, hb_scr, *, n_chunks):
    c = pl.program_id(1)

    @pl.when(c == 0)
    def _():
        xt_scr[...] = hn_ref[...].T
        acc_scr[...] = jnp.zeros_like(acc_scr)
        hb_scr[...] = jnp.zeros_like(hb_scr)

    sub = TE_HALF // PEER_NKEYS
    xt = xt_scr[...]
    h_a = jnp.dot(u_ref[0:TE_HALF, :], xt, preferred_element_type=f32)
    act_bp = _gated_act(hb_scr[...], cntp_ref, e1p_ref, r2_ref, e2_ref, sub)
    part = jnp.dot(vtp_ref[...], act_bp, preferred_element_type=f32)
    hb_scr[...] = jnp.dot(u_ref[TE_HALF:TE, :], xt, preferred_element_type=f32)
    act_a = _gated_act(h_a, cnt_ref, e1_ref, r2_ref, e2_ref, 0)
    part = jnp.dot(vt_ref[:, 0:TE_HALF], act_a, preferred_element_type=f32) + part
    acc_scr[...] += part

    @pl.when(c == n_chunks - 1)
    def _():
        act_b = _gated_act(hb_scr[...], cnt_ref, e1_ref, r2_ref, e2_ref, sub)
        total = acc_scr[...] + jnp.dot(vt_ref[:, TE_HALF:TE], act_b, preferred_element_type=f32)
        o_ref[...] = total.T


def _peer_dense(hn, u_bf, vt_bf, cnt, e1, r2, e2):
    n = hn.shape[0]
    n_exp = u_bf.shape[0]
    n_chunks = n_exp // TE
    sub = TE // PEER_NKEYS
    prev = lambda c: jnp.maximum(c - 1, 0)
    return pl.pallas_call(
        functools.partial(_dense_kernel, n_chunks=n_chunks),
        grid=(n // TM, n_chunks),
        in_specs=[pl.BlockSpec((TM, D_MODEL), lambda i, c: (i, 0)),
                  pl.BlockSpec((TE, D_MODEL), lambda i, c: (c, 0)),
                  pl.BlockSpec((D_MODEL, TE), lambda i, c: (0, c)),
                  pl.BlockSpec((D_MODEL, TE_HALF), lambda i, c: (0, jnp.maximum(2 * c - 1, 0))),
                  pl.BlockSpec((PEER_HEADS, sub, TM), lambda i, c: (0, c, i)),
                  pl.BlockSpec((PEER_HEADS, sub, TM), lambda i, c: (0, c, i)),
                  pl.BlockSpec((PEER_HEADS, sub, TM), lambda i, c: (0, prev(c), i)),
                  pl.BlockSpec((PEER_HEADS, sub, TM), lambda i, c: (0, prev(c), i)),
                  pl.BlockSpec((PEER_HEADS, PEER_NKEYS, TM), lambda i, c: (0, 0, i)),
                  pl.BlockSpec((PEER_HEADS, PEER_NKEYS, TM), lambda i, c: (0, 0, i))],
        out_specs=pl.BlockSpec((TM, D_MODEL), lambda i, c: (i, 0)),
        out_shape=_sds((n, D_MODEL)),
        scratch_shapes=[pltpu.VMEM((D_MODEL, TM), bf16), pltpu.VMEM((D_MODEL, TM), f32),
                        pltpu.VMEM((TE_HALF, TM), f32)],
        compiler_params=_params("arbitrary", "arbitrary"),
        name="peer_dense",
    )(hn, u_bf, vt_bf, vt_bf, cnt, e1, cnt, e1, r2, e2)


def _final_kernel(x_ref, f_ref, gc_ref, g_ref, sc_ref, sh_ref, o_ref):
    x = x_ref[...] + gc_ref[0] * f_ref[...]
    y = x * lax.rsqrt(jnp.mean(x * x, axis=-1, keepdims=True) + NORM_EPS) * g_ref[...]
    o_ref[...] = y * (1.0 + sc_ref[0]) + sh_ref[0]


def _final(x, ffn, gc3, g, sc3, sh3, idx):
    n = x.shape[0]
    tok = pl.BlockSpec((TM, D_MODEL), lambda i: (i, 0))
    return pl.pallas_call(
        _final_kernel,
        grid=(n // TM,),
        in_specs=[tok, tok, _mod_spec(gc3, idx), pl.BlockSpec((1, D_MODEL), lambda i: (0, 0)),
                  _mod_spec(sc3, idx), _mod_spec(sh3, idx)],
        out_specs=tok, out_shape=_sds((n, D_MODEL)),
        compiler_params=_params("arbitrary"),
        name="final_norm",
    )(x, ffn, gc3, g.reshape(1, D_MODEL), sc3, sh3)


def _trunk(x, mod, mod_f, states, p, shared):
    bsz, t_len, _ = x.shape
    n = bsz * t_len
    shift_s, wkv_s, conv_s, c_s, n_s, m_s = states
    xf = x.reshape(n, D_MODEL)
    m3 = [_mod3(m, bsz, t_len, TM) for m in jnp.split(mod, 6, axis=-1)]
    idx = m3[0][1]
    sh_a, sc_a, g_a, sh_c, sc_c, g_c = (a for a, _ in m3)
    (shift_f, _), (scale_f, _) = (_mod3(m, bsz, t_len, TM) for m in jnp.split(mod_f, 2, axis=-1))

    z, gates = _norm_mod_matmul(xf, p['norm_mix_g'], sc_a, sh_a, idx, shared['w_in_bf'], Z_MAIN_COLS, 1280,
                                w_gate=shared['w_gate'], name="in_proj")
    y_a, new_shift, new_wkv = _rwkv_mix(z, bsz, t_len, shift_s, wkv_s, p, shared['wwa'], shared['g1'],
                                        shared['g1t'])
    y_b, new_conv, new_c, new_n, new_m = _mlstm_mix(z, gates, bsz, t_len, conv_s, c_s, n_s, m_s, p)
    x1 = _oproj(y_a, y_b, shared['w_out_bf'], xf, g_a, idx)
    q, hn = _norm_mod_matmul(x1, p['norm_ffn_g'], sc_c, sh_c, idx, shared['wq_bf'], D_MODEL, D_MODEL,
                             emit_hn=True, name="peer_query")
    cnt, e1, r2, e2 = _route(q, p['peer_keys'])
    ffn = _peer_dense(hn, shared['u_bf'], shared['vt_bf'], cnt, e1, r2, e2)
    y = _final(x1, ffn, g_c, p['norm_final_g'], scale_f, shift_f, idx)
    return y.reshape(bsz, t_len, D_MODEL), (new_shift, new_wkv, new_conv, new_c, new_n, new_m)


def kernel(x_prompt, x_sample, c_prompt, c_sample, state_rwkv_shift, state_rwkv_wkv, state_mlstm_conv,
           state_mlstm_C, state_mlstm_n, state_mlstm_m, w_ada, b_ada, norm_mix_g, w_in, a_mu, a_w0, a_w2,
           a_a0, a_a2, a_g2, a_kk, a_ka, a_rk, a_lnw, a_lnb, b_conv_w, b_conv_b, b_ig_bias, b_fg_bias,
           b_norm_g, w_out, norm_ffn_g, peer_wq, peer_keys, peer_u, peer_v, w_ada_final, b_ada_final,
           norm_final_g):
    layer = dict(w_ada=w_ada, b_ada=b_ada, norm_mix_g=norm_mix_g, w_in=w_in, a_mu=a_mu, a_w0=a_w0, a_w2=a_w2,
                 a_a0=a_a0, a_a2=a_a2, a_g2=a_g2, a_kk=a_kk, a_ka=a_ka, a_rk=a_rk, a_lnw=a_lnw, a_lnb=a_lnb,
                 b_conv_w=b_conv_w, b_conv_b=b_conv_b, b_ig_bias=b_ig_bias, b_fg_bias=b_fg_bias,
                 b_norm_g=b_norm_g, w_out=w_out, norm_ffn_g=norm_ffn_g, peer_wq=peer_wq, peer_keys=peer_keys,
                 peer_u=peer_u, peer_v=peer_v)
    assert w_ada.shape[0] == 1, "single-layer trunk"
    p = {k: v[0] for k, v in layer.items()}
    p['norm_final_g'] = norm_final_g
    bp, bs = x_prompt.shape[0], x_sample.shape[0]

    c_all = jnp.concatenate([c_prompt, c_sample], axis=0)
    c_all = jnp.pad(c_all, ((0, -c_all.shape[0] % SUBLANES), (0, 0)))
    mod = _ada(c_all, p['w_ada'], p['b_ada'])
    mod_f = _ada(c_all, w_ada_final, b_ada_final)

    head_of = jnp.arange(MIX_A) // A_HEAD
    g1 = (head_of[:, None] == jnp.arange(LANES)[None, :]).astype(bf16)
    zero = jnp.zeros((64, MIX_A), f32)
    wwa = jnp.concatenate([jnp.concatenate([p['a_w2'], zero], axis=1),
                           jnp.concatenate([zero, p['a_a2']], axis=1)], axis=0).astype(bf16)
    w_in_bf = p['w_in'].astype(bf16)
    shared = dict(
        g1=g1, g1t=g1.T, wwa=wwa, w_in_bf=w_in_bf,
        w_gate=jnp.pad(w_in_bf[:, Z_MAIN_COLS:], ((0, 0), (0, LANES - 2 * B_HEADS))),
        w_out_bf=p['w_out'].astype(bf16), wq_bf=p['peer_wq'].astype(bf16),
        u_bf=p['peer_u'].astype(bf16), vt_bf=p['peer_v'].T.astype(bf16))

    dt = x_prompt.dtype
    zero_states = (jnp.zeros((bp, A_SHIFT_COLS), dt), jnp.zeros((bp, A_HEADS, A_HEAD, A_HEAD), dt),
                   jnp.zeros((bp, B_CONV - 1, B_QK_COLS), dt), jnp.zeros((bp, B_HEADS, B_DQK, B_DV), dt),
                   jnp.zeros((bp, B_HEADS, B_DQK), dt), jnp.zeros((bp, B_HEADS), dt))
    y_prompt, p_states = _trunk(x_prompt, mod[:bp], mod_f[:bp], zero_states, p, shared)
    s_states_in = (state_rwkv_shift[0], state_rwkv_wkv[0], state_mlstm_conv[0], state_mlstm_C[0],
                   state_mlstm_n[0], state_mlstm_m[0])
    y_sample, s_states = _trunk(x_sample, mod[bp:bp + bs], mod_f[bp:bp + bs], s_states_in, p, shared)
    return (y_prompt, y_sample) + tuple(s[None] for s in p_states) + tuple(s[None] for s in s_states)
```

```python
import functools

import jax
import jax.numpy as jnp
from jax import lax
from jax.experimental import pallas as pl
from jax.experimental.pallas import tpu as pltpu

f32 = jnp.float32
bf16 = jnp.bfloat16

D_MODEL = 2048
MIX_A = 1024
A_HEAD = 64
A_HEADS = 16
A_SHIFT_COLS = 3 * MIX_A + 64 + 64 + 128
A_GN_EPS = A_HEAD * 1e-5
B_QK_COLS = 1024
B_HEADS = 4
B_DV = 256
B_DQK = 128
B_CONV = 4
B_CHUNK = 64
Z_MAIN_COLS = A_SHIFT_COLS + B_QK_COLS + 2 * 1024
PEER_HEADS = 8
PEER_NKEYS = 128
PEER_TOPK = 16
NORM_EPS = 1e-6

LANES = 128
SUBLANES = 8
VMEM_LIMIT_BYTES = 56 * 1024 * 1024

TM = 512
TM_EW = 256
TM_ROUTE = 512
TE = 1024
TE_HALF = TE // 2


def _params(*sem):
    return pltpu.CompilerParams(dimension_semantics=sem, vmem_limit_bytes=VMEM_LIMIT_BYTES)


def _sds(shape, dtype=f32):
    return jax.ShapeDtypeStruct(shape, dtype)


def _ada_kernel(c_ref, w_ref, b_ref, o_ref):
    c = c_ref[...]
    a = (c * jax.nn.sigmoid(c)).astype(bf16)
    o_ref[...] = jnp.dot(a, w_ref[...].astype(bf16), preferred_element_type=f32) + b_ref[...]


def _ada(c, w, b):
    m, n_out, tn = c.shape[0], w.shape[1], 1024
    return pl.pallas_call(
        _ada_kernel,
        grid=(n_out // tn,),
        in_specs=[pl.BlockSpec((m, D_MODEL), lambda j: (0, 0)),
                  pl.BlockSpec((D_MODEL, tn), lambda j: (0, j)),
                  pl.BlockSpec((1, tn), lambda j: (0, j))],
        out_specs=pl.BlockSpec((m, tn), lambda j: (0, j)),
        out_shape=_sds((m, n_out)),
        compiler_params=_params("arbitrary"),
        name="ada_mod",
    )(c, w, b.reshape(1, n_out))


def _mod3(m, bsz, t_len, tm):
    if t_len % tm == 0:
        per = t_len // tm
        return m.reshape(bsz, 1, D_MODEL), (lambda i: i // per)
    n = bsz * t_len
    assert tm % t_len == 0 and n % tm == 0
    return jnp.repeat(m, t_len, axis=0).reshape(n // tm, tm, D_MODEL), (lambda i: i)


def _mod_spec(arr, idx):
    return pl.BlockSpec((1, arr.shape[1], D_MODEL), lambda i, *_: (idx(i), 0, 0))


def _nmm_kernel(*refs, has_gate, emit_hn):
    it = iter(refs)
    x_ref, g_ref, sc_ref, sh_ref, w_ref = (next(it) for _ in range(5))
    wg_ref = next(it) if has_gate else None
    o_ref = next(it)
    og_ref = next(it) if has_gate else None
    hn_ref = next(it) if emit_hn else None
    hs_ref = next(it)

    @pl.when(pl.program_id(1) == 0)
    def _():
        x = x_ref[...]
        y = x * lax.rsqrt(jnp.mean(x * x, axis=-1, keepdims=True) + NORM_EPS) * g_ref[...]
        h = (y * (1.0 + sc_ref[0]) + sh_ref[0]).astype(bf16)
        hs_ref[...] = h
        if emit_hn:
            hn_ref[...] = h
        if has_gate:
            og_ref[...] = jnp.dot(h, wg_ref[...], preferred_element_type=f32)

    o_ref[...] = jnp.dot(hs_ref[...], w_ref[...], preferred_element_type=f32)


def _norm_mod_matmul(x, g, sc3, sh3, idx, w_bf, n_out, tn, w_gate=None, emit_hn=False, name="nmm"):
    n = x.shape[0]
    has_gate = w_gate is not None
    in_specs = [pl.BlockSpec((TM, D_MODEL), lambda i, j: (i, 0)),
                pl.BlockSpec((1, D_MODEL), lambda i, j: (0, 0)),
                _mod_spec(sc3, idx), _mod_spec(sh3, idx),
                pl.BlockSpec((D_MODEL, tn), lambda i, j: (0, j))]
    args = [x, g.reshape(1, D_MODEL), sc3, sh3, w_bf]
    out_specs = [pl.BlockSpec((TM, tn), lambda i, j: (i, j))]
    out_shape = [_sds((n, n_out))]
    if has_gate:
        in_specs.append(pl.BlockSpec((D_MODEL, LANES), lambda i, j: (0, 0)))
        args.append(w_gate)
        out_specs.append(pl.BlockSpec((TM, LANES), lambda i, j: (i, 0)))
        out_shape.append(_sds((n, LANES)))
    if emit_hn:
        out_specs.append(pl.BlockSpec((TM, D_MODEL), lambda i, j: (i, 0)))
        out_shape.append(_sds((n, D_MODEL), bf16))
    return pl.pallas_call(
        functools.partial(_nmm_kernel, has_gate=has_gate, emit_hn=emit_hn),
        grid=(n // TM, n_out // tn),
        in_specs=in_specs, out_specs=out_specs, out_shape=out_shape,
        scratch_shapes=[pltpu.VMEM((TM, D_MODEL), bf16)],
        compiler_params=_params("arbitrary", "arbitrary"),
        name=name,
    )(*args)


def _split_dot(x, w):
    hi = x.astype(bf16)
    lo = (x - hi.astype(f32)).astype(bf16)
    return (jnp.dot(hi, w, preferred_element_type=f32) + jnp.dot(lo, w, preferred_element_type=f32))


def _head_sum(x, g1, g1t):
    return _split_dot(_split_dot(x, g1), g1t)


def _rwkv_prep_kernel(z_ref, first_ref, mu_ref, w0_ref, wwa_ref, a0_ref, g2_ref, akk_ref, aka_ref, rk_ref,
                      g1_ref, g1t_ref, r_ref, k_ref, v_ref, d_ref, kk_ref, kka_ref, g_ref, bonus_ref,
                      *scratch, t_len, carry_mode):
    z = z_ref[...]
    tm = z.shape[0]
    row = lax.broadcasted_iota(jnp.int32, z.shape, 0)
    rolled = pltpu.roll(z, 1, axis=0)
    if carry_mode:
        carry_ref, = scratch
        first = jnp.where(pl.program_id(1) == 0, first_ref[0], carry_ref[0:1, :])
        prev = jnp.where(row == 0, first, rolled)
        carry_ref[0:1, :] = z[tm - 1:tm, :]
    else:
        prev = jnp.where(lax.rem(row, t_len) == 0, first_ref[...], rolled)
    zs = z + (prev - z) * mu_ref[...]
    r = zs[:, 0:MIX_A]
    k = zs[:, MIX_A:2 * MIX_A]
    v = zs[:, 2 * MIX_A:3 * MIX_A]
    xwa = zs[:, 3 * MIX_A:3 * MIX_A + 128]
    xg = zs[:, 3 * MIX_A + 128:3 * MIX_A + 256]
    lane = lax.broadcasted_iota(jnp.int32, xwa.shape, 1)
    xwa = jnp.where(lane < 64, jnp.tanh(xwa), xwa)
    lwa = jnp.dot(xwa.astype(bf16), wwa_ref[...], preferred_element_type=f32)
    u = w0_ref[...] + lwa[:, 0:MIX_A]
    w_log = -(jnp.maximum(-u, 0.0) + jnp.log(1.0 + jnp.exp(-jnp.abs(u)))) - 0.5
    decay = jnp.exp(-jnp.exp(w_log))
    a = jax.nn.sigmoid(a0_ref[...] + lwa[:, MIX_A:2 * MIX_A])
    g_ref[...] = jnp.dot(jax.nn.sigmoid(xg).astype(bf16), g2_ref[...].astype(bf16), preferred_element_type=f32)
    g1, g1t = g1_ref[...], g1t_ref[...]
    kk = k * akk_ref[...]
    nrm = jnp.sqrt(_head_sum(kk * kk, g1, g1t))
    kk = kk / jnp.maximum(nrm, 1e-12)
    k = k * (1.0 + (a - 1.0) * aka_ref[...])
    bonus_ref[...] = _head_sum(r * k * rk_ref[...], g1, g1t) * v
    for ref, val in ((r_ref, r), (k_ref, k), (v_ref, v), (d_ref, decay), (kk_ref, kk), (kka_ref, -(kk * a))):
        if carry_mode:
            ref[0] = val.T
        else:
            ref[...] = val


def _rwkv_prep(z, shift_buf, bsz, t_len, p, wwa, g1, g1t):
    n = bsz * t_len
    row = lambda a: a.reshape(1, -1)
    consts = [row(p['a_mu']), row(p['a_w0']), wwa, row(p['a_a0']), p['a_g2'], row(p['a_kk']), row(p['a_ka']),
              row(p['a_rk']), g1, g1t]
    carry_mode = t_len % TM_EW == 0
    if carry_mode:
        per = t_len // TM_EW
        grid = (bsz, per)
        tok = lambda w: pl.BlockSpec((TM_EW, w), lambda b, i: (b * per + i, 0))
        first = shift_buf.reshape(bsz, 1, A_SHIFT_COLS)
        first_spec = pl.BlockSpec((1, 1, A_SHIFT_COLS), lambda b, i: (b, 0, 0))
        scratch = [pltpu.VMEM((SUBLANES, A_SHIFT_COLS), f32)]
        sem = ("arbitrary", "arbitrary")
        scan_spec = pl.BlockSpec((1, MIX_A, TM_EW), lambda b, i: (b, 0, i))
        scan_shape = _sds((bsz, MIX_A, t_len))
    else:
        assert TM_EW % t_len == 0 and n % TM_EW == 0
        grid = (n // TM_EW,)
        tok = lambda w: pl.BlockSpec((TM_EW, w), lambda i: (i, 0))
        first = jnp.repeat(shift_buf, t_len, axis=0)
        first_spec = tok(A_SHIFT_COLS)
        scratch = []
        sem = ("arbitrary",)
        scan_spec, scan_shape = tok(MIX_A), _sds((n, MIX_A))
    full = lambda a: pl.BlockSpec(a.shape, lambda *_: (0,) * a.ndim)
    return pl.pallas_call(
        functools.partial(_rwkv_prep_kernel, t_len=t_len, carry_mode=carry_mode),
        grid=grid,
        in_specs=[tok(A_SHIFT_COLS), first_spec] + [full(a) for a in consts],
        out_specs=[scan_spec] * 6 + [tok(MIX_A)] * 2,
        out_shape=[scan_shape] * 6 + [_sds((n, MIX_A))] * 2,
        scratch_shapes=scratch,
        compiler_params=_params(*sem),
        name="rwkv_prep",
    )(z, first, *consts)


SCAN_CHUNK = 64
RELAYOUT_T = 128


def _rwkv_scan_kernel(r_ref, d_ref, k_ref, kk_ref, kka_ref, v_ref, s0_ref, y_ref, so_ref, *, n_rows, t_chunk):
    @pl.when(pl.program_id(1) == 0)
    def _():
        so_ref[...] = s0_ref[...]

    def step(t, carry):
        kk, d, kka, k, r = kk_ref[t], d_ref[t], kka_ref[t], k_ref[t], r_ref[t]
        for i in range(n_rows):
            s_i = so_ref[i]
            sa = jnp.sum(s_i * kk, axis=0, keepdims=True)
            v_i = v_ref[t, pl.ds(i, 1), :]
            s_n = s_i * d + sa * kka + v_i * k
            so_ref[i] = s_n
            y_ref[t, pl.ds(i, 1), :] = jnp.sum(s_n * r, axis=0, keepdims=True)
        return carry

    lax.fori_loop(0, t_chunk, step, 0)


def _rwkv_scan(r, d, k, kk, kka, v, s0, t_chunk):
    t_len, n_rows, ltot = v.shape
    jspec = pl.BlockSpec((t_chunk, A_HEAD, LANES), lambda l, t: (t, 0, l))
    vspec = pl.BlockSpec((t_chunk, n_rows, LANES), lambda l, t: (t, 0, l))
    sspec = pl.BlockSpec((n_rows, A_HEAD, LANES), lambda l, t: (0, 0, l))
    return pl.pallas_call(
        functools.partial(_rwkv_scan_kernel, n_rows=n_rows, t_chunk=t_chunk),
        grid=(ltot // LANES, t_len // t_chunk),
        in_specs=[jspec] * 5 + [vspec, sspec],
        out_specs=[vspec, sspec],
        out_shape=[_sds(v.shape), _sds((n_rows, A_HEAD, ltot))],
        compiler_params=_params("arbitrary", "arbitrary"),
        name="rwkv_scan",
    )(r, d, k, kk, kka, v, s0)


def _to_scan_kernel(x_ref, o_ref, *, bsz, pack, val_mode):
    n_idx = A_HEAD // pack if val_mode else A_HEAD
    for j0 in range(0, n_idx, SUBLANES):
        tiles = []
        for j in range(j0, j0 + SUBLANES):
            if val_mode:
                rows = [x_ref[b, pl.ds(g * n_idx + j, A_HEADS, stride=A_HEAD), :]
                        for g in range(pack) for b in range(bsz)]
            else:
                rows = [x_ref[b, pl.ds(j, A_HEADS, stride=A_HEAD), :] for b in range(bsz)] * pack
            tiles.append(jnp.concatenate(rows, axis=0).T)
        o_ref[:, j0:j0 + SUBLANES, :] = jnp.swapaxes(jnp.stack(tiles, axis=0), 0, 1)


def _to_scan_layout(a, bsz, t_len, pack, val_mode):
    per = t_len // RELAYOUT_T
    n_idx = A_HEAD // pack if val_mode else A_HEAD
    return pl.pallas_call(
        functools.partial(_to_scan_kernel, bsz=bsz, pack=pack, val_mode=val_mode),
        grid=(per,),
        in_specs=[pl.BlockSpec((bsz, MIX_A, RELAYOUT_T), lambda tb: (0, 0, tb))],
        out_specs=pl.BlockSpec((RELAYOUT_T, n_idx, LANES), lambda tb: (tb, 0, 0)),
        out_shape=_sds((t_len, n_idx, LANES)),
        compiler_params=_params("arbitrary"),
        name="to_scan_val" if val_mode else "to_scan_key",
    )(a)


def _from_scan_kernel(y_ref, o_ref, pt_ref, *, bsz, pack):
    n_idx = A_HEAD // pack
    for j0 in range(0, n_idx, SUBLANES):
        tiles = jnp.swapaxes(y_ref[:, j0:j0 + SUBLANES, :], 0, 1)
        for j in range(j0, j0 + SUBLANES):
            tt = tiles[j - j0].T
            for g in range(pack):
                for b in range(bsz):
                    lo = (g * bsz + b) * A_HEADS
                    pt_ref[b, pl.ds(g * n_idx + j, A_HEADS, stride=A_HEAD), :] = tt[lo:lo + A_HEADS, :]
    for b in range(bsz):
        o_ref[b] = pt_ref[b].T


def _from_scan_layout(y, bsz, t_len, pack):
    per = t_len // RELAYOUT_T
    n_idx = A_HEAD // pack
    out = pl.pallas_call(
        functools.partial(_from_scan_kernel, bsz=bsz, pack=pack),
        grid=(per,),
        in_specs=[pl.BlockSpec((RELAYOUT_T, n_idx, LANES), lambda tb: (tb, 0, 0))],
        out_specs=pl.BlockSpec((bsz, RELAYOUT_T, MIX_A), lambda tb: (0, tb, 0)),
        out_shape=_sds((bsz, t_len, MIX_A)),
        scratch_shapes=[pltpu.VMEM((bsz, MIX_A, RELAYOUT_T), f32)],
        compiler_params=_params("arbitrary"),
        name="from_scan",
    )(y)
    return out.reshape(bsz * t_len, MIX_A)


def _rwkv_post_kernel(y_ref, bonus_ref, g_ref, lnw_ref, lnb_ref, g1_ref, g1t_ref, o_ref):
    y = y_ref[...]
    g1, g1t = g1_ref[...], g1t_ref[...]
    mu = _head_sum(y, g1, g1t) * (1.0 / A_HEAD)
    yc = y - mu
    var = _head_sum(yc * yc, g1, g1t) * (1.0 / A_HEAD)
    yn = yc * lax.rsqrt(var + A_GN_EPS) * lnw_ref[...] + lnb_ref[...]
    o_ref[...] = (yn + bonus_ref[...]) * g_ref[...]


def _rwkv_post(y, bonus, g, p, g1, g1t):
    n = y.shape[0]
    tok = pl.BlockSpec((TM_EW, MIX_A), lambda i: (i, 0))
    full = lambda a: pl.BlockSpec(a.shape, lambda i: (0,) * a.ndim)
    consts = [p['a_lnw'].reshape(1, -1), p['a_lnb'].reshape(1, -1), g1, g1t]
    return pl.pallas_call(
        _rwkv_post_kernel,
        grid=(n // TM_EW,),
        in_specs=[tok] * 3 + [full(a) for a in consts],
        out_specs=tok, out_shape=_sds((n, MIX_A)),
        compiler_params=_params("arbitrary"),
        name="rwkv_post",
    )(y, bonus, g, *consts)


def _rwkv_mix(z, bsz, t_len, shift_buf, wkv0, p, wwa, g1, g1t):
    n = bsz * t_len
    r, k, v, d, kk, kka, g, bonus = _rwkv_prep(z, shift_buf, bsz, t_len, p, wwa, g1, g1t)

    bh = bsz * A_HEADS
    pack = LANES // bh if bh < LANES else 1
    n_rows = A_HEAD // pack

    s0 = wkv0.reshape(bsz, A_HEADS, pack, n_rows, A_HEAD).transpose(3, 4, 2, 0, 1).reshape(n_rows, A_HEAD, pack * bh)
    if pack * bh == LANES and t_len % RELAYOUT_T == 0:
        keys = [_to_scan_layout(a, bsz, t_len, pack, False) for a in (r, d, k, kk, kka)]
        y, s_out = _rwkv_scan(*keys, _to_scan_layout(v, bsz, t_len, pack, True), s0, SCAN_CHUNK)
        y = _from_scan_layout(y, bsz, t_len, pack)
    else:
        assert pack == 1 and t_len <= SCAN_CHUNK

        def scan_rows(a):
            return a.reshape(bsz, t_len, A_HEADS, A_HEAD).transpose(1, 3, 0, 2).reshape(t_len, A_HEAD, bh)

        y, s_out = _rwkv_scan(*(scan_rows(a) for a in (r, d, k, kk, kka, v)), s0, t_len)
        y = y.reshape(t_len, A_HEAD, bsz, A_HEADS).transpose(2, 0, 3, 1).reshape(n, MIX_A)
    new_wkv = s_out.reshape(n_rows, A_HEAD, pack, bsz, A_HEADS).transpose(3, 4, 2, 0, 1)
    new_wkv = new_wkv.reshape(bsz, A_HEADS, A_HEAD, A_HEAD)
    y = _rwkv_post(y, bonus, g, p, g1, g1t)
    new_shift = z.reshape(bsz, t_len, Z_MAIN_COLS)[:, -1, :A_SHIFT_COLS]
    return y, new_shift, new_wkv


def _log_sigmoid(x):
    return jnp.minimum(x, 0.0) - jnp.log(1.0 + jnp.exp(-jnp.abs(x)))


def _mlstm_kernel(*refs, chunk, bb):
    zq_refs = refs[0:2 * B_HEADS]
    v_refs = refs[2 * B_HEADS:3 * B_HEADS]
    o_refs = refs[3 * B_HEADS:4 * B_HEADS]
    (cw_ref, cb_ref, cbuf_ref, gc_ref, gr_ref, bias_r_ref, bias_c_ref, ng_ref, c0_ref, n0_ref, m0_ref,
     y_ref, c_ref, n_ref, m_ref, tail_ref) = refs[4 * B_HEADS:]
    n_tail = B_CONV - 1

    @pl.when(pl.program_id(1) == 0)
    def _():
        c_ref[...] = c0_ref[...]
        n_ref[...] = n0_ref[...]
        m_ref[...] = m0_ref[...]
        tail_ref[...] = jnp.zeros_like(tail_ref)
        for bi in range(bb):
            tail_ref[(bi + 1) * SUBLANES - n_tail:(bi + 1) * SUBLANES, :] = cbuf_ref[bi]

    def conv_silu(cb, bi):
        cols = slice(cb * B_DQK, (cb + 1) * B_DQK)
        blk = zq_refs[cb][bi]
        zp = jnp.concatenate([tail_ref[bi * SUBLANES:(bi + 1) * SUBLANES, cols], blk], axis=0)
        acc = cb_ref[:, cols]
        for w in range(B_CONV):
            off = SUBLANES - n_tail + w
            acc = acc + zp[off:off + chunk, :] * cw_ref[w:w + 1, cols]
        tail_ref[(bi + 1) * SUBLANES - n_tail:(bi + 1) * SUBLANES, cols] = blk[chunk - n_tail:chunk, :]
        return acc * jax.nn.sigmoid(acc)

    ti = lax.broadcasted_iota(jnp.int32, (chunk, chunk), 0)
    si = lax.broadcasted_iota(jnp.int32, (chunk, chunk), 1)
    tri = si <= ti
    nt = (((1,), (1,)), ((), ()))
    tn = (((0,), (0,)), ((), ()))
    def head_chain(bi, h):
        gcol = gc_ref[bi] + bias_r_ref[...]
        grow = gr_ref[bi, 0] + bias_c_ref[...]
        q = conv_silu(h, bi)
        k = conv_silu(B_HEADS + h, bi) * (B_DQK ** -0.5)
        v = v_refs[h][bi]
        yield
        i_col, f_col = gcol[:, h:h + 1], gcol[:, B_HEADS + h:B_HEADS + h + 1]
        i_row, f_row = grow[h:h + 1, :], grow[B_HEADS + h:B_HEADS + h + 1, :]
        lf_col, lf_row = _log_sigmoid(f_col), _log_sigmoid(f_row)
        b_col = jnp.sum(jnp.where(tri, lf_row, 0.0), axis=1, keepdims=True)
        b_row = jnp.sum(jnp.where(ti <= si, lf_col, 0.0), axis=0, keepdims=True)
        m_prev = m_ref[bi, h:h + 1, :]
        yield
        log_d = jnp.where(tri, b_col - b_row + i_row, -jnp.inf)
        inter = b_col + m_prev
        m_t = jnp.maximum(inter, jnp.max(log_d, axis=1, keepdims=True))
        qb, kb, vb = q.astype(bf16), k.astype(bf16), v.astype(bf16)
        yield
        s = lax.dot_general(qb, kb, nt, preferred_element_type=f32) * jnp.exp(log_d - m_t)
        w_inter = jnp.exp(inter - m_t)
        c_old = c_ref[bi, h]
        n_old = n_ref[bi, h:h + 1, :]
        yield
        num = (jnp.dot(s.astype(bf16), vb, preferred_element_type=f32)
               + w_inter * jnp.dot(qb, c_old.astype(bf16), preferred_element_type=f32))
        den = jnp.sum(s, axis=1, keepdims=True) + w_inter * jnp.sum(q * n_old, axis=1, keepdims=True)
        yield
        hh = num / jnp.maximum(jnp.abs(den), jnp.exp(-m_t))
        m_new = m_t[chunk - 1:chunk, :]
        b_last = b_col[chunk - 1:chunk, :]
        ws = jnp.exp(b_last - b_col + i_col - m_new)
        d_c = jnp.exp(b_last + m_prev - m_new)
        kw = k * ws
        yield
        c_ref[bi, h] = d_c * c_old + lax.dot_general(kw.astype(bf16), vb, tn, preferred_element_type=f32)
        n_ref[bi, h:h + 1, :] = d_c * n_old + jnp.sum(kw, axis=0, keepdims=True)
        m_ref[bi, h:h + 1, :] = m_new
        yield
        hn = hh * lax.rsqrt(jnp.mean(hh * hh, axis=1, keepdims=True) + NORM_EPS)
        y_ref[bi, :, h * B_DV:(h + 1) * B_DV] = (hn * ng_ref[:, h * B_DV:(h + 1) * B_DV]
                                                 * jax.nn.sigmoid(o_refs[h][bi]))

    _interleave(*[head_chain(bi, h) for bi in range(bb) for h in range(B_HEADS)])


def _interleave(*gens):
    live = list(gens)
    while live:
        for gen in list(live):
            try:
                next(gen)
            except StopIteration:
                live.remove(gen)


MLSTM_BATCH = 4


def _mlstm(z, conv_buf, gates_col, gates_row, p, c0, n0, m0, bsz, n_chunks, chunk):
    t_pad = n_chunks * chunk
    bb = MLSTM_BATCH if bsz % MLSTM_BATCH == 0 else 1
    qk_col0 = A_SHIFT_COLS // B_DQK
    v_col0 = (A_SHIFT_COLS + B_QK_COLS) // B_DV
    o_col0 = v_col0 + B_HEADS
    tok = lambda w: pl.BlockSpec((bb, chunk, w), lambda b, c: (b, c, 0))
    zcol = lambda w, cb: pl.BlockSpec((bb, chunk, w), lambda b, c: (b, c, cb))
    full = lambda a: pl.BlockSpec(a.shape, lambda b, c: (0,) * a.ndim)
    bias = jnp.concatenate([p['b_ig_bias'], p['b_fg_bias']])
    bias_r = jnp.pad(bias, (0, LANES - 2 * B_HEADS)).reshape(1, LANES)
    bias_c = bias.reshape(2 * B_HEADS, 1)
    ng = p['b_norm_g'].reshape(1, -1)
    cw, cb = p['b_conv_w'], p['b_conv_b'].reshape(1, -1)
    cspec = pl.BlockSpec((bb, B_HEADS, B_DQK, B_DV), lambda b, c: (b, 0, 0, 0))
    nspec = pl.BlockSpec((bb, B_HEADS, B_DQK), lambda b, c: (b, 0, 0))
    mspec = pl.BlockSpec((bb, B_HEADS, 1), lambda b, c: (b, 0, 0))
    return pl.pallas_call(
        functools.partial(_mlstm_kernel, chunk=chunk, bb=bb),
        grid=(bsz // bb, n_chunks),
        in_specs=[zcol(B_DQK, qk_col0 + j) for j in range(2 * B_HEADS)]
                 + [zcol(B_DV, v_col0 + h) for h in range(B_HEADS)]
                 + [zcol(B_DV, o_col0 + h) for h in range(B_HEADS)]
                 + [full(cw), full(cb), pl.BlockSpec((bb, B_CONV - 1, B_QK_COLS), lambda b, c: (b, 0, 0)),
                    tok(LANES), pl.BlockSpec((bb, 1, 2 * B_HEADS, chunk), lambda b, c: (b, c, 0, 0)),
                    full(bias_r), full(bias_c), full(ng), cspec, nspec, mspec],
        out_specs=[tok(B_HEADS * B_DV), cspec, nspec, mspec],
        out_shape=[_sds((bsz, t_pad, B_HEADS * B_DV)), _sds(c0.shape), _sds(n0.shape), _sds(m0.shape)],
        scratch_shapes=[pltpu.VMEM((bb * SUBLANES, B_QK_COLS), f32)],
        compiler_params=_params("arbitrary", "arbitrary"),
        name="mlstm",
    )(*([z.reshape(bsz, t_pad, -1)] * (4 * B_HEADS)), cw, cb, conv_buf, gates_col, gates_row, bias_r, bias_c, ng,
      c0, n0, m0)


def _mlstm_mix(z, gates, bsz, t_len, conv_buf, c0, n0, m0, p):
    n = bsz * t_len
    z3 = z.reshape(bsz, t_len, Z_MAIN_COLS)
    zqk = z3[:, :, A_SHIFT_COLS:A_SHIFT_COLS + B_QK_COLS]
    new_conv = jnp.concatenate([conv_buf, zqk[:, max(t_len - (B_CONV - 1), 0):]], axis=1)[:, -(B_CONV - 1):]
    chunk = B_CHUNK if t_len % B_CHUNK == 0 else SUBLANES
    t_pad = -(-t_len // chunk) * chunk
    g3 = gates.reshape(bsz, t_len, LANES)
    if t_pad != t_len:
        z = jnp.pad(z3, ((0, 0), (0, t_pad - t_len), (0, 0))).reshape(bsz * t_pad, -1)
        lane = jnp.arange(LANES)
        fill = jnp.where(lane < B_HEADS, -1e30, jnp.where(lane < 2 * B_HEADS, 1e30, 0.0)).astype(f32)
        g3 = jnp.concatenate([g3, jnp.broadcast_to(fill, (bsz, t_pad - t_len, LANES))], axis=1)
    n_chunks = t_pad // chunk
    gates_col = g3
    gates_row = g3[:, :, :2 * B_HEADS].reshape(bsz, n_chunks, chunk, 2 * B_HEADS).transpose(0, 1, 3, 2)
    y, c_new, n_new, m_new = _mlstm(z, conv_buf, gates_col, gates_row, p, c0, n0, m0.reshape(bsz, B_HEADS, 1),
                                    bsz, n_chunks, chunk)
    y = y[:, :t_len].reshape(n, -1)
    return y, new_conv, c_new, n_new, m_new.reshape(bsz, B_HEADS)


def _oproj_kernel(ya_ref, yb_ref, wt_ref, wb_ref, x_ref, ga_ref, o_ref):
    acc = jnp.dot(ya_ref[...].astype(bf16), wt_ref[...], preferred_element_type=f32)
    acc = acc + jnp.dot(yb_ref[...].astype(bf16), wb_ref[...], preferred_element_type=f32)
    o_ref[...] = x_ref[...] + ga_ref[0] * acc


def _oproj(ya, yb, w_out_bf, x, ga3, idx):
    n = x.shape[0]
    half = D_MODEL // 2
    tok = lambda w: pl.BlockSpec((TM, w), lambda i: (i, 0))
    return pl.pallas_call(
        _oproj_kernel,
        grid=(n // TM,),
        in_specs=[tok(half), tok(half),
                  pl.BlockSpec((half, D_MODEL), lambda i: (0, 0)), pl.BlockSpec((half, D_MODEL), lambda i: (1, 0)),
                  tok(D_MODEL), _mod_spec(ga3, idx)],
        out_specs=tok(D_MODEL),
        out_shape=_sds((n, D_MODEL)),
        compiler_params=_params("arbitrary"),
        name="out_proj",
    )(ya, yb, w_out_bf, w_out_bf, x, ga3)


def _route_kernel(q_ref, keys_ref, cnt_ref, e1_ref, r2_ref, e2_ref, v1_scr, v2_scr):
    q = q_ref[...]
    tm = q.shape[0]
    neg_inf = -jnp.inf
    iota_k = lax.broadcasted_iota(jnp.int32, (PEER_NKEYS, tm), 0).astype(f32)
    nt = (((1,), (1,)), ((), ()))
    ranks, es = [], []
    for half, vscr in enumerate((v1_scr, v2_scr)):
        qp = q[:, half * 128:(half + 1) * 128].astype(bf16)
        s = lax.dot_general(keys_ref[half, 0].astype(bf16), qp, nt, preferred_element_type=f32)
        rank = jnp.full((PEER_NKEYS, tm), float(PEER_TOPK), f32)
        work = s
        for a in range(PEER_TOPK):
            m = jnp.max(work, axis=0, keepdims=True)
            idx = jnp.min(jnp.where(work == m, iota_k, float(PEER_NKEYS)), axis=0, keepdims=True)
            sel = iota_k == idx
            rank = jnp.where(sel, float(a), rank)
            work = jnp.where(sel, neg_inf, work)
            vscr[a:a + 1, :] = m
        ranks.append(rank)
        es.append(jnp.exp(s - vscr[0:1, :]))
    v1, v2 = v1_scr[...], v2_scr[...]
    row8 = lax.broadcasted_iota(jnp.int32, (SUBLANES, tm), 0).astype(f32)
    row16 = lax.broadcasted_iota(jnp.int32, (PEER_TOPK, tm), 0).astype(f32)
    pieces, poss = [v1[0:1, :] + v2], [row16]
    for a in range(1, 8):
        nb = PEER_TOPK // (a + 1)
        pieces.append(jnp.where(row8 < float(nb), v1[a:a + 1, :] + v2[0:SUBLANES, :], neg_inf))
        poss.append(row8 + float(a * PEER_TOPK))
    pieces.append(v1[8:16, :] + v2[0:1, :])
    poss.append((row8 + 8.0) * float(PEER_TOPK))
    cand = jnp.concatenate(pieces, axis=0)
    pos = jnp.concatenate(poss, axis=0)
    cnt16 = jnp.zeros((PEER_TOPK, tm), f32)
    zsum = None
    best0 = None
    for j in range(PEER_TOPK):
        m = jnp.max(cand, axis=0, keepdims=True)
        pj = jnp.min(jnp.where(cand == m, pos, 1e9), axis=0, keepdims=True)
        cand = jnp.where(pos == pj, neg_inf, cand)
        if j == 0:
            best0 = m
            zsum = jnp.ones_like(m)
        else:
            zsum = zsum + jnp.exp(m - best0)
        a_j = jnp.floor(pj * (1.0 / PEER_TOPK))
        cnt16 = cnt16 + jnp.where(row16 == a_j, 1.0, 0.0)
    cnt_i = jnp.zeros((PEER_NKEYS, tm), f32)
    for a in range(PEER_TOPK):
        cnt_i = jnp.where(ranks[0] == float(a), cnt16[a:a + 1, :], cnt_i)
    cnt_ref[0] = cnt_i
    e1_ref[0] = es[0] / zsum
    r2_ref[0] = ranks[1].astype(bf16)
    e2_ref[0] = es[1].astype(bf16)


def _route(q, keys):
    n = q.shape[0]
    ospec = pl.BlockSpec((1, PEER_NKEYS, TM_ROUTE), lambda i, h: (h, 0, i))
    oshape = (PEER_HEADS, PEER_NKEYS, n)
    return pl.pallas_call(
        _route_kernel,
        grid=(n // TM_ROUTE, PEER_HEADS),
        in_specs=[pl.BlockSpec((TM_ROUTE, 256), lambda i, h: (i, h)),
                  pl.BlockSpec((2, 1, PEER_NKEYS, 128), lambda i, h: (0, h, 0, 0))],
        out_specs=[ospec] * 4,
        out_shape=[_sds(oshape), _sds(oshape), _sds(oshape, bf16), _sds(oshape, bf16)],
        scratch_shapes=[pltpu.VMEM((PEER_TOPK, TM_ROUTE), f32), pltpu.VMEM((PEER_TOPK, TM_ROUTE), f32)],
        compiler_params=_params("arbitrary", "arbitrary"),
        name="peer_route",
    )(q, keys)


def _gated_act(h, cnt_ref, e1_ref, r2_ref, e2_ref, row0):
    zero = jnp.zeros((), bf16)
    parts = []
    for q in range(TE_HALF // PEER_NKEYS):
        row = row0 + q
        hs = h[q * PEER_NKEYS:(q + 1) * PEER_NKEYS, :]
        gate = None
        for hd in range(PEER_HEADS):
            cnt_row = cnt_ref[hd, row:row + 1, :].astype(bf16)
            e1_row = e1_ref[hd, row:row + 1, :].astype(bf16)
            term = jnp.where(r2_ref[hd] < cnt_row, e2_ref[hd] * e1_row, zero)
            gate = term if gate is None else gate + term
        gelu = 0.5 * hs * (1.0 + lax.erf(hs * (2.0 ** -0.5)))
        parts.append(gelu.astype(bf16) * gate)
    return jnp.concatenate(parts, axis=0)


def _dense_kernel(hn_ref, u_ref, vt_ref, vtp_ref, cnt_ref, e1_ref, cntp_ref, e1p_ref, r2_ref, e2_ref, o_ref,
                  xt_scr, acc_scr, hb_scr, *, n_chunks):
    c = pl.program_id(1)

    @pl.when(c == 0)
    def _():
        xt_scr[...] = hn_ref[...].T
        acc_scr[...] = jnp.zeros_like(acc_scr)
        hb_scr[...] = jnp.zeros_like(hb_scr)

    sub = TE_HALF // PEER_NKEYS
    xt = xt_scr[...]
    h_a = jnp.dot(u_ref[0:TE_HALF, :], xt, preferred_element_type=f32)
    act_bp = _gated_act(hb_scr[...], cntp_ref, e1p_ref, r2_ref, e2_ref, sub)
    part = jnp.dot(vtp_ref[...], act_bp, preferred_element_type=f32)
    hb_scr[...] = jnp.dot(u_ref[TE_HALF:TE, :], xt, preferred_element_type=f32)
    act_a = _gated_act(h_a, cnt_ref, e1_ref, r2_ref, e2_ref, 0)
    part = jnp.dot(vt_ref[:, 0:TE_HALF], act_a, preferred_element_type=f32) + part
    acc_scr[...] += part

    @pl.when(c == n_chunks - 1)
    def _():
        act_b = _gated_act(hb_scr[...], cnt_ref, e1_ref, r2_ref, e2_ref, sub)
        total = acc_scr[...] + jnp.dot(vt_ref[:, TE_HALF:TE], act_b, preferred_element_type=f32)
        o_ref[...] = total.T


def _peer_dense(hn, u_bf, vt_bf, cnt, e1, r2, e2):
    n = hn.shape[0]
    n_exp = u_bf.shape[0]
    n_chunks = n_exp // TE
    sub = TE // PEER_NKEYS
    prev = lambda c: jnp.maximum(c - 1, 0)
    return pl.pallas_call(
        functools.partial(_dense_kernel, n_chunks=n_chunks),
        grid=(n // TM, n_chunks),
        in_specs=[pl.BlockSpec((TM, D_MODEL), lambda i, c: (i, 0)),
                  pl.BlockSpec((TE, D_MODEL), lambda i, c: (c, 0)),
                  pl.BlockSpec((D_MODEL, TE), lambda i, c: (0, c)),
                  pl.BlockSpec((D_MODEL, TE_HALF), lambda i, c: (0, jnp.maximum(2 * c - 1, 0))),
                  pl.BlockSpec((PEER_HEADS, sub, TM), lambda i, c: (0, c, i)),
                  pl.BlockSpec((PEER_HEADS, sub, TM), lambda i, c: (0, c, i)),
                  pl.BlockSpec((PEER_HEADS, sub, TM), lambda i, c: (0, prev(c), i)),
                  pl.BlockSpec((PEER_HEADS, sub, TM), lambda i, c: (0, prev(c), i)),
                  pl.BlockSpec((PEER_HEADS, PEER_NKEYS, TM), lambda i, c: (0, 0, i)),
                  pl.BlockSpec((PEER_HEADS, PEER_NKEYS, TM), lambda i, c: (0, 0, i))],
        out_specs=pl.BlockSpec((TM, D_MODEL), lambda i, c: (i, 0)),
        out_shape=_sds((n, D_MODEL)),
        scratch_shapes=[pltpu.VMEM((D_MODEL, TM), bf16), pltpu.VMEM((D_MODEL, TM), f32),
                        pltpu.VMEM((TE_HALF, TM), f32)],
        compiler_params=_params("arbitrary", "arbitrary"),
        name="peer_dense",
    )(hn, u_bf, vt_bf, vt_bf, cnt, e1, cnt, e1, r2, e2)


def _peer_weights_kernel(u_ref, v_ref, ub_ref, vt_ref):
    ub_ref[...] = u_ref[...].astype(bf16)
    vt_ref[...] = v_ref[...].astype(bf16).T


def _peer_weights(u, v):
    n_exp, te = u.shape[0], 512
    rows = pl.BlockSpec((te, D_MODEL), lambda i: (i, 0))
    return pl.pallas_call(
        _peer_weights_kernel,
        grid=(n_exp // te,),
        in_specs=[rows, rows],
        out_specs=[rows, pl.BlockSpec((D_MODEL, te), lambda i: (0, i))],
        out_shape=[_sds((n_exp, D_MODEL), bf16), _sds((D_MODEL, n_exp), bf16)],
        compiler_params=_params("arbitrary"),
        name="peer_weights",
    )(u, v)


def _final_kernel(x_ref, f_ref, gc_ref, g_ref, sc_ref, sh_ref, o_ref):
    x = x_ref[...] + gc_ref[0] * f_ref[...]
    y = x * lax.rsqrt(jnp.mean(x * x, axis=-1, keepdims=True) + NORM_EPS) * g_ref[...]
    o_ref[...] = y * (1.0 + sc_ref[0]) + sh_ref[0]


def _final(x, ffn, gc3, g, sc3, sh3, idx):
    n = x.shape[0]
    tok = pl.BlockSpec((TM, D_MODEL), lambda i: (i, 0))
    return pl.pallas_call(
        _final_kernel,
        grid=(n // TM,),
        in_specs=[tok, tok, _mod_spec(gc3, idx), pl.BlockSpec((1, D_MODEL), lambda i: (0, 0)),
                  _mod_spec(sc3, idx), _mod_spec(sh3, idx)],
        out_specs=tok, out_shape=_sds((n, D_MODEL)),
        compiler_params=_params("arbitrary"),
        name="final_norm",
    )(x, ffn, gc3, g.reshape(1, D_MODEL), sc3, sh3)


def _trunk(x, mod, mod_f, states, p, shared):
    bsz, t_len, _ = x.shape
    n = bsz * t_len
    shift_s, wkv_s, conv_s, c_s, n_s, m_s = states
    xf = x.reshape(n, D_MODEL)
    m3 = [_mod3(m, bsz, t_len, TM) for m in jnp.split(mod, 6, axis=-1)]
    idx = m3[0][1]
    sh_a, sc_a, g_a, sh_c, sc_c, g_c = (a for a, _ in m3)
    (shift_f, _), (scale_f, _) = (_mod3(m, bsz, t_len, TM) for m in jnp.split(mod_f, 2, axis=-1))

    z, gates = _norm_mod_matmul(xf, p['norm_mix_g'], sc_a, sh_a, idx, shared['w_in_bf'], Z_MAIN_COLS, 1280,
                                w_gate=shared['w_gate'], name="in_proj")
    y_a, new_shift, new_wkv = _rwkv_mix(z, bsz, t_len, shift_s, wkv_s, p, shared['wwa'], shared['g1'],
                                        shared['g1t'])
    y_b, new_conv, new_c, new_n, new_m = _mlstm_mix(z, gates, bsz, t_len, conv_s, c_s, n_s, m_s, p)
    x1 = _oproj(y_a, y_b, shared['w_out_bf'], xf, g_a, idx)
    q, hn = _norm_mod_matmul(x1, p['norm_ffn_g'], sc_c, sh_c, idx, shared['wq_bf'], D_MODEL, D_MODEL,
                             emit_hn=True, name="peer_query")
    cnt, e1, r2, e2 = _route(q, p['peer_keys'])
    ffn = _peer_dense(hn, shared['u_bf'], shared['vt_bf'], cnt, e1, r2, e2)
    y = _final(x1, ffn, g_c, p['norm_final_g'], scale_f, shift_f, idx)
    return y.reshape(bsz, t_len, D_MODEL), (new_shift, new_wkv, new_conv, new_c, new_n, new_m)


def kernel(x_prompt, x_sample, c_prompt, c_sample, state_rwkv_shift, state_rwkv_wkv, state_mlstm_conv,
           state_mlstm_C, state_mlstm_n, state_mlstm_m, w_ada, b_ada, norm_mix_g, w_in, a_mu, a_w0, a_w2,
           a_a0, a_a2, a_g2, a_kk, a_ka, a_rk, a_lnw, a_lnb, b_conv_w, b_conv_b, b_ig_bias, b_fg_bias,
           b_norm_g, w_out, norm_ffn_g, peer_wq, peer_keys, peer_u, peer_v, w_ada_final, b_ada_final,
           norm_final_g):
    layer = dict(w_ada=w_ada, b_ada=b_ada, norm_mix_g=norm_mix_g, w_in=w_in, a_mu=a_mu, a_w0=a_w0, a_w2=a_w2,
                 a_a0=a_a0, a_a2=a_a2, a_g2=a_g2, a_kk=a_kk, a_ka=a_ka, a_rk=a_rk, a_lnw=a_lnw, a_lnb=a_lnb,
                 b_conv_w=b_conv_w, b_conv_b=b_conv_b, b_ig_bias=b_ig_bias, b_fg_bias=b_fg_bias,
                 b_norm_g=b_norm_g, w_out=w_out, norm_ffn_g=norm_ffn_g, peer_wq=peer_wq, peer_keys=peer_keys,
                 peer_u=peer_u, peer_v=peer_v)
    assert w_ada.shape[0] == 1, "single-layer trunk"
    p = {k: v[0] for k, v in layer.items()}
    p['norm_final_g'] = norm_final_g
    bp, bs = x_prompt.shape[0], x_sample.shape[0]

    c_all = jnp.concatenate([c_prompt, c_sample], axis=0)
    c_all = jnp.pad(c_all, ((0, -c_all.shape[0] % SUBLANES), (0, 0)))
    mod = _ada(c_all, p['w_ada'], p['b_ada'])
    mod_f = _ada(c_all, w_ada_final, b_ada_final)

    head_of = jnp.arange(MIX_A) // A_HEAD
    g1 = (head_of[:, None] == jnp.arange(LANES)[None, :]).astype(bf16)
    zero = jnp.zeros((64, MIX_A), f32)
    wwa = jnp.concatenate([jnp.concatenate([p['a_w2'], zero], axis=1),
                           jnp.concatenate([zero, p['a_a2']], axis=1)], axis=0).astype(bf16)
    w_in_bf = p['w_in'].astype(bf16)
    shared = dict(
        g1=g1, g1t=g1.T, wwa=wwa, w_in_bf=w_in_bf,
        w_gate=jnp.pad(w_in_bf[:, Z_MAIN_COLS:], ((0, 0), (0, LANES - 2 * B_HEADS))),
        w_out_bf=p['w_out'].astype(bf16), wq_bf=p['peer_wq'].astype(bf16))
    shared['u_bf'], shared['vt_bf'] = _peer_weights(p['peer_u'], p['peer_v'])

    dt = x_prompt.dtype
    zero_states = (jnp.zeros((bp, A_SHIFT_COLS), dt), jnp.zeros((bp, A_HEADS, A_HEAD, A_HEAD), dt),
                   jnp.zeros((bp, B_CONV - 1, B_QK_COLS), dt), jnp.zeros((bp, B_HEADS, B_DQK, B_DV), dt),
                   jnp.zeros((bp, B_HEADS, B_DQK), dt), jnp.zeros((bp, B_HEADS), dt))
    y_prompt, p_states = _trunk(x_prompt, mod[:bp], mod_f[:bp], zero_states, p, shared)
    s_states_in = (state_rwkv_shift[0], state_rwkv_wkv[0], state_mlstm_conv[0], state_mlstm_C[0],
                   state_mlstm_n[0], state_mlstm_m[0])
    y_sample, s_states = _trunk(x_sample, mod[bp:bp + bs], mod_f[bp:bp + bs], s_states_in, p, shared)
    return (y_prompt, y_sample) + tuple(s[None] for s in p_states) + tuple(s[None] for s in s_states)
```

```python
import functools

import jax
import jax.numpy as jnp
from jax import lax
from jax.experimental import pallas as pl
from jax.experimental.pallas import tpu as pltpu

f32 = jnp.float32
bf16 = jnp.bfloat16

D_MODEL = 2048
MIX_A = 1024
A_HEAD = 64
A_HEADS = 16
A_SHIFT_COLS = 3 * MIX_A + 64 + 64 + 128
A_GN_EPS = A_HEAD * 1e-5
B_QK_COLS = 1024
B_HEADS = 4
B_DV = 256
B_DQK = 128
B_CONV = 4
B_CHUNK = 64
Z_MAIN_COLS = A_SHIFT_COLS + B_QK_COLS + 2 * 1024
PEER_HEADS = 8
PEER_NKEYS = 128
PEER_TOPK = 16
NORM_EPS = 1e-6

LANES = 128
SUBLANES = 8
VMEM_LIMIT_BYTES = 56 * 1024 * 1024

TM = 512
TM_EW = 256
TM_ROUTE = 512
TE = 1024
TE_HALF = TE // 2


def _params(*sem):
    return pltpu.CompilerParams(dimension_semantics=sem, vmem_limit_bytes=VMEM_LIMIT_BYTES)


def _sds(shape, dtype=f32):
    return jax.ShapeDtypeStruct(shape, dtype)


def _ada_kernel(c_ref, w_ref, b_ref, o_ref):
    c = c_ref[...]
    a = (c * jax.nn.sigmoid(c)).astype(bf16)
    o_ref[...] = jnp.dot(a, w_ref[...].astype(bf16), preferred_element_type=f32) + b_ref[...]


def _ada(c, w, b):
    m, n_out, tn = c.shape[0], w.shape[1], 1024
    return pl.pallas_call(
        _ada_kernel,
        grid=(n_out // tn,),
        in_specs=[pl.BlockSpec((m, D_MODEL), lambda j: (0, 0)),
                  pl.BlockSpec((D_MODEL, tn), lambda j: (0, j)),
                  pl.BlockSpec((1, tn), lambda j: (0, j))],
        out_specs=pl.BlockSpec((m, tn), lambda j: (0, j)),
        out_shape=_sds((m, n_out)),
        compiler_params=_params("arbitrary"),
        name="ada_mod",
    )(c, w, b.reshape(1, n_out))


def _cast_kernel(x_ref, o_ref):
    o_ref[...] = x_ref[...].astype(o_ref.dtype)


def _cast_bf16(w):
    rows, cols = w.shape
    tr = 256
    spec = pl.BlockSpec((tr, cols), lambda i: (i, 0))
    return pl.pallas_call(
        _cast_kernel, grid=(rows // tr,), in_specs=[spec], out_specs=spec, out_shape=_sds(w.shape, bf16),
        compiler_params=_params("arbitrary"), name="cast_bf16",
    )(w)


def _mod3(m, bsz, t_len, tm):
    if t_len % tm == 0:
        per = t_len // tm
        return m.reshape(bsz, 1, D_MODEL), (lambda i: i // per)
    n = bsz * t_len
    assert tm % t_len == 0 and n % tm == 0
    return jnp.repeat(m, t_len, axis=0).reshape(n // tm, tm, D_MODEL), (lambda i: i)


def _mod_spec(arr, idx):
    return pl.BlockSpec((1, arr.shape[1], D_MODEL), lambda i, *_: (idx(i), 0, 0))


def _nmm_kernel(*refs, has_gate, emit_hn):
    it = iter(refs)
    x_ref, g_ref, sc_ref, sh_ref, w_ref = (next(it) for _ in range(5))
    wg_ref = next(it) if has_gate else None
    o_ref = next(it)
    og_ref = next(it) if has_gate else None
    hn_ref = next(it) if emit_hn else None
    hs_ref = next(it)

    @pl.when(pl.program_id(1) == 0)
    def _():
        x = x_ref[...]
        y = x * lax.rsqrt(jnp.mean(x * x, axis=-1, keepdims=True) + NORM_EPS) * g_ref[...]
        h = (y * (1.0 + sc_ref[0]) + sh_ref[0]).astype(bf16)
        hs_ref[...] = h
        if emit_hn:
            hn_ref[...] = h
        if has_gate:
            og_ref[...] = jnp.dot(h, wg_ref[...], preferred_element_type=f32)

    o_ref[...] = jnp.dot(hs_ref[...], w_ref[...], preferred_element_type=f32)


def _norm_mod_matmul(x, g, sc3, sh3, idx, w_bf, n_out, tn, w_gate=None, emit_hn=False, name="nmm"):
    n = x.shape[0]
    has_gate = w_gate is not None
    in_specs = [pl.BlockSpec((TM, D_MODEL), lambda i, j: (i, 0)),
                pl.BlockSpec((1, D_MODEL), lambda i, j: (0, 0)),
                _mod_spec(sc3, idx), _mod_spec(sh3, idx),
                pl.BlockSpec((D_MODEL, tn), lambda i, j: (0, j))]
    args = [x, g.reshape(1, D_MODEL), sc3, sh3, w_bf]
    out_specs = [pl.BlockSpec((TM, tn), lambda i, j: (i, j))]
    out_shape = [_sds((n, n_out))]
    if has_gate:
        in_specs.append(pl.BlockSpec((D_MODEL, LANES), lambda i, j: (0, 0)))
        args.append(w_gate)
        out_specs.append(pl.BlockSpec((TM, LANES), lambda i, j: (i, 0)))
        out_shape.append(_sds((n, LANES)))
    if emit_hn:
        out_specs.append(pl.BlockSpec((TM, D_MODEL), lambda i, j: (i, 0)))
        out_shape.append(_sds((n, D_MODEL), bf16))
    return pl.pallas_call(
        functools.partial(_nmm_kernel, has_gate=has_gate, emit_hn=emit_hn),
        grid=(n // TM, n_out // tn),
        in_specs=in_specs, out_specs=out_specs, out_shape=out_shape,
        scratch_shapes=[pltpu.VMEM((TM, D_MODEL), bf16)],
        compiler_params=_params("arbitrary", "arbitrary"),
        name=name,
    )(*args)


def _split_dot(x, w):
    hi = x.astype(bf16)
    lo = (x - hi.astype(f32)).astype(bf16)
    return (jnp.dot(hi, w, preferred_element_type=f32) + jnp.dot(lo, w, preferred_element_type=f32))


def _head_sum(x, g1, g1t):
    return _split_dot(_split_dot(x, g1), g1t)


def _rwkv_prep_kernel(z_ref, first_ref, mu_ref, w0_ref, wwa_ref, a0_ref, g2_ref, akk_ref, aka_ref, rk_ref,
                      g1_ref, g1t_ref, r_ref, k_ref, v_ref, d_ref, kk_ref, kka_ref, g_ref, bonus_ref,
                      *scratch, t_len, carry_mode):
    z = z_ref[...]
    tm = z.shape[0]
    row = lax.broadcasted_iota(jnp.int32, z.shape, 0)
    rolled = pltpu.roll(z, 1, axis=0)
    if carry_mode:
        carry_ref, = scratch
        first = jnp.where(pl.program_id(1) == 0, first_ref[0], carry_ref[0:1, :])
        prev = jnp.where(row == 0, first, rolled)
        carry_ref[0:1, :] = z[tm - 1:tm, :]
    else:
        prev = jnp.where(lax.rem(row, t_len) == 0, first_ref[...], rolled)
    zs = z + (prev - z) * mu_ref[...]
    r = zs[:, 0:MIX_A]
    k = zs[:, MIX_A:2 * MIX_A]
    v = zs[:, 2 * MIX_A:3 * MIX_A]
    xwa = zs[:, 3 * MIX_A:3 * MIX_A + 128]
    xg = zs[:, 3 * MIX_A + 128:3 * MIX_A + 256]
    lane = lax.broadcasted_iota(jnp.int32, xwa.shape, 1)
    xwa = jnp.where(lane < 64, jnp.tanh(xwa), xwa)
    lwa = jnp.dot(xwa.astype(bf16), wwa_ref[...], preferred_element_type=f32)
    u = w0_ref[...] + lwa[:, 0:MIX_A]
    w_log = -(jnp.maximum(-u, 0.0) + jnp.log(1.0 + jnp.exp(-jnp.abs(u)))) - 0.5
    decay = jnp.exp(-jnp.exp(w_log))
    a = jax.nn.sigmoid(a0_ref[...] + lwa[:, MIX_A:2 * MIX_A])
    g_ref[...] = jnp.dot(jax.nn.sigmoid(xg).astype(bf16), g2_ref[...].astype(bf16), preferred_element_type=f32)
    g1, g1t = g1_ref[...], g1t_ref[...]
    kk = k * akk_ref[...]
    nrm = jnp.sqrt(_head_sum(kk * kk, g1, g1t))
    kk = kk / jnp.maximum(nrm, 1e-12)
    k = k * (1.0 + (a - 1.0) * aka_ref[...])
    bonus_ref[...] = _head_sum(r * k * rk_ref[...], g1, g1t) * v
    for ref, val in ((r_ref, r), (k_ref, k), (v_ref, v), (d_ref, decay), (kk_ref, kk), (kka_ref, -(kk * a))):
        if carry_mode:
            ref[0] = val.T
        else:
            ref[...] = val


def _rwkv_prep(z, shift_buf, bsz, t_len, p, wwa, g1, g1t):
    n = bsz * t_len
    row = lambda a: a.reshape(1, -1)
    consts = [row(p['a_mu']), row(p['a_w0']), wwa, row(p['a_a0']), p['a_g2'], row(p['a_kk']), row(p['a_ka']),
              row(p['a_rk']), g1, g1t]
    carry_mode = t_len % TM_EW == 0
    if carry_mode:
        per = t_len // TM_EW
        grid = (bsz, per)
        tok = lambda w: pl.BlockSpec((TM_EW, w), lambda b, i: (b * per + i, 0))
        first = shift_buf.reshape(bsz, 1, A_SHIFT_COLS)
        first_spec = pl.BlockSpec((1, 1, A_SHIFT_COLS), lambda b, i: (b, 0, 0))
        scratch = [pltpu.VMEM((SUBLANES, A_SHIFT_COLS), f32)]
        sem = ("arbitrary", "arbitrary")
        scan_spec = pl.BlockSpec((1, MIX_A, TM_EW), lambda b, i: (b, 0, i))
        scan_shape = _sds((bsz, MIX_A, t_len))
    else:
        assert TM_EW % t_len == 0 and n % TM_EW == 0
        grid = (n // TM_EW,)
        tok = lambda w: pl.BlockSpec((TM_EW, w), lambda i: (i, 0))
        first = jnp.repeat(shift_buf, t_len, axis=0)
        first_spec = tok(A_SHIFT_COLS)
        scratch = []
        sem = ("arbitrary",)
        scan_spec, scan_shape = tok(MIX_A), _sds((n, MIX_A))
    full = lambda a: pl.BlockSpec(a.shape, lambda *_: (0,) * a.ndim)
    return pl.pallas_call(
        functools.partial(_rwkv_prep_kernel, t_len=t_len, carry_mode=carry_mode),
        grid=grid,
        in_specs=[tok(A_SHIFT_COLS), first_spec] + [full(a) for a in consts],
        out_specs=[scan_spec] * 6 + [tok(MIX_A)] * 2,
        out_shape=[scan_shape] * 6 + [_sds((n, MIX_A))] * 2,
        scratch_shapes=scratch,
        compiler_params=_params(*sem),
        name="rwkv_prep",
    )(z, first, *consts)


SCAN_CHUNK = 64
RELAYOUT_T = 128


def _rwkv_scan_kernel(r_ref, d_ref, k_ref, kk_ref, kka_ref, v_ref, s0_ref, y_ref, so_ref, *, n_rows, t_chunk):
    @pl.when(pl.program_id(1) == 0)
    def _():
        so_ref[...] = s0_ref[...]

    def step(t, gam):
        gam_new = gam * d_ref[t]
        inv = 1.0 / gam_new
        kk = kk_ref[t] * gam
        r = r_ref[t] * gam_new
        kka = kka_ref[t] * inv
        k = k_ref[t] * inv
        for i in range(n_rows):
            s_i = so_ref[i]
            sa = jnp.sum(s_i * kk, axis=0, keepdims=True)
            v_i = v_ref[t, pl.ds(i, 1), :]
            s_n = s_i + sa * kka + v_i * k
            so_ref[i] = s_n
            y_ref[t, pl.ds(i, 1), :] = jnp.sum(s_n * r, axis=0, keepdims=True)
        return gam_new

    gam = lax.fori_loop(0, t_chunk, step, jnp.ones((A_HEAD, LANES), f32))
    for i in range(n_rows):
        so_ref[i] = so_ref[i] * gam


def _rwkv_scan(r, d, k, kk, kka, v, s0, t_chunk):
    t_len, n_rows, ltot = v.shape
    jspec = pl.BlockSpec((t_chunk, A_HEAD, LANES), lambda l, t: (t, 0, l))
    vspec = pl.BlockSpec((t_chunk, n_rows, LANES), lambda l, t: (t, 0, l))
    sspec = pl.BlockSpec((n_rows, A_HEAD, LANES), lambda l, t: (0, 0, l))
    return pl.pallas_call(
        functools.partial(_rwkv_scan_kernel, n_rows=n_rows, t_chunk=t_chunk),
        grid=(ltot // LANES, t_len // t_chunk),
        in_specs=[jspec] * 5 + [vspec, sspec],
        out_specs=[vspec, sspec],
        out_shape=[_sds(v.shape), _sds((n_rows, A_HEAD, ltot))],
        compiler_params=_params("arbitrary", "arbitrary"),
        name="rwkv_scan",
    )(r, d, k, kk, kka, v, s0)


def _to_scan_kernel(x_ref, o_ref, *, bsz, pack, val_mode):
    n_idx = A_HEAD // pack if val_mode else A_HEAD
    for j0 in range(0, n_idx, SUBLANES):
        tiles = []
        for j in range(j0, j0 + SUBLANES):
            if val_mode:
                rows = [x_ref[b, pl.ds(g * n_idx + j, A_HEADS, stride=A_HEAD), :]
                        for g in range(pack) for b in range(bsz)]
            else:
                rows = [x_ref[b, pl.ds(j, A_HEADS, stride=A_HEAD), :] for b in range(bsz)] * pack
            tiles.append(jnp.concatenate(rows, axis=0).T)
        o_ref[:, j0:j0 + SUBLANES, :] = jnp.swapaxes(jnp.stack(tiles, axis=0), 0, 1)


def _to_scan_layout(a, bsz, t_len, pack, val_mode):
    per = t_len // RELAYOUT_T
    n_idx = A_HEAD // pack if val_mode else A_HEAD
    return pl.pallas_call(
        functools.partial(_to_scan_kernel, bsz=bsz, pack=pack, val_mode=val_mode),
        grid=(per,),
        in_specs=[pl.BlockSpec((bsz, MIX_A, RELAYOUT_T), lambda tb: (0, 0, tb))],
        out_specs=pl.BlockSpec((RELAYOUT_T, n_idx, LANES), lambda tb: (tb, 0, 0)),
        out_shape=_sds((t_len, n_idx, LANES)),
        compiler_params=_params("arbitrary"),
        name="to_scan_val" if val_mode else "to_scan_key",
    )(a)


def _from_scan_kernel(y_ref, o_ref, pt_ref, *, bsz, pack):
    n_idx = A_HEAD // pack
    for j0 in range(0, n_idx, SUBLANES):
        tiles = jnp.swapaxes(y_ref[:, j0:j0 + SUBLANES, :], 0, 1)
        for j in range(j0, j0 + SUBLANES):
            tt = tiles[j - j0].T
            for g in range(pack):
                for b in range(bsz):
                    lo = (g * bsz + b) * A_HEADS
                    pt_ref[b, pl.ds(g * n_idx + j, A_HEADS, stride=A_HEAD), :] = tt[lo:lo + A_HEADS, :]
    for b in range(bsz):
        o_ref[b] = pt_ref[b].T


def _from_scan_layout(y, bsz, t_len, pack):
    per = t_len // RELAYOUT_T
    n_idx = A_HEAD // pack
    out = pl.pallas_call(
        functools.partial(_from_scan_kernel, bsz=bsz, pack=pack),
        grid=(per,),
        in_specs=[pl.BlockSpec((RELAYOUT_T, n_idx, LANES), lambda tb: (tb, 0, 0))],
        out_specs=pl.BlockSpec((bsz, RELAYOUT_T, MIX_A), lambda tb: (0, tb, 0)),
        out_shape=_sds((bsz, t_len, MIX_A)),
        scratch_shapes=[pltpu.VMEM((bsz, MIX_A, RELAYOUT_T), f32)],
        compiler_params=_params("arbitrary"),
        name="from_scan",
    )(y)
    return out.reshape(bsz * t_len, MIX_A)


def _rwkv_post_kernel(y_ref, bonus_ref, g_ref, lnw_ref, lnb_ref, g1_ref, g1t_ref, o_ref):
    y = y_ref[...]
    g1, g1t = g1_ref[...], g1t_ref[...]
    mu = _head_sum(y, g1, g1t) * (1.0 / A_HEAD)
    yc = y - mu
    var = _head_sum(yc * yc, g1, g1t) * (1.0 / A_HEAD)
    yn = yc * lax.rsqrt(var + A_GN_EPS) * lnw_ref[...] + lnb_ref[...]
    o_ref[...] = (yn + bonus_ref[...]) * g_ref[...]


def _rwkv_post(y, bonus, g, p, g1, g1t):
    n = y.shape[0]
    tok = pl.BlockSpec((TM_EW, MIX_A), lambda i: (i, 0))
    full = lambda a: pl.BlockSpec(a.shape, lambda i: (0,) * a.ndim)
    consts = [p['a_lnw'].reshape(1, -1), p['a_lnb'].reshape(1, -1), g1, g1t]
    return pl.pallas_call(
        _rwkv_post_kernel,
        grid=(n // TM_EW,),
        in_specs=[tok] * 3 + [full(a) for a in consts],
        out_specs=tok, out_shape=_sds((n, MIX_A)),
        compiler_params=_params("arbitrary"),
        name="rwkv_post",
    )(y, bonus, g, *consts)


def _rwkv_mix(z, bsz, t_len, shift_buf, wkv0, p, wwa, g1, g1t):
    n = bsz * t_len
    r, k, v, d, kk, kka, g, bonus = _rwkv_prep(z, shift_buf, bsz, t_len, p, wwa, g1, g1t)

    bh = bsz * A_HEADS
    pack = LANES // bh if bh < LANES else 1
    n_rows = A_HEAD // pack

    s0 = wkv0.reshape(bsz, A_HEADS, pack, n_rows, A_HEAD).transpose(3, 4, 2, 0, 1).reshape(n_rows, A_HEAD, pack * bh)
    if pack * bh == LANES and t_len % RELAYOUT_T == 0:
        keys = [_to_scan_layout(a, bsz, t_len, pack, False) for a in (r, d, k, kk, kka)]
        y, s_out = _rwkv_scan(*keys, _to_scan_layout(v, bsz, t_len, pack, True), s0, SCAN_CHUNK)
        y = _from_scan_layout(y, bsz, t_len, pack)
    else:
        assert pack == 1 and t_len <= SCAN_CHUNK

        def scan_rows(a):
            return a.reshape(bsz, t_len, A_HEADS, A_HEAD).transpose(1, 3, 0, 2).reshape(t_len, A_HEAD, bh)

        y, s_out = _rwkv_scan(*(scan_rows(a) for a in (r, d, k, kk, kka, v)), s0, t_len)
        y = y.reshape(t_len, A_HEAD, bsz, A_HEADS).transpose(2, 0, 3, 1).reshape(n, MIX_A)
    new_wkv = s_out.reshape(n_rows, A_HEAD, pack, bsz, A_HEADS).transpose(3, 4, 2, 0, 1)
    new_wkv = new_wkv.reshape(bsz, A_HEADS, A_HEAD, A_HEAD)
    y = _rwkv_post(y, bonus, g, p, g1, g1t)
    new_shift = z.reshape(bsz, t_len, Z_MAIN_COLS)[:, -1, :A_SHIFT_COLS]
    return y, new_shift, new_wkv


def _log_sigmoid(x):
    return jnp.minimum(x, 0.0) - jnp.log(1.0 + jnp.exp(-jnp.abs(x)))


def _mlstm_kernel(*refs, chunk, t_blk, bb):
    zq_refs = refs[0:2 * B_HEADS]
    v_refs = refs[2 * B_HEADS:3 * B_HEADS]
    o_refs = refs[3 * B_HEADS:4 * B_HEADS]
    (cw_ref, cb_ref, cbuf_ref, gc_ref, gr_ref, bias_r_ref, bias_c_ref, ng_ref, c0_ref, n0_ref, m0_ref,
     y_ref, c_ref, n_ref, m_ref, tail_ref) = refs[4 * B_HEADS:]
    n_tail = B_CONV - 1

    def pad_rows(x):
        if t_blk == chunk:
            return x
        return jnp.concatenate([x, jnp.zeros((chunk - t_blk, x.shape[1]), x.dtype)], axis=0)

    @pl.when(pl.program_id(1) == 0)
    def _():
        c_ref[...] = c0_ref[...]
        n_ref[...] = n0_ref[...]
        m_ref[...] = m0_ref[...]
        tail_ref[...] = jnp.zeros_like(tail_ref)
        for bi in range(bb):
            tail_ref[(bi + 1) * SUBLANES - n_tail:(bi + 1) * SUBLANES, :] = cbuf_ref[bi]

    def conv_silu(cb, bi):
        cols = slice(cb * B_DQK, (cb + 1) * B_DQK)
        blk = pad_rows(zq_refs[cb][bi])
        zp = jnp.concatenate([tail_ref[bi * SUBLANES:(bi + 1) * SUBLANES, cols], blk], axis=0)
        acc = cb_ref[:, cols]
        for w in range(B_CONV):
            off = SUBLANES - n_tail + w
            acc = acc + zp[off:off + chunk, :] * cw_ref[w:w + 1, cols]
        tail_ref[(bi + 1) * SUBLANES - n_tail:(bi + 1) * SUBLANES, cols] = blk[chunk - n_tail:chunk, :]
        return acc * jax.nn.sigmoid(acc)

    ti = lax.broadcasted_iota(jnp.int32, (chunk, chunk), 0)
    si = lax.broadcasted_iota(jnp.int32, (chunk, chunk), 1)
    tri = si <= ti
    nt = (((1,), (1,)), ((), ()))
    tn = (((0,), (0,)), ((), ()))
    def head_chain(bi, h):
        gcol = gc_ref[bi] + bias_r_ref[...]
        grow = gr_ref[bi, 0] + bias_c_ref[...]
        q = conv_silu(h, bi)
        k = conv_silu(B_HEADS + h, bi) * (B_DQK ** -0.5)
        v = pad_rows(v_refs[h][bi])
        yield
        i_col, f_col = gcol[:, h:h + 1], gcol[:, B_HEADS + h:B_HEADS + h + 1]
        i_row, f_row = grow[h:h + 1, :], grow[B_HEADS + h:B_HEADS + h + 1, :]
        lf_col, lf_row = _log_sigmoid(f_col), _log_sigmoid(f_row)
        b_col = jnp.sum(jnp.where(tri, lf_row, 0.0), axis=1, keepdims=True)
        b_row = jnp.sum(jnp.where(ti <= si, lf_col, 0.0), axis=0, keepdims=True)
        m_prev = m_ref[bi, h:h + 1, :]
        yield
        log_d = jnp.where(tri, b_col - b_row + i_row, -jnp.inf)
        inter = b_col + m_prev
        m_t = jnp.maximum(inter, jnp.max(log_d, axis=1, keepdims=True))
        qb, kb, vb = q.astype(bf16), k.astype(bf16), v.astype(bf16)
        yield
        s = lax.dot_general(qb, kb, nt, preferred_element_type=f32) * jnp.exp(log_d - m_t)
        w_inter = jnp.exp(inter - m_t)
        c_old = c_ref[bi, h]
        n_old = n_ref[bi, h:h + 1, :]
        yield
        num = (jnp.dot(s.astype(bf16), vb, preferred_element_type=f32)
               + w_inter * jnp.dot(qb, c_old.astype(bf16), preferred_element_type=f32))
        den = jnp.sum(s, axis=1, keepdims=True) + w_inter * jnp.sum(q * n_old, axis=1, keepdims=True)
        yield
        hh = num / jnp.maximum(jnp.abs(den), jnp.exp(-m_t))
        m_new = m_t[chunk - 1:chunk, :]
        b_last = b_col[chunk - 1:chunk, :]
        ws = jnp.exp(b_last - b_col + i_col - m_new)
        d_c = jnp.exp(b_last + m_prev - m_new)
        kw = k * ws
        yield
        c_ref[bi, h] = d_c * c_old + lax.dot_general(kw.astype(bf16), vb, tn, preferred_element_type=f32)
        n_ref[bi, h:h + 1, :] = d_c * n_old + jnp.sum(kw, axis=0, keepdims=True)
        m_ref[bi, h:h + 1, :] = m_new
        yield
        hn = hh * lax.rsqrt(jnp.mean(hh * hh, axis=1, keepdims=True) + NORM_EPS)
        y_ref[bi, :, h * B_DV:(h + 1) * B_DV] = ((hn * ng_ref[:, h * B_DV:(h + 1) * B_DV])[0:t_blk, :]
                                                 * jax.nn.sigmoid(o_refs[h][bi]))

    _interleave(*[head_chain(bi, h) for bi in range(bb) for h in range(B_HEADS)])


def _interleave(*gens):
    live = list(gens)
    while live:
        for gen in list(live):
            try:
                next(gen)
            except StopIteration:
                live.remove(gen)


MLSTM_BATCH = 4


def _mlstm(z3, conv_buf, gates_col, gates_row, p, c0, n0, m0, bsz, n_chunks, chunk):
    t_rows = z3.shape[1]
    t_blk = chunk if t_rows % chunk == 0 else t_rows
    assert t_blk == chunk or n_chunks == 1
    bb = MLSTM_BATCH if bsz % MLSTM_BATCH == 0 else 1
    qk_col0 = A_SHIFT_COLS // B_DQK
    v_col0 = (A_SHIFT_COLS + B_QK_COLS) // B_DV
    o_col0 = v_col0 + B_HEADS
    tok = lambda w, rows=t_blk: pl.BlockSpec((bb, rows, w), lambda b, c: (b, c, 0))
    zcol = lambda w, cb: pl.BlockSpec((bb, t_blk, w), lambda b, c: (b, c, cb))
    full = lambda a: pl.BlockSpec(a.shape, lambda b, c: (0,) * a.ndim)
    bias = jnp.concatenate([p['b_ig_bias'], p['b_fg_bias']])
    bias_r = jnp.pad(bias, (0, LANES - 2 * B_HEADS)).reshape(1, LANES)
    bias_c = bias.reshape(2 * B_HEADS, 1)
    ng = p['b_norm_g'].reshape(1, -1)
    cw, cb = p['b_conv_w'], p['b_conv_b'].reshape(1, -1)
    cspec = pl.BlockSpec((bb, B_HEADS, B_DQK, B_DV), lambda b, c: (b, 0, 0, 0))
    nspec = pl.BlockSpec((bb, B_HEADS, B_DQK), lambda b, c: (b, 0, 0))
    mspec = pl.BlockSpec((bb, B_HEADS, 1), lambda b, c: (b, 0, 0))
    return pl.pallas_call(
        functools.partial(_mlstm_kernel, chunk=chunk, t_blk=t_blk, bb=bb),
        grid=(bsz // bb, n_chunks),
        in_specs=[zcol(B_DQK, qk_col0 + j) for j in range(2 * B_HEADS)]
                 + [zcol(B_DV, v_col0 + h) for h in range(B_HEADS)]
                 + [zcol(B_DV, o_col0 + h) for h in range(B_HEADS)]
                 + [full(cw), full(cb), pl.BlockSpec((bb, B_CONV - 1, B_QK_COLS), lambda b, c: (b, 0, 0)),
                    tok(LANES, chunk), pl.BlockSpec((bb, 1, 2 * B_HEADS, chunk), lambda b, c: (b, c, 0, 0)),
                    full(bias_r), full(bias_c), full(ng), cspec, nspec, mspec],
        out_specs=[tok(B_HEADS * B_DV), cspec, nspec, mspec],
        out_shape=[_sds((bsz, t_rows, B_HEADS * B_DV)), _sds(c0.shape), _sds(n0.shape), _sds(m0.shape)],
        scratch_shapes=[pltpu.VMEM((bb * SUBLANES, B_QK_COLS), f32)],
        compiler_params=_params("arbitrary", "arbitrary"),
        name="mlstm",
    )(*([z3] * (4 * B_HEADS)), cw, cb, conv_buf, gates_col, gates_row, bias_r, bias_c, ng, c0, n0, m0)


def _mlstm_mix(z, gates, bsz, t_len, conv_buf, c0, n0, m0, p):
    n = bsz * t_len
    z3 = z.reshape(bsz, t_len, Z_MAIN_COLS)
    zqk = z3[:, :, A_SHIFT_COLS:A_SHIFT_COLS + B_QK_COLS]
    new_conv = jnp.concatenate([conv_buf, zqk[:, max(t_len - (B_CONV - 1), 0):]], axis=1)[:, -(B_CONV - 1):]
    chunk = B_CHUNK if t_len % B_CHUNK == 0 else SUBLANES
    t_pad = -(-t_len // chunk) * chunk
    g3 = gates.reshape(bsz, t_len, LANES)
    if t_pad != t_len:
        lane = jnp.arange(LANES)
        fill = jnp.where(lane < B_HEADS, -1e30, jnp.where(lane < 2 * B_HEADS, 1e30, 0.0)).astype(f32)
        g3 = jnp.concatenate([g3, jnp.broadcast_to(fill, (bsz, t_pad - t_len, LANES))], axis=1)
    n_chunks = t_pad // chunk
    gates_col = g3
    gates_row = g3[:, :, :2 * B_HEADS].reshape(bsz, n_chunks, chunk, 2 * B_HEADS).transpose(0, 1, 3, 2)
    y, c_new, n_new, m_new = _mlstm(z3, conv_buf, gates_col, gates_row, p, c0, n0, m0.reshape(bsz, B_HEADS, 1),
                                    bsz, n_chunks, chunk)
    y = y.reshape(n, -1)
    return y, new_conv, c_new, n_new, m_new.reshape(bsz, B_HEADS)


def _oproj_kernel(ya_ref, yb_ref, wt_ref, wb_ref, x_ref, ga_ref, o_ref):
    acc = jnp.dot(ya_ref[...].astype(bf16), wt_ref[...], preferred_element_type=f32)
    acc = acc + jnp.dot(yb_ref[...].astype(bf16), wb_ref[...], preferred_element_type=f32)
    o_ref[...] = x_ref[...] + ga_ref[0] * acc


def _oproj(ya, yb, w_out_bf, x, ga3, idx):
    n = x.shape[0]
    half = D_MODEL // 2
    tok = lambda w: pl.BlockSpec((TM, w), lambda i: (i, 0))
    return pl.pallas_call(
        _oproj_kernel,
        grid=(n // TM,),
        in_specs=[tok(half), tok(half),
                  pl.BlockSpec((half, D_MODEL), lambda i: (0, 0)), pl.BlockSpec((half, D_MODEL), lambda i: (1, 0)),
                  tok(D_MODEL), _mod_spec(ga3, idx)],
        out_specs=tok(D_MODEL),
        out_shape=_sds((n, D_MODEL)),
        compiler_params=_params("arbitrary"),
        name="out_proj",
    )(ya, yb, w_out_bf, w_out_bf, x, ga3)


def _route_kernel(q_ref, keys_ref, cnt_ref, e1_ref, r2_ref, e2_ref, v1_scr, v2_scr):
    q = q_ref[...]
    tm = q.shape[0]
    neg_inf = -jnp.inf
    iota_k = lax.broadcasted_iota(jnp.int32, (PEER_NKEYS, tm), 0).astype(f32)
    nt = (((1,), (1,)), ((), ()))
    ranks, es = [], []
    for half, vscr in enumerate((v1_scr, v2_scr)):
        qp = q[:, half * 128:(half + 1) * 128].astype(bf16)
        s = lax.dot_general(keys_ref[half, 0].astype(bf16), qp, nt, preferred_element_type=f32)
        rank = jnp.full((PEER_NKEYS, tm), float(PEER_TOPK), f32)
        work = s
        for a in range(PEER_TOPK):
            m = jnp.max(work, axis=0, keepdims=True)
            idx = jnp.min(jnp.where(work == m, iota_k, float(PEER_NKEYS)), axis=0, keepdims=True)
            sel = iota_k == idx
            rank = jnp.where(sel, float(a), rank)
            work = jnp.where(sel, neg_inf, work)
            vscr[a:a + 1, :] = m
        ranks.append(rank)
        es.append(jnp.exp(s - vscr[0:1, :]))
    v1, v2 = v1_scr[...], v2_scr[...]
    row8 = lax.broadcasted_iota(jnp.int32, (SUBLANES, tm), 0).astype(f32)
    row16 = lax.broadcasted_iota(jnp.int32, (PEER_TOPK, tm), 0).astype(f32)
    pieces, poss = [v1[0:1, :] + v2], [row16]
    for a in range(1, 8):
        nb = PEER_TOPK // (a + 1)
        pieces.append(jnp.where(row8 < float(nb), v1[a:a + 1, :] + v2[0:SUBLANES, :], neg_inf))
        poss.append(row8 + float(a * PEER_TOPK))
    pieces.append(v1[8:16, :] + v2[0:1, :])
    poss.append((row8 + 8.0) * float(PEER_TOPK))
    cand = jnp.concatenate(pieces, axis=0)
    pos = jnp.concatenate(poss, axis=0)
    cnt16 = jnp.zeros((PEER_TOPK, tm), f32)
    zsum = None
    best0 = None
    for j in range(PEER_TOPK):
        m = jnp.max(cand, axis=0, keepdims=True)
        pj = jnp.min(jnp.where(cand == m, pos, 1e9), axis=0, keepdims=True)
        cand = jnp.where(pos == pj, neg_inf, cand)
        if j == 0:
            best0 = m
            zsum = jnp.ones_like(m)
        else:
            zsum = zsum + jnp.exp(m - best0)
        a_j = jnp.floor(pj * (1.0 / PEER_TOPK))
        cnt16 = cnt16 + jnp.where(row16 == a_j, 1.0, 0.0)
    cnt_i = jnp.zeros((PEER_NKEYS, tm), f32)
    for a in range(PEER_TOPK):
        cnt_i = jnp.where(ranks[0] == float(a), cnt16[a:a + 1, :], cnt_i)
    cnt_ref[0] = cnt_i
    e1_ref[0] = es[0] / zsum
    r2_ref[0] = ranks[1].astype(bf16)
    e2_ref[0] = es[1].astype(bf16)


def _route(q, keys):
    n = q.shape[0]
    ospec = pl.BlockSpec((1, PEER_NKEYS, TM_ROUTE), lambda i, h: (h, 0, i))
    oshape = (PEER_HEADS, PEER_NKEYS, n)
    return pl.pallas_call(
        _route_kernel,
        grid=(n // TM_ROUTE, PEER_HEADS),
        in_specs=[pl.BlockSpec((TM_ROUTE, 256), lambda i, h: (i, h)),
                  pl.BlockSpec((2, 1, PEER_NKEYS, 128), lambda i, h: (0, h, 0, 0))],
        out_specs=[ospec] * 4,
        out_shape=[_sds(oshape), _sds(oshape), _sds(oshape, bf16), _sds(oshape, bf16)],
        scratch_shapes=[pltpu.VMEM((PEER_TOPK, TM_ROUTE), f32), pltpu.VMEM((PEER_TOPK, TM_ROUTE), f32)],
        compiler_params=_params("arbitrary", "arbitrary"),
        name="peer_route",
    )(q, keys)


def _gated_act(h, cnt_ref, e1_ref, r2_ref, e2_ref, row0):
    zero = jnp.zeros((), bf16)
    parts = []
    for q in range(TE_HALF // PEER_NKEYS):
        row = row0 + q
        hs = h[q * PEER_NKEYS:(q + 1) * PEER_NKEYS, :]
        gate = None
        for hd in range(PEER_HEADS):
            cnt_row = cnt_ref[hd, row:row + 1, :].astype(bf16)
            e1_row = e1_ref[hd, row:row + 1, :].astype(bf16)
            term = jnp.where(r2_ref[hd] < cnt_row, e2_ref[hd] * e1_row, zero)
            gate = term if gate is None else gate + term
        gelu = 0.5 * hs * (1.0 + lax.erf(hs * (2.0 ** -0.5)))
        parts.append(gelu.astype(bf16) * gate)
    return jnp.concatenate(parts, axis=0)


def _dense_kernel(hn_ref, u_ref, vt_ref, vtp_ref, cnt_ref, e1_ref, cntp_ref, e1p_ref, r2_ref, e2_ref, o_ref,
                  xt_scr, acc_scr, hb_scr, *, n_chunks):
    c = pl.program_id(1)

    @pl.when(c == 0)
    def _():
        xt_scr[...] = hn_ref[...].T
        acc_scr[...] = jnp.zeros_like(acc_scr)
        hb_scr[...] = jnp.zeros_like(hb_scr)

    sub = TE_HALF // PEER_NKEYS
    xt = xt_scr[...]
    h_a = jnp.dot(u_ref[0:TE_HALF, :], xt, preferred_element_type=f32)
    act_bp = _gated_act(hb_scr[...], cntp_ref, e1p_ref, r2_ref, e2_ref, sub)
    part = jnp.dot(vtp_ref[...], act_bp, preferred_element_type=f32)
    hb_scr[...] = jnp.dot(u_ref[TE_HALF:TE, :], xt, preferred_element_type=f32)
    act_a = _gated_act(h_a, cnt_ref, e1_ref, r2_ref, e2_ref, 0)
    part = jnp.dot(vt_ref[:, 0:TE_HALF], act_a, preferred_element_type=f32) + part
    acc_scr[...] += part

    @pl.when(c == n_chunks - 1)
    def _():
        act_b = _gated_act(hb_scr[...], cnt_ref, e1_ref, r2_ref, e2_ref, sub)
        total = acc_scr[...] + jnp.dot(vt_ref[:, TE_HALF:TE], act_b, preferred_element_type=f32)
        o_ref[...] = total.T


def _peer_dense(hn, u_bf, vt_bf, cnt, e1, r2, e2):
    n = hn.shape[0]
    n_exp = u_bf.shape[0]
    n_chunks = n_exp // TE
    sub = TE // PEER_NKEYS
    prev = lambda c: jnp.maximum(c - 1, 0)
    return pl.pallas_call(
        functools.partial(_dense_kernel, n_chunks=n_chunks),
        grid=(n // TM, n_chunks),
        in_specs=[pl.BlockSpec((TM, D_MODEL), lambda i, c: (i, 0)),
                  pl.BlockSpec((TE, D_MODEL), lambda i, c: (c, 0)),
                  pl.BlockSpec((D_MODEL, TE), lambda i, c: (0, c)),
                  pl.BlockSpec((D_MODEL, TE_HALF), lambda i, c: (0, jnp.maximum(2 * c - 1, 0))),
                  pl.BlockSpec((PEER_HEADS, sub, TM), lambda i, c: (0, c, i)),
                  pl.BlockSpec((PEER_HEADS, sub, TM), lambda i, c: (0, c, i)),
                  pl.BlockSpec((PEER_HEADS, sub, TM), lambda i, c: (0, prev(c), i)),
                  pl.BlockSpec((PEER_HEADS, sub, TM), lambda i, c: (0, prev(c), i)),
                  pl.BlockSpec((PEER_HEADS, PEER_NKEYS, TM), lambda i, c: (0, 0, i)),
                  pl.BlockSpec((PEER_HEADS, PEER_NKEYS, TM), lambda i, c: (0, 0, i))],
        out_specs=pl.BlockSpec((TM, D_MODEL), lambda i, c: (i, 0)),
        out_shape=_sds((n, D_MODEL)),
        scratch_shapes=[pltpu.VMEM((D_MODEL, TM), bf16), pltpu.VMEM((D_MODEL, TM), f32),
                        pltpu.VMEM((TE_HALF, TM), f32)],
        compiler_params=_params("arbitrary", "arbitrary"),
        name="peer_dense",
    )(hn, u_bf, vt_bf, vt_bf, cnt, e1, cnt, e1, r2, e2)


def _peer_weights_kernel(u_ref, v_ref, ub_ref, vt_ref):
    ub_ref[...] = u_ref[...].astype(bf16)
    vt_ref[...] = v_ref[...].astype(bf16).T


def _peer_weights(u, v):
    n_exp, te = u.shape[0], 512
    rows = pl.BlockSpec((te, D_MODEL), lambda i: (i, 0))
    return pl.pallas_call(
        _peer_weights_kernel,
        grid=(n_exp // te,),
        in_specs=[rows, rows],
        out_specs=[rows, pl.BlockSpec((D_MODEL, te), lambda i: (0, i))],
        out_shape=[_sds((n_exp, D_MODEL), bf16), _sds((D_MODEL, n_exp), bf16)],
        compiler_params=_params("arbitrary"),
        name="peer_weights",
    )(u, v)


def _final_kernel(x_ref, f_ref, gc_ref, g_ref, sc_ref, sh_ref, o_ref):
    x = x_ref[...] + gc_ref[0] * f_ref[...]
    y = x * lax.rsqrt(jnp.mean(x * x, axis=-1, keepdims=True) + NORM_EPS) * g_ref[...]
    o_ref[...] = y * (1.0 + sc_ref[0]) + sh_ref[0]


def _final(x, ffn, gc3, g, sc3, sh3, idx):
    n = x.shape[0]
    tok = pl.BlockSpec((TM, D_MODEL), lambda i: (i, 0))
    return pl.pallas_call(
        _final_kernel,
        grid=(n // TM,),
        in_specs=[tok, tok, _mod_spec(gc3, idx), pl.BlockSpec((1, D_MODEL), lambda i: (0, 0)),
                  _mod_spec(sc3, idx), _mod_spec(sh3, idx)],
        out_specs=tok, out_shape=_sds((n, D_MODEL)),
        compiler_params=_params("arbitrary"),
        name="final_norm",
    )(x, ffn, gc3, g.reshape(1, D_MODEL), sc3, sh3)


def _trunk(x, mod, mod_f, states, p, shared):
    bsz, t_len, _ = x.shape
    n = bsz * t_len
    shift_s, wkv_s, conv_s, c_s, n_s, m_s = states
    xf = x.reshape(n, D_MODEL)
    m3 = [_mod3(m, bsz, t_len, TM) for m in jnp.split(mod, 6, axis=-1)]
    idx = m3[0][1]
    sh_a, sc_a, g_a, sh_c, sc_c, g_c = (a for a, _ in m3)
    (shift_f, _), (scale_f, _) = (_mod3(m, bsz, t_len, TM) for m in jnp.split(mod_f, 2, axis=-1))

    z, gates = _norm_mod_matmul(xf, p['norm_mix_g'], sc_a, sh_a, idx, shared['w_in_bf'], Z_MAIN_COLS, 1280,
                                w_gate=shared['w_gate'], name="in_proj")
    y_a, new_shift, new_wkv = _rwkv_mix(z, bsz, t_len, shift_s, wkv_s, p, shared['wwa'], shared['g1'],
                                        shared['g1t'])
    y_b, new_conv, new_c, new_n, new_m = _mlstm_mix(z, gates, bsz, t_len, conv_s, c_s, n_s, m_s, p)
    x1 = _oproj(y_a, y_b, shared['w_out_bf'], xf, g_a, idx)
    q, hn = _norm_mod_matmul(x1, p['norm_ffn_g'], sc_c, sh_c, idx, shared['wq_bf'], D_MODEL, D_MODEL,
                             emit_hn=True, name="peer_query")
    cnt, e1, r2, e2 = _route(q, p['peer_keys'])
    ffn = _peer_dense(hn, shared['u_bf'], shared['vt_bf'], cnt, e1, r2, e2)
    y = _final(x1, ffn, g_c, p['norm_final_g'], scale_f, shift_f, idx)
    return y.reshape(bsz, t_len, D_MODEL), (new_shift, new_wkv, new_conv, new_c, new_n, new_m)


def kernel(x_prompt, x_sample, c_prompt, c_sample, state_rwkv_shift, state_rwkv_wkv, state_mlstm_conv,
           state_mlstm_C, state_mlstm_n, state_mlstm_m, w_ada, b_ada, norm_mix_g, w_in, a_mu, a_w0, a_w2,
           a_a0, a_a2, a_g2, a_kk, a_ka, a_rk, a_lnw, a_lnb, b_conv_w, b_conv_b, b_ig_bias, b_fg_bias,
           b_norm_g, w_out, norm_ffn_g, peer_wq, peer_keys, peer_u, peer_v, w_ada_final, b_ada_final,
           norm_final_g):
    layer = dict(w_ada=w_ada, b_ada=b_ada, norm_mix_g=norm_mix_g, w_in=w_in, a_mu=a_mu, a_w0=a_w0, a_w2=a_w2,
                 a_a0=a_a0, a_a2=a_a2, a_g2=a_g2, a_kk=a_kk, a_ka=a_ka, a_rk=a_rk, a_lnw=a_lnw, a_lnb=a_lnb,
                 b_conv_w=b_conv_w, b_conv_b=b_conv_b, b_ig_bias=b_ig_bias, b_fg_bias=b_fg_bias,
                 b_norm_g=b_norm_g, w_out=w_out, norm_ffn_g=norm_ffn_g, peer_wq=peer_wq, peer_keys=peer_keys,
                 peer_u=peer_u, peer_v=peer_v)
    assert w_ada.shape[0] == 1, "single-layer trunk"
    p = {k: v[0] for k, v in layer.items()}
    p['norm_final_g'] = norm_final_g
    bp, bs = x_prompt.shape[0], x_sample.shape[0]

    c_all = jnp.concatenate([c_prompt, c_sample], axis=0)
    c_all = jnp.pad(c_all, ((0, -c_all.shape[0] % SUBLANES), (0, 0)))
    mod = _ada(c_all, p['w_ada'], p['b_ada'])
    mod_f = _ada(c_all, w_ada_final, b_ada_final)

    head_of = jnp.arange(MIX_A) // A_HEAD
    g1 = (head_of[:, None] == jnp.arange(LANES)[None, :]).astype(bf16)
    zero = jnp.zeros((64, MIX_A), f32)
    wwa = jnp.concatenate([jnp.concatenate([p['a_w2'], zero], axis=1),
                           jnp.concatenate([zero, p['a_a2']], axis=1)], axis=0).astype(bf16)
    w_in_bf = _cast_bf16(p['w_in'])
    shared = dict(
        g1=g1, g1t=g1.T, wwa=wwa, w_in_bf=w_in_bf,
        w_gate=jnp.pad(w_in_bf[:, Z_MAIN_COLS:], ((0, 0), (0, LANES - 2 * B_HEADS))),
        w_out_bf=p['w_out'].astype(bf16), wq_bf=p['peer_wq'].astype(bf16))
    shared['u_bf'], shared['vt_bf'] = _peer_weights(p['peer_u'], p['peer_v'])

    dt = x_prompt.dtype
    zero_states = (jnp.zeros((bp, A_SHIFT_COLS), dt), jnp.zeros((bp, A_HEADS, A_HEAD, A_HEAD), dt),
                   jnp.zeros((bp, B_CONV - 1, B_QK_COLS), dt), jnp.zeros((bp, B_HEADS, B_DQK, B_DV), dt),
                   jnp.zeros((bp, B_HEADS, B_DQK), dt), jnp.zeros((bp, B_HEADS), dt))
    y_prompt, p_states = _trunk(x_prompt, mod[:bp], mod_f[:bp], zero_states, p, shared)
    s_states_in = (state_rwkv_shift[0], state_rwkv_wkv[0], state_mlstm_conv[0], state_mlstm_C[0],
                   state_mlstm_n[0], state_mlstm_m[0])
    y_sample, s_states = _trunk(x_sample, mod[bp:bp + bs], mod_f[bp:bp + bs], s_states_in, p, shared)
    return (y_prompt, y_sample) + tuple(s[None] for s in p_states) + tuple(s[None] for s in s_states)
```

```python
import functools

import jax
import jax.numpy as jnp
from jax import lax
from jax.experimental import pallas as pl
from jax.experimental.pallas import tpu as pltpu

f32 = jnp.float32
bf16 = jnp.bfloat16

D_MODEL = 2048
MIX_A = 1024
A_HEAD = 64
A_HEADS = 16
A_SHIFT_COLS = 3 * MIX_A + 64 + 64 + 128
A_GN_EPS = A_HEAD * 1e-5
B_QK_COLS = 1024
B_HEADS = 4
B_DV = 256
B_DQK = 128
B_CONV = 4
B_CHUNK = 64
Z_MAIN_COLS = A_SHIFT_COLS + B_QK_COLS + 2 * 1024
PEER_HEADS = 8
PEER_NKEYS = 128
PEER_TOPK = 16
NORM_EPS = 1e-6

LANES = 128
SUBLANES = 8
VMEM_LIMIT_BYTES = 56 * 1024 * 1024

TM = 512
TM_EW = 256
TM_ROUTE = 512
TE = 1024
TE_HALF = TE // 2


def _params(*sem):
    return pltpu.CompilerParams(dimension_semantics=sem, vmem_limit_bytes=VMEM_LIMIT_BYTES)


def _sds(shape, dtype=f32):
    return jax.ShapeDtypeStruct(shape, dtype)


def _ada_kernel(c_ref, w_ref, b_ref, o_ref):
    c = c_ref[...]
    a = (c * jax.nn.sigmoid(c)).astype(bf16)
    o_ref[...] = jnp.dot(a, w_ref[...].astype(bf16), preferred_element_type=f32) + b_ref[...]


def _ada(c, w, b):
    m, n_out, tn = c.shape[0], w.shape[1], 1024
    return pl.pallas_call(
        _ada_kernel,
        grid=(n_out // tn,),
        in_specs=[pl.BlockSpec((m, D_MODEL), lambda j: (0, 0)),
                  pl.BlockSpec((D_MODEL, tn), lambda j: (0, j)),
                  pl.BlockSpec((1, tn), lambda j: (0, j))],
        out_specs=pl.BlockSpec((m, tn), lambda j: (0, j)),
        out_shape=_sds((m, n_out)),
        compiler_params=_params("arbitrary"),
        name="ada_mod",
    )(c, w, b.reshape(1, n_out))


def _mod3(m, bsz, t_len, tm):
    if t_len % tm == 0:
        per = t_len // tm
        return m.reshape(bsz, 1, D_MODEL), (lambda i: i // per)
    n = bsz * t_len
    assert tm % t_len == 0 and n % tm == 0
    return jnp.repeat(m, t_len, axis=0).reshape(n // tm, tm, D_MODEL), (lambda i: i)


def _mod_spec(arr, idx):
    return pl.BlockSpec((1, arr.shape[1], D_MODEL), lambda i, *_: (idx(i), 0, 0))


def _nmm_kernel(*refs, has_gate, emit_hn):
    it = iter(refs)
    x_ref, g_ref, sc_ref, sh_ref, w_ref = (next(it) for _ in range(5))
    wg_ref = next(it) if has_gate else None
    o_ref = next(it)
    og_ref = next(it) if has_gate else None
    hn_ref = next(it) if emit_hn else None
    hs_ref = next(it)

    @pl.when(pl.program_id(1) == 0)
    def _():
        x = x_ref[...]
        y = x * lax.rsqrt(jnp.mean(x * x, axis=-1, keepdims=True) + NORM_EPS) * g_ref[...]
        h = (y * (1.0 + sc_ref[0]) + sh_ref[0]).astype(bf16)
        hs_ref[...] = h
        if emit_hn:
            hn_ref[...] = h
        if has_gate:
            og_ref[...] = jnp.dot(h, wg_ref[...], preferred_element_type=f32)

    o_ref[...] = jnp.dot(hs_ref[...], w_ref[...], preferred_element_type=f32)


def _norm_mod_matmul(x, g, sc3, sh3, idx, w_bf, n_out, tn, w_gate=None, emit_hn=False, name="nmm"):
    n = x.shape[0]
    has_gate = w_gate is not None
    in_specs = [pl.BlockSpec((TM, D_MODEL), lambda i, j: (i, 0)),
                pl.BlockSpec((1, D_MODEL), lambda i, j: (0, 0)),
                _mod_spec(sc3, idx), _mod_spec(sh3, idx),
                pl.BlockSpec((D_MODEL, tn), lambda i, j: (0, j))]
    args = [x, g.reshape(1, D_MODEL), sc3, sh3, w_bf]
    out_specs = [pl.BlockSpec((TM, tn), lambda i, j: (i, j))]
    out_shape = [_sds((n, n_out))]
    if has_gate:
        in_specs.append(pl.BlockSpec((D_MODEL, LANES), lambda i, j: (0, 0)))
        args.append(w_gate)
        out_specs.append(pl.BlockSpec((TM, LANES), lambda i, j: (i, 0)))
        out_shape.append(_sds((n, LANES)))
    if emit_hn:
        out_specs.append(pl.BlockSpec((TM, D_MODEL), lambda i, j: (i, 0)))
        out_shape.append(_sds((n, D_MODEL), bf16))
    return pl.pallas_call(
        functools.partial(_nmm_kernel, has_gate=has_gate, emit_hn=emit_hn),
        grid=(n // TM, n_out // tn),
        in_specs=in_specs, out_specs=out_specs, out_shape=out_shape,
        scratch_shapes=[pltpu.VMEM((TM, D_MODEL), bf16)],
        compiler_params=_params("arbitrary", "arbitrary"),
        name=name,
    )(*args)


def _split_dot(x, w):
    hi = x.astype(bf16)
    lo = (x - hi.astype(f32)).astype(bf16)
    return (jnp.dot(hi, w, preferred_element_type=f32) + jnp.dot(lo, w, preferred_element_type=f32))


def _head_sum(x, g1, g1t):
    return _split_dot(_split_dot(x, g1), g1t)


def _rwkv_prep_kernel(z_ref, first_ref, mu_ref, w0_ref, wwa_ref, a0_ref, g2_ref, akk_ref, aka_ref, rk_ref,
                      g1_ref, g1t_ref, r_ref, k_ref, v_ref, d_ref, kk_ref, kka_ref, g_ref, bonus_ref,
                      *scratch, t_len, carry_mode):
    z = z_ref[...]
    tm = z.shape[0]
    row = lax.broadcasted_iota(jnp.int32, z.shape, 0)
    rolled = pltpu.roll(z, 1, axis=0)
    if carry_mode:
        carry_ref, = scratch
        first = jnp.where(pl.program_id(1) == 0, first_ref[0], carry_ref[0:1, :])
        prev = jnp.where(row == 0, first, rolled)
        carry_ref[0:1, :] = z[tm - 1:tm, :]
    else:
        prev = jnp.where(lax.rem(row, t_len) == 0, first_ref[...], rolled)
    zs = z + (prev - z) * mu_ref[...]
    r = zs[:, 0:MIX_A]
    k = zs[:, MIX_A:2 * MIX_A]
    v = zs[:, 2 * MIX_A:3 * MIX_A]
    xwa = zs[:, 3 * MIX_A:3 * MIX_A + 128]
    xg = zs[:, 3 * MIX_A + 128:3 * MIX_A + 256]
    lane = lax.broadcasted_iota(jnp.int32, xwa.shape, 1)
    xwa = jnp.where(lane < 64, jnp.tanh(xwa), xwa)
    lwa = jnp.dot(xwa.astype(bf16), wwa_ref[...], preferred_element_type=f32)
    u = w0_ref[...] + lwa[:, 0:MIX_A]
    w_log = -(jnp.maximum(-u, 0.0) + jnp.log(1.0 + jnp.exp(-jnp.abs(u)))) - 0.5
    decay = jnp.exp(-jnp.exp(w_log))
    a = jax.nn.sigmoid(a0_ref[...] + lwa[:, MIX_A:2 * MIX_A])
    g_ref[...] = jnp.dot(jax.nn.sigmoid(xg).astype(bf16), g2_ref[...].astype(bf16), preferred_element_type=f32)
    g1, g1t = g1_ref[...], g1t_ref[...]
    kk = k * akk_ref[...]
    nrm = jnp.sqrt(_head_sum(kk * kk, g1, g1t))
    kk = kk / jnp.maximum(nrm, 1e-12)
    k = k * (1.0 + (a - 1.0) * aka_ref[...])
    bonus_ref[...] = _head_sum(r * k * rk_ref[...], g1, g1t) * v
    for ref, val in ((r_ref, r), (k_ref, k), (v_ref, v), (d_ref, decay), (kk_ref, kk), (kka_ref, -(kk * a))):
        if carry_mode:
            ref[0] = val.T
        else:
            ref[...] = val


def _rwkv_prep(z, shift_buf, bsz, t_len, p, wwa, g1, g1t):
    n = bsz * t_len
    row = lambda a: a.reshape(1, -1)
    consts = [row(p['a_mu']), row(p['a_w0']), wwa, row(p['a_a0']), p['a_g2'], row(p['a_kk']), row(p['a_ka']),
              row(p['a_rk']), g1, g1t]
    carry_mode = t_len % TM_EW == 0
    if carry_mode:
        per = t_len // TM_EW
        grid = (bsz, per)
        tok = lambda w: pl.BlockSpec((TM_EW, w), lambda b, i: (b * per + i, 0))
        first = shift_buf.reshape(bsz, 1, A_SHIFT_COLS)
        first_spec = pl.BlockSpec((1, 1, A_SHIFT_COLS), lambda b, i: (b, 0, 0))
        scratch = [pltpu.VMEM((SUBLANES, A_SHIFT_COLS), f32)]
        sem = ("arbitrary", "arbitrary")
        scan_spec = pl.BlockSpec((1, MIX_A, TM_EW), lambda b, i: (b, 0, i))
        scan_shape = _sds((bsz, MIX_A, t_len))
    else:
        assert TM_EW % t_len == 0 and n % TM_EW == 0
        grid = (n // TM_EW,)
        tok = lambda w: pl.BlockSpec((TM_EW, w), lambda i: (i, 0))
        first = jnp.repeat(shift_buf, t_len, axis=0)
        first_spec = tok(A_SHIFT_COLS)
        scratch = []
        sem = ("arbitrary",)
        scan_spec, scan_shape = tok(MIX_A), _sds((n, MIX_A))
    full = lambda a: pl.BlockSpec(a.shape, lambda *_: (0,) * a.ndim)
    return pl.pallas_call(
        functools.partial(_rwkv_prep_kernel, t_len=t_len, carry_mode=carry_mode),
        grid=grid,
        in_specs=[tok(A_SHIFT_COLS), first_spec] + [full(a) for a in consts],
        out_specs=[scan_spec] * 6 + [tok(MIX_A)] * 2,
        out_shape=[scan_shape] * 6 + [_sds((n, MIX_A))] * 2,
        scratch_shapes=scratch,
        compiler_params=_params(*sem),
        name="rwkv_prep",
    )(z, first, *consts)


SCAN_CHUNK = 64
RELAYOUT_T = 128


def _rwkv_scan_kernel(r_ref, d_ref, k_ref, kk_ref, kka_ref, v_ref, s0_ref, y_ref, so_ref, *, n_rows, t_chunk):
    @pl.when(pl.program_id(1) == 0)
    def _():
        so_ref[...] = s0_ref[...]

    def step(t, gam):
        gam_new = gam * d_ref[t]
        inv = 1.0 / gam_new
        kk = kk_ref[t] * gam
        r = r_ref[t] * gam_new
        kka = kka_ref[t] * inv
        k = k_ref[t] * inv
        for i in range(n_rows):
            s_i = so_ref[i]
            sa = jnp.sum(s_i * kk, axis=0, keepdims=True)
            v_i = v_ref[t, pl.ds(i, 1), :]
            s_n = s_i + sa * kka + v_i * k
            so_ref[i] = s_n
            y_ref[t, pl.ds(i, 1), :] = jnp.sum(s_n * r, axis=0, keepdims=True)
        return gam_new

    gam = lax.fori_loop(0, t_chunk, step, jnp.ones((A_HEAD, LANES), f32))
    for i in range(n_rows):
        so_ref[i] = so_ref[i] * gam


def _rwkv_scan(r, d, k, kk, kka, v, s0, t_chunk):
    t_len, n_rows, ltot = v.shape
    jspec = pl.BlockSpec((t_chunk, A_HEAD, LANES), lambda l, t: (t, 0, l))
    vspec = pl.BlockSpec((t_chunk, n_rows, LANES), lambda l, t: (t, 0, l))
    sspec = pl.BlockSpec((n_rows, A_HEAD, LANES), lambda l, t: (0, 0, l))
    return pl.pallas_call(
        functools.partial(_rwkv_scan_kernel, n_rows=n_rows, t_chunk=t_chunk),
        grid=(ltot // LANES, t_len // t_chunk),
        in_specs=[jspec] * 5 + [vspec, sspec],
        out_specs=[vspec, sspec],
        out_shape=[_sds(v.shape), _sds((n_rows, A_HEAD, ltot))],
        compiler_params=_params("arbitrary", "arbitrary"),
        name="rwkv_scan",
    )(r, d, k, kk, kka, v, s0)


def _to_scan_kernel(x_ref, o_ref, *, bsz, pack, val_mode):
    n_idx = A_HEAD // pack if val_mode else A_HEAD
    for j0 in range(0, n_idx, SUBLANES):
        tiles = []
        for j in range(j0, j0 + SUBLANES):
            if val_mode:
                rows = [x_ref[b, pl.ds(g * n_idx + j, A_HEADS, stride=A_HEAD), :]
                        for g in range(pack) for b in range(bsz)]
            else:
                rows = [x_ref[b, pl.ds(j, A_HEADS, stride=A_HEAD), :] for b in range(bsz)] * pack
            tiles.append(jnp.concatenate(rows, axis=0).T)
        o_ref[:, j0:j0 + SUBLANES, :] = jnp.swapaxes(jnp.stack(tiles, axis=0), 0, 1)


def _to_scan_layout(a, bsz, t_len, pack, val_mode):
    per = t_len // RELAYOUT_T
    n_idx = A_HEAD // pack if val_mode else A_HEAD
    return pl.pallas_call(
        functools.partial(_to_scan_kernel, bsz=bsz, pack=pack, val_mode=val_mode),
        grid=(per,),
        in_specs=[pl.BlockSpec((bsz, MIX_A, RELAYOUT_T), lambda tb: (0, 0, tb))],
        out_specs=pl.BlockSpec((RELAYOUT_T, n_idx, LANES), lambda tb: (tb, 0, 0)),
        out_shape=_sds((t_len, n_idx, LANES)),
        compiler_params=_params("arbitrary"),
        name="to_scan_val" if val_mode else "to_scan_key",
    )(a)


def _from_scan_kernel(y_ref, o_ref, pt_ref, *, bsz, pack):
    n_idx = A_HEAD // pack
    for j0 in range(0, n_idx, SUBLANES):
        tiles = jnp.swapaxes(y_ref[:, j0:j0 + SUBLANES, :], 0, 1)
        for j in range(j0, j0 + SUBLANES):
            tt = tiles[j - j0].T
            for g in range(pack):
                for b in range(bsz):
                    lo = (g * bsz + b) * A_HEADS
                    pt_ref[b, pl.ds(g * n_idx + j, A_HEADS, stride=A_HEAD), :] = tt[lo:lo + A_HEADS, :]
    for b in range(bsz):
        o_ref[b] = pt_ref[b].T


def _from_scan_layout(y, bsz, t_len, pack):
    per = t_len // RELAYOUT_T
    n_idx = A_HEAD // pack
    out = pl.pallas_call(
        functools.partial(_from_scan_kernel, bsz=bsz, pack=pack),
        grid=(per,),
        in_specs=[pl.BlockSpec((RELAYOUT_T, n_idx, LANES), lambda tb: (tb, 0, 0))],
        out_specs=pl.BlockSpec((bsz, RELAYOUT_T, MIX_A), lambda tb: (0, tb, 0)),
        out_shape=_sds((bsz, t_len, MIX_A)),
        scratch_shapes=[pltpu.VMEM((bsz, MIX_A, RELAYOUT_T), f32)],
        compiler_params=_params("arbitrary"),
        name="from_scan",
    )(y)
    return out.reshape(bsz * t_len, MIX_A)


def _rwkv_post_kernel(y_ref, bonus_ref, g_ref, lnw_ref, lnb_ref, g1_ref, g1t_ref, o_ref):
    y = y_ref[...]
    g1, g1t = g1_ref[...], g1t_ref[...]
    mu = _head_sum(y, g1, g1t) * (1.0 / A_HEAD)
    yc = y - mu
    var = _head_sum(yc * yc, g1, g1t) * (1.0 / A_HEAD)
    yn = yc * lax.rsqrt(var + A_GN_EPS) * lnw_ref[...] + lnb_ref[...]
    o_ref[...] = (yn + bonus_ref[...]) * g_ref[...]


def _rwkv_post(y, bonus, g, p, g1, g1t):
    n = y.shape[0]
    tok = pl.BlockSpec((TM_EW, MIX_A), lambda i: (i, 0))
    full = lambda a: pl.BlockSpec(a.shape, lambda i: (0,) * a.ndim)
    consts = [p['a_lnw'].reshape(1, -1), p['a_lnb'].reshape(1, -1), g1, g1t]
    return pl.pallas_call(
        _rwkv_post_kernel,
        grid=(n // TM_EW,),
        in_specs=[tok] * 3 + [full(a) for a in consts],
        out_specs=tok, out_shape=_sds((n, MIX_A)),
        compiler_params=_params("arbitrary"),
        name="rwkv_post",
    )(y, bonus, g, *consts)


def _rwkv_mix(z, bsz, t_len, shift_buf, wkv0, p, wwa, g1, g1t):
    n = bsz * t_len
    r, k, v, d, kk, kka, g, bonus = _rwkv_prep(z, shift_buf, bsz, t_len, p, wwa, g1, g1t)

    bh = bsz * A_HEADS
    pack = LANES // bh if bh < LANES else 1
    n_rows = A_HEAD // pack

    s0 = wkv0.reshape(bsz, A_HEADS, pack, n_rows, A_HEAD).transpose(3, 4, 2, 0, 1).reshape(n_rows, A_HEAD, pack * bh)
    if pack * bh == LANES and t_len % RELAYOUT_T == 0:
        keys = [_to_scan_layout(a, bsz, t_len, pack, False) for a in (r, d, k, kk, kka)]
        y, s_out = _rwkv_scan(*keys, _to_scan_layout(v, bsz, t_len, pack, True), s0, SCAN_CHUNK)
        y = _from_scan_layout(y, bsz, t_len, pack)
    else:
        assert pack == 1 and t_len <= SCAN_CHUNK

        def scan_rows(a):
            return a.reshape(bsz, t_len, A_HEADS, A_HEAD).transpose(1, 3, 0, 2).reshape(t_len, A_HEAD, bh)

        y, s_out = _rwkv_scan(*(scan_rows(a) for a in (r, d, k, kk, kka, v)), s0, t_len)
        y = y.reshape(t_len, A_HEAD, bsz, A_HEADS).transpose(2, 0, 3, 1).reshape(n, MIX_A)
    new_wkv = s_out.reshape(n_rows, A_HEAD, pack, bsz, A_HEADS).transpose(3, 4, 2, 0, 1)
    new_wkv = new_wkv.reshape(bsz, A_HEADS, A_HEAD, A_HEAD)
    y = _rwkv_post(y, bonus, g, p, g1, g1t)
    new_shift = z.reshape(bsz, t_len, Z_MAIN_COLS)[:, -1, :A_SHIFT_COLS]
    return y, new_shift, new_wkv


def _log_sigmoid(x):
    return jnp.minimum(x, 0.0) - jnp.log(1.0 + jnp.exp(-jnp.abs(x)))


def _mlstm_kernel(*refs, chunk, t_blk, bb):
    zq_refs = refs[0:2 * B_HEADS]
    v_refs = refs[2 * B_HEADS:3 * B_HEADS]
    o_refs = refs[3 * B_HEADS:4 * B_HEADS]
    (cw_ref, cb_ref, cbuf_ref, gc_ref, gr_ref, bias_r_ref, bias_c_ref, ng_ref, c0_ref, n0_ref, m0_ref,
     y_ref, c_ref, n_ref, m_ref, tail_ref) = refs[4 * B_HEADS:]
    n_tail = B_CONV - 1

    def pad_rows(x):
        if t_blk == chunk:
            return x
        return jnp.concatenate([x, jnp.zeros((chunk - t_blk, x.shape[1]), x.dtype)], axis=0)

    @pl.when(pl.program_id(1) == 0)
    def _():
        c_ref[...] = c0_ref[...]
        n_ref[...] = n0_ref[...]
        m_ref[...] = m0_ref[...]
        tail_ref[...] = jnp.zeros_like(tail_ref)
        for bi in range(bb):
            tail_ref[(bi + 1) * SUBLANES - n_tail:(bi + 1) * SUBLANES, :] = cbuf_ref[bi]

    def conv_silu(cb, bi):
        cols = slice(cb * B_DQK, (cb + 1) * B_DQK)
        blk = pad_rows(zq_refs[cb][bi])
        zp = jnp.concatenate([tail_ref[bi * SUBLANES:(bi + 1) * SUBLANES, cols], blk], axis=0)
        acc = cb_ref[:, cols]
        for w in range(B_CONV):
            off = SUBLANES - n_tail + w
            acc = acc + zp[off:off + chunk, :] * cw_ref[w:w + 1, cols]
        tail_ref[(bi + 1) * SUBLANES - n_tail:(bi + 1) * SUBLANES, cols] = blk[chunk - n_tail:chunk, :]
        return acc * jax.nn.sigmoid(acc)

    ti = lax.broadcasted_iota(jnp.int32, (chunk, chunk), 0)
    si = lax.broadcasted_iota(jnp.int32, (chunk, chunk), 1)
    tri = si <= ti
    nt = (((1,), (1,)), ((), ()))
    tn = (((0,), (0,)), ((), ()))
    def head_chain(bi, h):
        gcol = gc_ref[bi] + bias_r_ref[...]
        grow = gr_ref[bi, 0] + bias_c_ref[...]
        q = conv_silu(h, bi)
        k = conv_silu(B_HEADS + h, bi) * (B_DQK ** -0.5)
        v = pad_rows(v_refs[h][bi])
        yield
        i_col, f_col = gcol[:, h:h + 1], gcol[:, B_HEADS + h:B_HEADS + h + 1]
        i_row, f_row = grow[h:h + 1, :], grow[B_HEADS + h:B_HEADS + h + 1, :]
        lf_col, lf_row = _log_sigmoid(f_col), _log_sigmoid(f_row)
        b_col = jnp.sum(jnp.where(tri, lf_row, 0.0), axis=1, keepdims=True)
        b_row = jnp.sum(jnp.where(ti <= si, lf_col, 0.0), axis=0, keepdims=True)
        m_prev = m_ref[bi, h:h + 1, :]
        yield
        log_d = jnp.where(tri, b_col - b_row + i_row, -jnp.inf)
        inter = b_col + m_prev
        m_t = jnp.maximum(inter, jnp.max(log_d, axis=1, keepdims=True))
        qb, kb, vb = q.astype(bf16), k.astype(bf16), v.astype(bf16)
        yield
        s = lax.dot_general(qb, kb, nt, preferred_element_type=f32) * jnp.exp(log_d - m_t)
        w_inter = jnp.exp(inter - m_t)
        c_old = c_ref[bi, h]
        n_old = n_ref[bi, h:h + 1, :]
        yield
        num = (jnp.dot(s.astype(bf16), vb, preferred_element_type=f32)
               + w_inter * jnp.dot(qb, c_old.astype(bf16), preferred_element_type=f32))
        den = jnp.sum(s, axis=1, keepdims=True) + w_inter * jnp.sum(q * n_old, axis=1, keepdims=True)
        yield
        hh = num / jnp.maximum(jnp.abs(den), jnp.exp(-m_t))
        m_new = m_t[chunk - 1:chunk, :]
        b_last = b_col[chunk - 1:chunk, :]
        ws = jnp.exp(b_last - b_col + i_col - m_new)
        d_c = jnp.exp(b_last + m_prev - m_new)
        kw = k * ws
        yield
        c_ref[bi, h] = d_c * c_old + lax.dot_general(kw.astype(bf16), vb, tn, preferred_element_type=f32)
        n_ref[bi, h:h + 1, :] = d_c * n_old + jnp.sum(kw, axis=0, keepdims=True)
        m_ref[bi, h:h + 1, :] = m_new
        yield
        hn = hh * lax.rsqrt(jnp.mean(hh * hh, axis=1, keepdims=True) + NORM_EPS)
        y_ref[bi, :, h * B_DV:(h + 1) * B_DV] = ((hn * ng_ref[:, h * B_DV:(h + 1) * B_DV])[0:t_blk, :]
                                                 * jax.nn.sigmoid(o_refs[h][bi]))

    _interleave(*[head_chain(bi, h) for bi in range(bb) for h in range(B_HEADS)])


def _interleave(*gens):
    live = list(gens)
    while live:
        for gen in list(live):
            try:
                next(gen)
            except StopIteration:
                live.remove(gen)


MLSTM_BATCH = 4


def _mlstm(z3, conv_buf, gates_col, gates_row, p, c0, n0, m0, bsz, n_chunks, chunk):
    t_rows = z3.shape[1]
    t_blk = chunk if t_rows % chunk == 0 else t_rows
    assert t_blk == chunk or n_chunks == 1
    bb = MLSTM_BATCH if bsz % MLSTM_BATCH == 0 else 1
    qk_col0 = A_SHIFT_COLS // B_DQK
    v_col0 = (A_SHIFT_COLS + B_QK_COLS) // B_DV
    o_col0 = v_col0 + B_HEADS
    tok = lambda w, rows=t_blk: pl.BlockSpec((bb, rows, w), lambda b, c: (b, c, 0))
    zcol = lambda w, cb: pl.BlockSpec((bb, t_blk, w), lambda b, c: (b, c, cb))
    full = lambda a: pl.BlockSpec(a.shape, lambda b, c: (0,) * a.ndim)
    bias = jnp.concatenate([p['b_ig_bias'], p['b_fg_bias']])
    bias_r = jnp.pad(bias, (0, LANES - 2 * B_HEADS)).reshape(1, LANES)
    bias_c = bias.reshape(2 * B_HEADS, 1)
    ng = p['b_norm_g'].reshape(1, -1)
    cw, cb = p['b_conv_w'], p['b_conv_b'].reshape(1, -1)
    cspec = pl.BlockSpec((bb, B_HEADS, B_DQK, B_DV), lambda b, c: (b, 0, 0, 0))
    nspec = pl.BlockSpec((bb, B_HEADS, B_DQK), lambda b, c: (b, 0, 0))
    mspec = pl.BlockSpec((bb, B_HEADS, 1), lambda b, c: (b, 0, 0))
    return pl.pallas_call(
        functools.partial(_mlstm_kernel, chunk=chunk, t_blk=t_blk, bb=bb),
        grid=(bsz // bb, n_chunks),
        in_specs=[zcol(B_DQK, qk_col0 + j) for j in range(2 * B_HEADS)]
                 + [zcol(B_DV, v_col0 + h) for h in range(B_HEADS)]
                 + [zcol(B_DV, o_col0 + h) for h in range(B_HEADS)]
                 + [full(cw), full(cb), pl.BlockSpec((bb, B_CONV - 1, B_QK_COLS), lambda b, c: (b, 0, 0)),
                    tok(LANES, chunk), pl.BlockSpec((bb, 1, 2 * B_HEADS, chunk), lambda b, c: (b, c, 0, 0)),
                    full(bias_r), full(bias_c), full(ng), cspec, nspec, mspec],
        out_specs=[tok(B_HEADS * B_DV), cspec, nspec, mspec],
        out_shape=[_sds((bsz, t_rows, B_HEADS * B_DV)), _sds(c0.shape), _sds(n0.shape), _sds(m0.shape)],
        scratch_shapes=[pltpu.VMEM((bb * SUBLANES, B_QK_COLS), f32)],
        compiler_params=_params("arbitrary", "arbitrary"),
        name="mlstm",
    )(*([z3] * (4 * B_HEADS)), cw, cb, conv_buf, gates_col, gates_row, bias_r, bias_c, ng, c0, n0, m0)


def _mlstm_mix(z, gates, bsz, t_len, conv_buf, c0, n0, m0, p):
    n = bsz * t_len
    z3 = z.reshape(bsz, t_len, Z_MAIN_COLS)
    zqk = z3[:, :, A_SHIFT_COLS:A_SHIFT_COLS + B_QK_COLS]
    new_conv = jnp.concatenate([conv_buf, zqk[:, max(t_len - (B_CONV - 1), 0):]], axis=1)[:, -(B_CONV - 1):]
    chunk = B_CHUNK if t_len % B_CHUNK == 0 else SUBLANES
    t_pad = -(-t_len // chunk) * chunk
    g3 = gates.reshape(bsz, t_len, LANES)
    if t_pad != t_len:
        lane = jnp.arange(LANES)
        fill = jnp.where(lane < B_HEADS, -1e30, jnp.where(lane < 2 * B_HEADS, 1e30, 0.0)).astype(f32)
        g3 = jnp.concatenate([g3, jnp.broadcast_to(fill, (bsz, t_pad - t_len, LANES))], axis=1)
    n_chunks = t_pad // chunk
    gates_col = g3
    gates_row = g3[:, :, :2 * B_HEADS].reshape(bsz, n_chunks, chunk, 2 * B_HEADS).transpose(0, 1, 3, 2)
    y, c_new, n_new, m_new = _mlstm(z3, conv_buf, gates_col, gates_row, p, c0, n0, m0.reshape(bsz, B_HEADS, 1),
                                    bsz, n_chunks, chunk)
    y = y.reshape(n, -1)
    return y, new_conv, c_new, n_new, m_new.reshape(bsz, B_HEADS)


def _oproj_kernel(ya_ref, yb_ref, wt_ref, wb_ref, x_ref, ga_ref, o_ref):
    acc = jnp.dot(ya_ref[...].astype(bf16), wt_ref[...], preferred_element_type=f32)
    acc = acc + jnp.dot(yb_ref[...].astype(bf16), wb_ref[...], preferred_element_type=f32)
    o_ref[...] = x_ref[...] + ga_ref[0] * acc


def _oproj(ya, yb, w_out_bf, x, ga3, idx):
    n = x.shape[0]
    half = D_MODEL // 2
    tok = lambda w: pl.BlockSpec((TM, w), lambda i: (i, 0))
    return pl.pallas_call(
        _oproj_kernel,
        grid=(n // TM,),
        in_specs=[tok(half), tok(half),
                  pl.BlockSpec((half, D_MODEL), lambda i: (0, 0)), pl.BlockSpec((half, D_MODEL), lambda i: (1, 0)),
                  tok(D_MODEL), _mod_spec(ga3, idx)],
        out_specs=tok(D_MODEL),
        out_shape=_sds((n, D_MODEL)),
        compiler_params=_params("arbitrary"),
        name="out_proj",
    )(ya, yb, w_out_bf, w_out_bf, x, ga3)


def _route_kernel(q_ref, keys_ref, cnt_ref, e1_ref, r2_ref, e2_ref, v1_scr, v2_scr):
    q = q_ref[...]
    tm = q.shape[0]
    neg_inf = -jnp.inf
    iota_k = lax.broadcasted_iota(jnp.int32, (PEER_NKEYS, tm), 0).astype(f32)
    nt = (((1,), (1,)), ((), ()))
    ranks, es = [], []
    for half, vscr in enumerate((v1_scr, v2_scr)):
        qp = q[:, half * 128:(half + 1) * 128].astype(bf16)
        s = lax.dot_general(keys_ref[half, 0].astype(bf16), qp, nt, preferred_element_type=f32)
        rank = jnp.full((PEER_NKEYS, tm), float(PEER_TOPK), f32)
        work = s
        for a in range(PEER_TOPK):
            m = jnp.max(work, axis=0, keepdims=True)
            idx = jnp.min(jnp.where(work == m, iota_k, float(PEER_NKEYS)), axis=0, keepdims=True)
            sel = iota_k == idx
            rank = jnp.where(sel, float(a), rank)
            work = jnp.where(sel, neg_inf, work)
            vscr[a:a + 1, :] = m
        ranks.append(rank)
        es.append(jnp.exp(s - vscr[0:1, :]))
    v1, v2 = v1_scr[...], v2_scr[...]
    row8 = lax.broadcasted_iota(jnp.int32, (SUBLANES, tm), 0).astype(f32)
    row16 = lax.broadcasted_iota(jnp.int32, (PEER_TOPK, tm), 0).astype(f32)
    pieces, poss = [v1[0:1, :] + v2], [row16]
    for a in range(1, 8):
        nb = PEER_TOPK // (a + 1)
        pieces.append(jnp.where(row8 < float(nb), v1[a:a + 1, :] + v2[0:SUBLANES, :], neg_inf))
        poss.append(row8 + float(a * PEER_TOPK))
    pieces.append(v1[8:16, :] + v2[0:1, :])
    poss.append((row8 + 8.0) * float(PEER_TOPK))
    cand = jnp.concatenate(pieces, axis=0)
    pos = jnp.concatenate(poss, axis=0)
    cnt16 = jnp.zeros((PEER_TOPK, tm), f32)
    zsum = None
    best0 = None
    for j in range(PEER_TOPK):
        m = jnp.max(cand, axis=0, keepdims=True)
        pj = jnp.min(jnp.where(cand == m, pos, 1e9), axis=0, keepdims=True)
        cand = jnp.where(pos == pj, neg_inf, cand)
        if j == 0:
            best0 = m
            zsum = jnp.ones_like(m)
        else:
            zsum = zsum + jnp.exp(m - best0)
        a_j = jnp.floor(pj * (1.0 / PEER_TOPK))
        cnt16 = cnt16 + jnp.where(row16 == a_j, 1.0, 0.0)
    cnt_i = jnp.zeros((PEER_NKEYS, tm), f32)
    for a in range(PEER_TOPK):
        cnt_i = jnp.where(ranks[0] == float(a), cnt16[a:a + 1, :], cnt_i)
    cnt_ref[0] = cnt_i
    e1_ref[0] = es[0] / zsum
    r2_ref[0] = ranks[1].astype(bf16)
    e2_ref[0] = es[1].astype(bf16)


def _route(q, keys):
    n = q.shape[0]
    ospec = pl.BlockSpec((1, PEER_NKEYS, TM_ROUTE), lambda i, h: (h, 0, i))
    oshape = (PEER_HEADS, PEER_NKEYS, n)
    return pl.pallas_call(
        _route_kernel,
        grid=(n // TM_ROUTE, PEER_HEADS),
        in_specs=[pl.BlockSpec((TM_ROUTE, 256), lambda i, h: (i, h)),
                  pl.BlockSpec((2, 1, PEER_NKEYS, 128), lambda i, h: (0, h, 0, 0))],
        out_specs=[ospec] * 4,
        out_shape=[_sds(oshape), _sds(oshape), _sds(oshape, bf16), _sds(oshape, bf16)],
        scratch_shapes=[pltpu.VMEM((PEER_TOPK, TM_ROUTE), f32), pltpu.VMEM((PEER_TOPK, TM_ROUTE), f32)],
        compiler_params=_params("arbitrary", "arbitrary"),
        name="peer_route",
    )(q, keys)


def _gated_act(h, cnt_ref, e1_ref, r2_ref, e2_ref, row0):
    zero = jnp.zeros((), bf16)
    parts = []
    for q in range(TE_HALF // PEER_NKEYS):
        row = row0 + q
        hs = h[q * PEER_NKEYS:(q + 1) * PEER_NKEYS, :]
        gate = None
        for hd in range(PEER_HEADS):
            cnt_row = cnt_ref[hd, row:row + 1, :].astype(bf16)
            e1_row = e1_ref[hd, row:row + 1, :].astype(bf16)
            term = jnp.where(r2_ref[hd] < cnt_row, e2_ref[hd] * e1_row, zero)
            gate = term if gate is None else gate + term
        gelu = 0.5 * hs * (1.0 + lax.erf(hs * (2.0 ** -0.5)))
        parts.append(gelu.astype(bf16) * gate)
    return jnp.concatenate(parts, axis=0)


def _dense_kernel(hn_ref, u_ref, vt_ref, vtp_ref, cnt_ref, e1_ref, cntp_ref, e1p_ref, r2_ref, e2_ref, o_ref,
                  xt_scr, acc_scr, hb_scr, *, n_chunks):
    c = pl.program_id(1)

    @pl.when(c == 0)
    def _():
        xt_scr[...] = hn_ref[...].T
        acc_scr[...] = jnp.zeros_like(acc_scr)
        hb_scr[...] = jnp.zeros_like(hb_scr)

    sub = TE_HALF // PEER_NKEYS
    xt = xt_scr[...]
    h_a = jnp.dot(u_ref[0:TE_HALF, :], xt, preferred_element_type=f32)
    act_bp = _gated_act(hb_scr[...], cntp_ref, e1p_ref, r2_ref, e2_ref, sub)
    part = jnp.dot(vtp_ref[...], act_bp, preferred_element_type=f32)
    hb_scr[...] = jnp.dot(u_ref[TE_HALF:TE, :], xt, preferred_element_type=f32)
    act_a = _gated_act(h_a, cnt_ref, e1_ref, r2_ref, e2_ref, 0)
    part = jnp.dot(vt_ref[:, 0:TE_HALF], act_a, preferred_element_type=f32) + part
    acc_scr[...] += part

    @pl.when(c == n_chunks - 1)
    def _():
        act_b = _gated_act(hb_scr[...], cnt_ref, e1_ref, r2_ref, e2_ref, sub)
        total = acc_scr[...] + jnp.dot(vt_ref[:, TE_HALF:TE], act_b, preferred_element_type=f32)
        o_ref[...] = total.T


def _peer_dense(hn, u_bf, vt_bf, cnt, e1, r2, e2):
    n = hn.shape[0]
    n_exp = u_bf.shape[0]
    n_chunks = n_exp // TE
    sub = TE // PEER_NKEYS
    prev = lambda c: jnp.maximum(c - 1, 0)
    return pl.pallas_call(
        functools.partial(_dense_kernel, n_chunks=n_chunks),
        grid=(n // TM, n_chunks),
        in_specs=[pl.BlockSpec((TM, D_MODEL), lambda i, c: (i, 0)),
                  pl.BlockSpec((TE, D_MODEL), lambda i, c: (c, 0)),
                  pl.BlockSpec((D_MODEL, TE), lambda i, c: (0, c)),
                  pl.BlockSpec((D_MODEL, TE_HALF), lambda i, c: (0, jnp.maximum(2 * c - 1, 0))),
                  pl.BlockSpec((PEER_HEADS, sub, TM), lambda i, c: (0, c, i)),
                  pl.BlockSpec((PEER_HEADS, sub, TM), lambda i, c: (0, c, i)),
                  pl.BlockSpec((PEER_HEADS, sub, TM), lambda i, c: (0, prev(c), i)),
                  pl.BlockSpec((PEER_HEADS, sub, TM), lambda i, c: (0, prev(c), i)),
                  pl.BlockSpec((PEER_HEADS, PEER_NKEYS, TM), lambda i, c: (0, 0, i)),
                  pl.BlockSpec((PEER_HEADS, PEER_NKEYS, TM), lambda i, c: (0, 0, i))],
        out_specs=pl.BlockSpec((TM, D_MODEL), lambda i, c: (i, 0)),
        out_shape=_sds((n, D_MODEL)),
        scratch_shapes=[pltpu.VMEM((D_MODEL, TM), bf16), pltpu.VMEM((D_MODEL, TM), f32),
                        pltpu.VMEM((TE_HALF, TM), f32)],
        compiler_params=_params("arbitrary", "arbitrary"),
        name="peer_dense",
    )(hn, u_bf, vt_bf, vt_bf, cnt, e1, cnt, e1, r2, e2)


def _peer_weights_kernel(u_ref, v_ref, ub_ref, vt_ref):
    ub_ref[...] = u_ref[...].astype(bf16)
    vt_ref[...] = v_ref[...].astype(bf16).T


def _peer_weights(u, v):
    n_exp, te = u.shape[0], 512
    rows = pl.BlockSpec((te, D_MODEL), lambda i: (i, 0))
    return pl.pallas_call(
        _peer_weights_kernel,
        grid=(n_exp // te,),
        in_specs=[rows, rows],
        out_specs=[rows, pl.BlockSpec((D_MODEL, te), lambda i: (0, i))],
        out_shape=[_sds((n_exp, D_MODEL), bf16), _sds((D_MODEL, n_exp), bf16)],
        compiler_params=_params("arbitrary"),
        name="peer_weights",
    )(u, v)


def _final_kernel(x_ref, f_ref, gc_ref, g_ref, sc_ref, sh_ref, o_ref):
    x = x_ref[...] + gc_ref[0] * f_ref[...]
    y = x * lax.rsqrt(jnp.mean(x * x, axis=-1, keepdims=True) + NORM_EPS) * g_ref[...]
    o_ref[...] = y * (1.0 + sc_ref[0]) + sh_ref[0]


def _final(x, ffn, gc3, g, sc3, sh3, idx):
    n = x.shape[0]
    tok = pl.BlockSpec((TM, D_MODEL), lambda i: (i, 0))
    return pl.pallas_call(
        _final_kernel,
        grid=(n // TM,),
        in_specs=[tok, tok, _mod_spec(gc3, idx), pl.BlockSpec((1, D_MODEL), lambda i: (0, 0)),
                  _mod_spec(sc3, idx), _mod_spec(sh3, idx)],
        out_specs=tok, out_shape=_sds((n, D_MODEL)),
        compiler_params=_params("arbitrary"),
        name="final_norm",
    )(x, ffn, gc3, g.reshape(1, D_MODEL), sc3, sh3)


def _trunk(x, mod, mod_f, states, p, shared):
    bsz, t_len, _ = x.shape
    n = bsz * t_len
    shift_s, wkv_s, conv_s, c_s, n_s, m_s = states
    xf = x.reshape(n, D_MODEL)
    m3 = [_mod3(m, bsz, t_len, TM) for m in jnp.split(mod, 6, axis=-1)]
    idx = m3[0][1]
    sh_a, sc_a, g_a, sh_c, sc_c, g_c = (a for a, _ in m3)
    (shift_f, _), (scale_f, _) = (_mod3(m, bsz, t_len, TM) for m in jnp.split(mod_f, 2, axis=-1))

    z, gates = _norm_mod_matmul(xf, p['norm_mix_g'], sc_a, sh_a, idx, shared['w_in_bf'], Z_MAIN_COLS, 3200,
                                w_gate=shared['w_gate'], name="in_proj")
    y_a, new_shift, new_wkv = _rwkv_mix(z, bsz, t_len, shift_s, wkv_s, p, shared['wwa'], shared['g1'],
                                        shared['g1t'])
    y_b, new_conv, new_c, new_n, new_m = _mlstm_mix(z, gates, bsz, t_len, conv_s, c_s, n_s, m_s, p)
    x1 = _oproj(y_a, y_b, shared['w_out_bf'], xf, g_a, idx)
    q, hn = _norm_mod_matmul(x1, p['norm_ffn_g'], sc_c, sh_c, idx, shared['wq_bf'], D_MODEL, D_MODEL,
                             emit_hn=True, name="peer_query")
    cnt, e1, r2, e2 = _route(q, p['peer_keys'])
    ffn = _peer_dense(hn, shared['u_bf'], shared['vt_bf'], cnt, e1, r2, e2)
    y = _final(x1, ffn, g_c, p['norm_final_g'], scale_f, shift_f, idx)
    return y.reshape(bsz, t_len, D_MODEL), (new_shift, new_wkv, new_conv, new_c, new_n, new_m)


def kernel(x_prompt, x_sample, c_prompt, c_sample, state_rwkv_shift, state_rwkv_wkv, state_mlstm_conv,
           state_mlstm_C, state_mlstm_n, state_mlstm_m, w_ada, b_ada, norm_mix_g, w_in, a_mu, a_w0, a_w2,
           a_a0, a_a2, a_g2, a_kk, a_ka, a_rk, a_lnw, a_lnb, b_conv_w, b_conv_b, b_ig_bias, b_fg_bias,
           b_norm_g, w_out, norm_ffn_g, peer_wq, peer_keys, peer_u, peer_v, w_ada_final, b_ada_final,
           norm_final_g):
    layer = dict(w_ada=w_ada, b_ada=b_ada, norm_mix_g=norm_mix_g, w_in=w_in, a_mu=a_mu, a_w0=a_w0, a_w2=a_w2,
                 a_a0=a_a0, a_a2=a_a2, a_g2=a_g2, a_kk=a_kk, a_ka=a_ka, a_rk=a_rk, a_lnw=a_lnw, a_lnb=a_lnb,
                 b_conv_w=b_conv_w, b_conv_b=b_conv_b, b_ig_bias=b_ig_bias, b_fg_bias=b_fg_bias,
                 b_norm_g=b_norm_g, w_out=w_out, norm_ffn_g=norm_ffn_g, peer_wq=peer_wq, peer_keys=peer_keys,
                 peer_u=peer_u, peer_v=peer_v)
    assert w_ada.shape[0] == 1, "single-layer trunk"
    p = {k: v[0] for k, v in layer.items()}
    p['norm_final_g'] = norm_final_g
    bp, bs = x_prompt.shape[0], x_sample.shape[0]

    c_all = jnp.concatenate([c_prompt, c_sample], axis=0)
    c_all = jnp.pad(c_all, ((0, -c_all.shape[0] % SUBLANES), (0, 0)))
    mod = _ada(c_all, p['w_ada'], p['b_ada'])
    mod_f = _ada(c_all, w_ada_final, b_ada_final)

    head_of = jnp.arange(MIX_A) // A_HEAD
    g1 = (head_of[:, None] == jnp.arange(LANES)[None, :]).astype(bf16)
    zero = jnp.zeros((64, MIX_A), f32)
    wwa = jnp.concatenate([jnp.concatenate([p['a_w2'], zero], axis=1),
                           jnp.concatenate([zero, p['a_a2']], axis=1)], axis=0).astype(bf16)
    w_in_bf = p['w_in'].astype(bf16)
    shared = dict(
        g1=g1, g1t=g1.T, wwa=wwa, w_in_bf=w_in_bf,
        w_gate=jnp.pad(w_in_bf[:, Z_MAIN_COLS:], ((0, 0), (0, LANES - 2 * B_HEADS))),
        w_out_bf=p['w_out'].astype(bf16), wq_bf=p['peer_wq'].astype(bf16))
    shared['u_bf'], shared['vt_bf'] = _peer_weights(p['peer_u'], p['peer_v'])

    dt = x_prompt.dtype
    zero_states = (jnp.zeros((bp, A_SHIFT_COLS), dt), jnp.zeros((bp, A_HEADS, A_HEAD, A_HEAD), dt),
                   jnp.zeros((bp, B_CONV - 1, B_QK_COLS), dt), jnp.zeros((bp, B_HEADS, B_DQK, B_DV), dt),
                   jnp.zeros((bp, B_HEADS, B_DQK), dt), jnp.zeros((bp, B_HEADS), dt))
    y_prompt, p_states = _trunk(x_prompt, mod[:bp], mod_f[:bp], zero_states, p, shared)
    s_states_in = (state_rwkv_shift[0], state_rwkv_wkv[0], state_mlstm_conv[0], state_mlstm_C[0],
                   state_mlstm_n[0], state_mlstm_m[0])
    y_sample, s_states = _trunk(x_sample, mod[bp:bp + bs], mod_f[bp:bp + bs], s_states_in, p, shared)
    return (y_prompt, y_sample) + tuple(s[None] for s in p_states) + tuple(s[None] for s in s_states)
```

```python
import functools

import jax
import jax.numpy as jnp
from jax import lax
from jax.experimental import pallas as pl
from jax.experimental.pallas import tpu as pltpu

f32 = jnp.float32
bf16 = jnp.bfloat16

D_MODEL = 2048
MIX_A = 1024
A_HEAD = 64
A_HEADS = 16
A_SHIFT_COLS = 3 * MIX_A + 64 + 64 + 128
A_GN_EPS = A_HEAD * 1e-5
B_QK_COLS = 1024
B_HEADS = 4
B_DV = 256
B_DQK = 128
B_CONV = 4
B_CHUNK = 64
Z_MAIN_COLS = A_SHIFT_COLS + B_QK_COLS + 2 * 1024
PEER_HEADS = 8
PEER_NKEYS = 128
PEER_TOPK = 16
NORM_EPS = 1e-6

LANES = 128
SUBLANES = 8
VMEM_LIMIT_BYTES = 56 * 1024 * 1024

TM = 512
TM_EW = 256
TM_ROUTE = 1024
TE = 1024
TE_HALF = TE // 2


def _params(*sem):
    return pltpu.CompilerParams(dimension_semantics=sem, vmem_limit_bytes=VMEM_LIMIT_BYTES)


def _sds(shape, dtype=f32):
    return jax.ShapeDtypeStruct(shape, dtype)


def _ada_kernel(c_ref, w_ref, b_ref, o_ref):
    c = c_ref[...]
    a = (c * jax.nn.sigmoid(c)).astype(bf16)
    o_ref[...] = jnp.dot(a, w_ref[...].astype(bf16), preferred_element_type=f32) + b_ref[...]


def _ada(c, w, b):
    m, n_out, tn = c.shape[0], w.shape[1], 1024
    return pl.pallas_call(
        _ada_kernel,
        grid=(n_out // tn,),
        in_specs=[pl.BlockSpec((m, D_MODEL), lambda j: (0, 0)),
                  pl.BlockSpec((D_MODEL, tn), lambda j: (0, j)),
                  pl.BlockSpec((1, tn), lambda j: (0, j))],
        out_specs=pl.BlockSpec((m, tn), lambda j: (0, j)),
        out_shape=_sds((m, n_out)),
        compiler_params=_params("arbitrary"),
        name="ada_mod",
    )(c, w, b.reshape(1, n_out))


def _mod3(m, bsz, t_len, tm):
    if t_len % tm == 0:
        per = t_len // tm
        return m.reshape(bsz, 1, D_MODEL), (lambda i: i // per)
    n = bsz * t_len
    assert tm % t_len == 0 and n % tm == 0
    return jnp.repeat(m, t_len, axis=0).reshape(n // tm, tm, D_MODEL), (lambda i: i)


def _mod_spec(arr, idx):
    return pl.BlockSpec((1, arr.shape[1], D_MODEL), lambda i, *_: (idx(i), 0, 0))


def _nmm_kernel(*refs, has_gate, emit_hn):
    it = iter(refs)
    x_ref, g_ref, sc_ref, sh_ref, w_ref = (next(it) for _ in range(5))
    wg_ref = next(it) if has_gate else None
    o_ref = next(it)
    og_ref = next(it) if has_gate else None
    hn_ref = next(it) if emit_hn else None
    hs_ref = next(it)

    @pl.when(pl.program_id(1) == 0)
    def _():
        x = x_ref[...]
        y = x * lax.rsqrt(jnp.mean(x * x, axis=-1, keepdims=True) + NORM_EPS) * g_ref[...]
        h = (y * (1.0 + sc_ref[0]) + sh_ref[0]).astype(bf16)
        hs_ref[...] = h
        if emit_hn:
            hn_ref[...] = h
        if has_gate:
            og_ref[...] = jnp.dot(h, wg_ref[...], preferred_element_type=f32)

    o_ref[...] = jnp.dot(hs_ref[...], w_ref[...], preferred_element_type=f32)


def _norm_mod_matmul(x, g, sc3, sh3, idx, w_bf, n_out, tn, w_gate=None, emit_hn=False, name="nmm"):
    n = x.shape[0]
    has_gate = w_gate is not None
    in_specs = [pl.BlockSpec((TM, D_MODEL), lambda i, j: (i, 0)),
                pl.BlockSpec((1, D_MODEL), lambda i, j: (0, 0)),
                _mod_spec(sc3, idx), _mod_spec(sh3, idx),
                pl.BlockSpec((D_MODEL, tn), lambda i, j: (0, j))]
    args = [x, g.reshape(1, D_MODEL), sc3, sh3, w_bf]
    out_specs = [pl.BlockSpec((TM, tn), lambda i, j: (i, j))]
    out_shape = [_sds((n, n_out))]
    if has_gate:
        in_specs.append(pl.BlockSpec((D_MODEL, LANES), lambda i, j: (0, 0)))
        args.append(w_gate)
        out_specs.append(pl.BlockSpec((TM, LANES), lambda i, j: (i, 0)))
        out_shape.append(_sds((n, LANES)))
    if emit_hn:
        out_specs.append(pl.BlockSpec((TM, D_MODEL), lambda i, j: (i, 0)))
        out_shape.append(_sds((n, D_MODEL), bf16))
    return pl.pallas_call(
        functools.partial(_nmm_kernel, has_gate=has_gate, emit_hn=emit_hn),
        grid=(n // TM, n_out // tn),
        in_specs=in_specs, out_specs=out_specs, out_shape=out_shape,
        scratch_shapes=[pltpu.VMEM((TM, D_MODEL), bf16)],
        compiler_params=_params("arbitrary", "arbitrary"),
        name=name,
    )(*args)


def _split_dot(x, w):
    hi = x.astype(bf16)
    lo = (x - hi.astype(f32)).astype(bf16)
    return (jnp.dot(hi, w, preferred_element_type=f32) + jnp.dot(lo, w, preferred_element_type=f32))


def _head_sum(x, g1, g1t):
    return _split_dot(_split_dot(x, g1), g1t)


def _rwkv_prep_kernel(z_ref, first_ref, mu_ref, w0_ref, wwa_ref, a0_ref, g2_ref, akk_ref, aka_ref, rk_ref,
                      g1_ref, g1t_ref, r_ref, k_ref, v_ref, d_ref, kk_ref, kka_ref, g_ref, bonus_ref,
                      *scratch, t_len, carry_mode):
    z = z_ref[...]
    tm = z.shape[0]
    row = lax.broadcasted_iota(jnp.int32, z.shape, 0)
    rolled = pltpu.roll(z, 1, axis=0)
    if carry_mode:
        carry_ref, = scratch
        first = jnp.where(pl.program_id(1) == 0, first_ref[0], carry_ref[0:1, :])
        prev = jnp.where(row == 0, first, rolled)
        carry_ref[0:1, :] = z[tm - 1:tm, :]
    else:
        prev = jnp.where(lax.rem(row, t_len) == 0, first_ref[...], rolled)
    zs = z + (prev - z) * mu_ref[...]
    r = zs[:, 0:MIX_A]
    k = zs[:, MIX_A:2 * MIX_A]
    v = zs[:, 2 * MIX_A:3 * MIX_A]
    xwa = zs[:, 3 * MIX_A:3 * MIX_A + 128]
    xg = zs[:, 3 * MIX_A + 128:3 * MIX_A + 256]
    lane = lax.broadcasted_iota(jnp.int32, xwa.shape, 1)
    xwa = jnp.where(lane < 64, jnp.tanh(xwa), xwa)
    lwa = jnp.dot(xwa.astype(bf16), wwa_ref[...], preferred_element_type=f32)
    u = w0_ref[...] + lwa[:, 0:MIX_A]
    w_log = -(jnp.maximum(-u, 0.0) + jnp.log(1.0 + jnp.exp(-jnp.abs(u)))) - 0.5
    decay = jnp.exp(-jnp.exp(w_log))
    a = jax.nn.sigmoid(a0_ref[...] + lwa[:, MIX_A:2 * MIX_A])
    g_ref[...] = jnp.dot(jax.nn.sigmoid(xg).astype(bf16), g2_ref[...].astype(bf16), preferred_element_type=f32)
    g1, g1t = g1_ref[...], g1t_ref[...]
    kk = k * akk_ref[...]
    nrm = jnp.sqrt(_head_sum(kk * kk, g1, g1t))
    kk = kk / jnp.maximum(nrm, 1e-12)
    k = k * (1.0 + (a - 1.0) * aka_ref[...])
    bonus_ref[...] = _head_sum(r * k * rk_ref[...], g1, g1t) * v
    for ref, val in ((r_ref, r), (k_ref, k), (v_ref, v), (d_ref, decay), (kk_ref, kk), (kka_ref, -(kk * a))):
        if carry_mode:
            ref[0] = val.T
        else:
            ref[...] = val


def _rwkv_prep(z, shift_buf, bsz, t_len, p, wwa, g1, g1t):
    n = bsz * t_len
    row = lambda a: a.reshape(1, -1)
    consts = [row(p['a_mu']), row(p['a_w0']), wwa, row(p['a_a0']), p['a_g2'], row(p['a_kk']), row(p['a_ka']),
              row(p['a_rk']), g1, g1t]
    carry_mode = t_len % TM_EW == 0
    if carry_mode:
        per = t_len // TM_EW
        grid = (bsz, per)
        tok = lambda w: pl.BlockSpec((TM_EW, w), lambda b, i: (b * per + i, 0))
        first = shift_buf.reshape(bsz, 1, A_SHIFT_COLS)
        first_spec = pl.BlockSpec((1, 1, A_SHIFT_COLS), lambda b, i: (b, 0, 0))
        scratch = [pltpu.VMEM((SUBLANES, A_SHIFT_COLS), f32)]
        sem = ("arbitrary", "arbitrary")
        scan_spec = pl.BlockSpec((1, MIX_A, TM_EW), lambda b, i: (b, 0, i))
        scan_shape = _sds((bsz, MIX_A, t_len))
    else:
        assert TM_EW % t_len == 0 and n % TM_EW == 0
        grid = (n // TM_EW,)
        tok = lambda w: pl.BlockSpec((TM_EW, w), lambda i: (i, 0))
        first = jnp.repeat(shift_buf, t_len, axis=0)
        first_spec = tok(A_SHIFT_COLS)
        scratch = []
        sem = ("arbitrary",)
        scan_spec, scan_shape = tok(MIX_A), _sds((n, MIX_A))
    full = lambda a: pl.BlockSpec(a.shape, lambda *_: (0,) * a.ndim)
    return pl.pallas_call(
        functools.partial(_rwkv_prep_kernel, t_len=t_len, carry_mode=carry_mode),
        grid=grid,
        in_specs=[tok(A_SHIFT_COLS), first_spec] + [full(a) for a in consts],
        out_specs=[scan_spec] * 6 + [tok(MIX_A)] * 2,
        out_shape=[scan_shape] * 6 + [_sds((n, MIX_A))] * 2,
        scratch_shapes=scratch,
        compiler_params=_params(*sem),
        name="rwkv_prep",
    )(z, first, *consts)


SCAN_CHUNK = 64
RELAYOUT_T = 128


def _rwkv_scan_kernel(r_ref, d_ref, k_ref, kk_ref, kka_ref, v_ref, s0_ref, y_ref, so_ref, *, n_rows, t_chunk):
    @pl.when(pl.program_id(1) == 0)
    def _():
        so_ref[...] = s0_ref[...]

    def step(t, gam):
        gam_new = gam * d_ref[t]
        inv = 1.0 / gam_new
        kk = kk_ref[t] * gam
        r = r_ref[t] * gam_new
        kka = kka_ref[t] * inv
        k = k_ref[t] * inv
        for i in range(n_rows):
            s_i = so_ref[i]
            sa = jnp.sum(s_i * kk, axis=0, keepdims=True)
            v_i = v_ref[t, pl.ds(i, 1), :]
            s_n = s_i + sa * kka + v_i * k
            so_ref[i] = s_n
            y_ref[t, pl.ds(i, 1), :] = jnp.sum(s_n * r, axis=0, keepdims=True)
        return gam_new

    gam = lax.fori_loop(0, t_chunk, step, jnp.ones((A_HEAD, LANES), f32))
    for i in range(n_rows):
        so_ref[i] = so_ref[i] * gam


def _rwkv_scan(r, d, k, kk, kka, v, s0, t_chunk):
    t_len, n_rows, ltot = v.shape
    jspec = pl.BlockSpec((t_chunk, A_HEAD, LANES), lambda l, t: (t, 0, l))
    vspec = pl.BlockSpec((t_chunk, n_rows, LANES), lambda l, t: (t, 0, l))
    sspec = pl.BlockSpec((n_rows, A_HEAD, LANES), lambda l, t: (0, 0, l))
    return pl.pallas_call(
        functools.partial(_rwkv_scan_kernel, n_rows=n_rows, t_chunk=t_chunk),
        grid=(ltot // LANES, t_len // t_chunk),
        in_specs=[jspec] * 5 + [vspec, sspec],
        out_specs=[vspec, sspec],
        out_shape=[_sds(v.shape), _sds((n_rows, A_HEAD, ltot))],
        compiler_params=_params("arbitrary", "arbitrary"),
        name="rwkv_scan",
    )(r, d, k, kk, kka, v, s0)


def _to_scan_kernel(x_ref, o_ref, *, bsz, pack, val_mode):
    n_idx = A_HEAD // pack if val_mode else A_HEAD
    for j0 in range(0, n_idx, SUBLANES):
        tiles = []
        for j in range(j0, j0 + SUBLANES):
            if val_mode:
                rows = [x_ref[b, pl.ds(g * n_idx + j, A_HEADS, stride=A_HEAD), :]
                        for g in range(pack) for b in range(bsz)]
            else:
                rows = [x_ref[b, pl.ds(j, A_HEADS, stride=A_HEAD), :] for b in range(bsz)] * pack
            tiles.append(jnp.concatenate(rows, axis=0).T)
        o_ref[:, j0:j0 + SUBLANES, :] = jnp.swapaxes(jnp.stack(tiles, axis=0), 0, 1)


def _to_scan_layout(a, bsz, t_len, pack, val_mode):
    per = t_len // RELAYOUT_T
    n_idx = A_HEAD // pack if val_mode else A_HEAD
    return pl.pallas_call(
        functools.partial(_to_scan_kernel, bsz=bsz, pack=pack, val_mode=val_mode),
        grid=(per,),
        in_specs=[pl.BlockSpec((bsz, MIX_A, RELAYOUT_T), lambda tb: (0, 0, tb))],
        out_specs=pl.BlockSpec((RELAYOUT_T, n_idx, LANES), lambda tb: (tb, 0, 0)),
        out_shape=_sds((t_len, n_idx, LANES)),
        compiler_params=_params("arbitrary"),
        name="to_scan_val" if val_mode else "to_scan_key",
    )(a)


def _from_scan_kernel(y_ref, o_ref, pt_ref, *, bsz, pack):
    n_idx = A_HEAD // pack
    for j0 in range(0, n_idx, SUBLANES):
        tiles = jnp.swapaxes(y_ref[:, j0:j0 + SUBLANES, :], 0, 1)
        for j in range(j0, j0 + SUBLANES):
            tt = tiles[j - j0].T
            for g in range(pack):
                for b in range(bsz):
                    lo = (g * bsz + b) * A_HEADS
                    pt_ref[b, pl.ds(g * n_idx + j, A_HEADS, stride=A_HEAD), :] = tt[lo:lo + A_HEADS, :]
    for b in range(bsz):
        o_ref[b] = pt_ref[b].T


def _from_scan_layout(y, bsz, t_len, pack):
    per = t_len // RELAYOUT_T
    n_idx = A_HEAD // pack
    out = pl.pallas_call(
        functools.partial(_from_scan_kernel, bsz=bsz, pack=pack),
        grid=(per,),
        in_specs=[pl.BlockSpec((RELAYOUT_T, n_idx, LANES), lambda tb: (tb, 0, 0))],
        out_specs=pl.BlockSpec((bsz, RELAYOUT_T, MIX_A), lambda tb: (0, tb, 0)),
        out_shape=_sds((bsz, t_len, MIX_A)),
        scratch_shapes=[pltpu.VMEM((bsz, MIX_A, RELAYOUT_T), f32)],
        compiler_params=_params("arbitrary"),
        name="from_scan",
    )(y)
    return out.reshape(bsz * t_len, MIX_A)


def _rwkv_post_kernel(y_ref, bonus_ref, g_ref, lnw_ref, lnb_ref, g1_ref, g1t_ref, o_ref):
    y = y_ref[...]
    g1, g1t = g1_ref[...], g1t_ref[...]
    mu = _head_sum(y, g1, g1t) * (1.0 / A_HEAD)
    yc = y - mu
    var = _head_sum(yc * yc, g1, g1t) * (1.0 / A_HEAD)
    yn = yc * lax.rsqrt(var + A_GN_EPS) * lnw_ref[...] + lnb_ref[...]
    o_ref[...] = (yn + bonus_ref[...]) * g_ref[...]


def _rwkv_post(y, bonus, g, p, g1, g1t):
    n = y.shape[0]
    tok = pl.BlockSpec((TM_EW, MIX_A), lambda i: (i, 0))
    full = lambda a: pl.BlockSpec(a.shape, lambda i: (0,) * a.ndim)
    consts = [p['a_lnw'].reshape(1, -1), p['a_lnb'].reshape(1, -1), g1, g1t]
    return pl.pallas_call(
        _rwkv_post_kernel,
        grid=(n // TM_EW,),
        in_specs=[tok] * 3 + [full(a) for a in consts],
        out_specs=tok, out_shape=_sds((n, MIX_A)),
        compiler_params=_params("arbitrary"),
        name="rwkv_post",
    )(y, bonus, g, *consts)


def _rwkv_mix(z, bsz, t_len, shift_buf, wkv0, p, wwa, g1, g1t):
    n = bsz * t_len
    r, k, v, d, kk, kka, g, bonus = _rwkv_prep(z, shift_buf, bsz, t_len, p, wwa, g1, g1t)

    bh = bsz * A_HEADS
    pack = LANES // bh if bh < LANES else 1
    n_rows = A_HEAD // pack

    s0 = wkv0.reshape(bsz, A_HEADS, pack, n_rows, A_HEAD).transpose(3, 4, 2, 0, 1).reshape(n_rows, A_HEAD, pack * bh)
    if pack * bh == LANES and t_len % RELAYOUT_T == 0:
        keys = [_to_scan_layout(a, bsz, t_len, pack, False) for a in (r, d, k, kk, kka)]
        y, s_out = _rwkv_scan(*keys, _to_scan_layout(v, bsz, t_len, pack, True), s0, SCAN_CHUNK)
        y = _from_scan_layout(y, bsz, t_len, pack)
    else:
        assert pack == 1 and t_len <= SCAN_CHUNK

        def scan_rows(a):
            return a.reshape(bsz, t_len, A_HEADS, A_HEAD).transpose(1, 3, 0, 2).reshape(t_len, A_HEAD, bh)

        y, s_out = _rwkv_scan(*(scan_rows(a) for a in (r, d, k, kk, kka, v)), s0, t_len)
        y = y.reshape(t_len, A_HEAD, bsz, A_HEADS).transpose(2, 0, 3, 1).reshape(n, MIX_A)
    new_wkv = s_out.reshape(n_rows, A_HEAD, pack, bsz, A_HEADS).transpose(3, 4, 2, 0, 1)
    new_wkv = new_wkv.reshape(bsz, A_HEADS, A_HEAD, A_HEAD)
    y = _rwkv_post(y, bonus, g, p, g1, g1t)
    new_shift = z.reshape(bsz, t_len, Z_MAIN_COLS)[:, -1, :A_SHIFT_COLS]
    return y, new_shift, new_wkv


def _log_sigmoid(x):
    return jnp.minimum(x, 0.0) - jnp.log(1.0 + jnp.exp(-jnp.abs(x)))


def _mlstm_kernel(*refs, chunk, t_blk, bb):
    zq_refs = refs[0:2 * B_HEADS]
    v_refs = refs[2 * B_HEADS:3 * B_HEADS]
    o_refs = refs[3 * B_HEADS:4 * B_HEADS]
    (cw_ref, cb_ref, cbuf_ref, gc_ref, gr_ref, bias_r_ref, bias_c_ref, ng_ref, c0_ref, n0_ref, m0_ref,
     y_ref, c_ref, n_ref, m_ref, tail_ref) = refs[4 * B_HEADS:]
    n_tail = B_CONV - 1

    def pad_rows(x):
        if t_blk == chunk:
            return x
        return jnp.concatenate([x, jnp.zeros((chunk - t_blk, x.shape[1]), x.dtype)], axis=0)

    @pl.when(pl.program_id(1) == 0)
    def _():
        c_ref[...] = c0_ref[...]
        n_ref[...] = n0_ref[...]
        m_ref[...] = m0_ref[...]
        tail_ref[...] = jnp.zeros_like(tail_ref)
        for bi in range(bb):
            tail_ref[(bi + 1) * SUBLANES - n_tail:(bi + 1) * SUBLANES, :] = cbuf_ref[bi]

    def conv_silu(cb, bi):
        cols = slice(cb * B_DQK, (cb + 1) * B_DQK)
        blk = pad_rows(zq_refs[cb][bi])
        zp = jnp.concatenate([tail_ref[bi * SUBLANES:(bi + 1) * SUBLANES, cols], blk], axis=0)
        acc = cb_ref[:, cols]
        for w in range(B_CONV):
            off = SUBLANES - n_tail + w
            acc = acc + zp[off:off + chunk, :] * cw_ref[w:w + 1, cols]
        tail_ref[(bi + 1) * SUBLANES - n_tail:(bi + 1) * SUBLANES, cols] = blk[chunk - n_tail:chunk, :]
        return acc * jax.nn.sigmoid(acc)

    ti = lax.broadcasted_iota(jnp.int32, (chunk, chunk), 0)
    si = lax.broadcasted_iota(jnp.int32, (chunk, chunk), 1)
    tri = si <= ti
    nt = (((1,), (1,)), ((), ()))
    tn = (((0,), (0,)), ((), ()))
    def head_chain(bi, h):
        gcol = gc_ref[bi] + bias_r_ref[...]
        grow = gr_ref[bi, 0] + bias_c_ref[...]
        q = conv_silu(h, bi)
        k = conv_silu(B_HEADS + h, bi) * (B_DQK ** -0.5)
        v = pad_rows(v_refs[h][bi])
        yield
        i_col, f_col = gcol[:, h:h + 1], gcol[:, B_HEADS + h:B_HEADS + h + 1]
        i_row, f_row = grow[h:h + 1, :], grow[B_HEADS + h:B_HEADS + h + 1, :]
        lf_col, lf_row = _log_sigmoid(f_col), _log_sigmoid(f_row)
        b_col = jnp.sum(jnp.where(tri, lf_row, 0.0), axis=1, keepdims=True)
        b_row = jnp.sum(jnp.where(ti <= si, lf_col, 0.0), axis=0, keepdims=True)
        m_prev = m_ref[bi, h:h + 1, :]
        yield
        log_d = jnp.where(tri, b_col - b_row + i_row, -jnp.inf)
        inter = b_col + m_prev
        m_t = jnp.maximum(inter, jnp.max(log_d, axis=1, keepdims=True))
        qb, kb, vb = q.astype(bf16), k.astype(bf16), v.astype(bf16)
        yield
        s = lax.dot_general(qb, kb, nt, preferred_element_type=f32) * jnp.exp(log_d - m_t)
        w_inter = jnp.exp(inter - m_t)
        c_old = c_ref[bi, h]
        n_old = n_ref[bi, h:h + 1, :]
        yield
        num = (jnp.dot(s.astype(bf16), vb, preferred_element_type=f32)
               + w_inter * jnp.dot(qb, c_old.astype(bf16), preferred_element_type=f32))
        den = jnp.sum(s, axis=1, keepdims=True) + w_inter * jnp.sum(q * n_old, axis=1, keepdims=True)
        yield
        hh = num / jnp.maximum(jnp.abs(den), jnp.exp(-m_t))
        m_new = m_t[chunk - 1:chunk, :]
        b_last = b_col[chunk - 1:chunk, :]
        ws = jnp.exp(b_last - b_col + i_col - m_new)
        d_c = jnp.exp(b_last + m_prev - m_new)
        kw = k * ws
        yield
        c_ref[bi, h] = d_c * c_old + lax.dot_general(kw.astype(bf16), vb, tn, preferred_element_type=f32)
        n_ref[bi, h:h + 1, :] = d_c * n_old + jnp.sum(kw, axis=0, keepdims=True)
        m_ref[bi, h:h + 1, :] = m_new
        yield
        hn = hh * lax.rsqrt(jnp.mean(hh * hh, axis=1, keepdims=True) + NORM_EPS)
        y_ref[bi, :, h * B_DV:(h + 1) * B_DV] = ((hn * ng_ref[:, h * B_DV:(h + 1) * B_DV])[0:t_blk, :]
                                                 * jax.nn.sigmoid(o_refs[h][bi]))

    _interleave(*[head_chain(bi, h) for bi in range(bb) for h in range(B_HEADS)])


def _interleave(*gens):
    live = list(gens)
    while live:
        for gen in list(live):
            try:
                next(gen)
            except StopIteration:
                live.remove(gen)


MLSTM_BATCH = 4


def _mlstm(z3, conv_buf, gates_col, gates_row, p, c0, n0, m0, bsz, n_chunks, chunk):
    t_rows = z3.shape[1]
    t_blk = chunk if t_rows % chunk == 0 else t_rows
    assert t_blk == chunk or n_chunks == 1
    bb = MLSTM_BATCH if bsz % MLSTM_BATCH == 0 else 1
    qk_col0 = A_SHIFT_COLS // B_DQK
    v_col0 = (A_SHIFT_COLS + B_QK_COLS) // B_DV
    o_col0 = v_col0 + B_HEADS
    tok = lambda w, rows=t_blk: pl.BlockSpec((bb, rows, w), lambda b, c: (b, c, 0))
    zcol = lambda w, cb: pl.BlockSpec((bb, t_blk, w), lambda b, c: (b, c, cb))
    full = lambda a: pl.BlockSpec(a.shape, lambda b, c: (0,) * a.ndim)
    bias = jnp.concatenate([p['b_ig_bias'], p['b_fg_bias']])
    bias_r = jnp.pad(bias, (0, LANES - 2 * B_HEADS)).reshape(1, LANES)
    bias_c = bias.reshape(2 * B_HEADS, 1)
    ng = p['b_norm_g'].reshape(1, -1)
    cw, cb = p['b_conv_w'], p['b_conv_b'].reshape(1, -1)
    cspec = pl.BlockSpec((bb, B_HEADS, B_DQK, B_DV), lambda b, c: (b, 0, 0, 0))
    nspec = pl.BlockSpec((bb, B_HEADS, B_DQK), lambda b, c: (b, 0, 0))
    mspec = pl.BlockSpec((bb, B_HEADS, 1), lambda b, c: (b, 0, 0))
    return pl.pallas_call(
        functools.partial(_mlstm_kernel, chunk=chunk, t_blk=t_blk, bb=bb),
        grid=(bsz // bb, n_chunks),
        in_specs=[zcol(B_DQK, qk_col0 + j) for j in range(2 * B_HEADS)]
                 + [zcol(B_DV, v_col0 + h) for h in range(B_HEADS)]
                 + [zcol(B_DV, o_col0 + h) for h in range(B_HEADS)]
                 + [full(cw), full(cb), pl.BlockSpec((bb, B_CONV - 1, B_QK_COLS), lambda b, c: (b, 0, 0)),
                    tok(LANES, chunk), pl.BlockSpec((bb, 1, 2 * B_HEADS, chunk), lambda b, c: (b, c, 0, 0)),
                    full(bias_r), full(bias_c), full(ng), cspec, nspec, mspec],
        out_specs=[tok(B_HEADS * B_DV), cspec, nspec, mspec],
        out_shape=[_sds((bsz, t_rows, B_HEADS * B_DV)), _sds(c0.shape), _sds(n0.shape), _sds(m0.shape)],
        scratch_shapes=[pltpu.VMEM((bb * SUBLANES, B_QK_COLS), f32)],
        compiler_params=_params("arbitrary", "arbitrary"),
        name="mlstm",
    )(*([z3] * (4 * B_HEADS)), cw, cb, conv_buf, gates_col, gates_row, bias_r, bias_c, ng, c0, n0, m0)


def _mlstm_mix(z, gates, bsz, t_len, conv_buf, c0, n0, m0, p):
    n = bsz * t_len
    z3 = z.reshape(bsz, t_len, Z_MAIN_COLS)
    zqk = z3[:, :, A_SHIFT_COLS:A_SHIFT_COLS + B_QK_COLS]
    new_conv = jnp.concatenate([conv_buf, zqk[:, max(t_len - (B_CONV - 1), 0):]], axis=1)[:, -(B_CONV - 1):]
    chunk = B_CHUNK if t_len % B_CHUNK == 0 else SUBLANES
    t_pad = -(-t_len // chunk) * chunk
    g3 = gates.reshape(bsz, t_len, LANES)
    if t_pad != t_len:
        lane = jnp.arange(LANES)
        fill = jnp.where(lane < B_HEADS, -1e30, jnp.where(lane < 2 * B_HEADS, 1e30, 0.0)).astype(f32)
        g3 = jnp.concatenate([g3, jnp.broadcast_to(fill, (bsz, t_pad - t_len, LANES))], axis=1)
    n_chunks = t_pad // chunk
    gates_col = g3
    gates_row = g3[:, :, :2 * B_HEADS].reshape(bsz, n_chunks, chunk, 2 * B_HEADS).transpose(0, 1, 3, 2)
    y, c_new, n_new, m_new = _mlstm(z3, conv_buf, gates_col, gates_row, p, c0, n0, m0.reshape(bsz, B_HEADS, 1),
                                    bsz, n_chunks, chunk)
    y = y.reshape(n, -1)
    return y, new_conv, c_new, n_new, m_new.reshape(bsz, B_HEADS)


def _oproj_kernel(ya_ref, yb_ref, wt_ref, wb_ref, x_ref, ga_ref, o_ref):
    acc = jnp.dot(ya_ref[...].astype(bf16), wt_ref[...], preferred_element_type=f32)
    acc = acc + jnp.dot(yb_ref[...].astype(bf16), wb_ref[...], preferred_element_type=f32)
    o_ref[...] = x_ref[...] + ga_ref[0] * acc


def _oproj(ya, yb, w_out_bf, x, ga3, idx):
    n = x.shape[0]
    half = D_MODEL // 2
    tok = lambda w: pl.BlockSpec((TM, w), lambda i: (i, 0))
    return pl.pallas_call(
        _oproj_kernel,
        grid=(n // TM,),
        in_specs=[tok(half), tok(half),
                  pl.BlockSpec((half, D_MODEL), lambda i: (0, 0)), pl.BlockSpec((half, D_MODEL), lambda i: (1, 0)),
                  tok(D_MODEL), _mod_spec(ga3, idx)],
        out_specs=tok(D_MODEL),
        out_shape=_sds((n, D_MODEL)),
        compiler_params=_params("arbitrary"),
        name="out_proj",
    )(ya, yb, w_out_bf, w_out_bf, x, ga3)


def _route_kernel(q_ref, keys_ref, cnt_ref, e1_ref, r2_ref, e2_ref, v1_scr, v2_scr):
    q = q_ref[...]
    tm = q.shape[0]
    neg_inf = -jnp.inf
    iota_k = lax.broadcasted_iota(jnp.int32, (PEER_NKEYS, tm), 0).astype(f32)
    nt = (((1,), (1,)), ((), ()))
    ranks, es = [], []
    for half, vscr in enumerate((v1_scr, v2_scr)):
        qp = q[:, half * 128:(half + 1) * 128].astype(bf16)
        s = lax.dot_general(keys_ref[half, 0].astype(bf16), qp, nt, preferred_element_type=f32)
        rank = jnp.full((PEER_NKEYS, tm), float(PEER_TOPK), f32)
        work = s
        for a in range(PEER_TOPK):
            m = jnp.max(work, axis=0, keepdims=True)
            idx = jnp.min(jnp.where(work == m, iota_k, float(PEER_NKEYS)), axis=0, keepdims=True)
            sel = iota_k == idx
            rank = jnp.where(sel, float(a), rank)
            work = jnp.where(sel, neg_inf, work)
            vscr[a:a + 1, :] = m
        ranks.append(rank)
        es.append(jnp.exp(s - vscr[0:1, :]))
    v1, v2 = v1_scr[...], v2_scr[...]
    row8 = lax.broadcasted_iota(jnp.int32, (SUBLANES, tm), 0).astype(f32)
    row16 = lax.broadcasted_iota(jnp.int32, (PEER_TOPK, tm), 0).astype(f32)
    pieces, poss = [v1[0:1, :] + v2], [row16]
    for a in range(1, 8):
        nb = PEER_TOPK // (a + 1)
        pieces.append(jnp.where(row8 < float(nb), v1[a:a + 1, :] + v2[0:SUBLANES, :], neg_inf))
        poss.append(row8 + float(a * PEER_TOPK))
    pieces.append(v1[8:16, :] + v2[0:1, :])
    poss.append((row8 + 8.0) * float(PEER_TOPK))
    cand = jnp.concatenate(pieces, axis=0)
    pos = jnp.concatenate(poss, axis=0)
    cnt16 = jnp.zeros((PEER_TOPK, tm), f32)
    zsum = None
    best0 = None
    for j in range(PEER_TOPK):
        m = jnp.max(cand, axis=0, keepdims=True)
        pj = jnp.min(jnp.where(cand == m, pos, 1e9), axis=0, keepdims=True)
        cand = jnp.where(pos == pj, neg_inf, cand)
        if j == 0:
            best0 = m
            zsum = jnp.ones_like(m)
        else:
            zsum = zsum + jnp.exp(m - best0)
        a_j = jnp.floor(pj * (1.0 / PEER_TOPK))
        cnt16 = cnt16 + jnp.where(row16 == a_j, 1.0, 0.0)
    cnt_i = jnp.zeros((PEER_NKEYS, tm), f32)
    for a in range(PEER_TOPK):
        cnt_i = jnp.where(ranks[0] == float(a), cnt16[a:a + 1, :], cnt_i)
    cnt_ref[0] = cnt_i
    e1_ref[0] = es[0] / zsum
    r2_ref[0] = ranks[1].astype(bf16)
    e2_ref[0] = es[1].astype(bf16)


def _route(q, keys):
    n = q.shape[0]
    tm = min(TM_ROUTE, n)
    ospec = pl.BlockSpec((1, PEER_NKEYS, tm), lambda i, h: (h, 0, i))
    oshape = (PEER_HEADS, PEER_NKEYS, n)
    return pl.pallas_call(
        _route_kernel,
        grid=(n // tm, PEER_HEADS),
        in_specs=[pl.BlockSpec((tm, 256), lambda i, h: (i, h)),
                  pl.BlockSpec((2, 1, PEER_NKEYS, 128), lambda i, h: (0, h, 0, 0))],
        out_specs=[ospec] * 4,
        out_shape=[_sds(oshape), _sds(oshape), _sds(oshape, bf16), _sds(oshape, bf16)],
        scratch_shapes=[pltpu.VMEM((PEER_TOPK, tm), f32), pltpu.VMEM((PEER_TOPK, tm), f32)],
        compiler_params=_params("arbitrary", "arbitrary"),
        name="peer_route",
    )(q, keys)


def _gated_act(h, cnt_ref, e1_ref, r2_ref, e2_ref, row0):
    zero = jnp.zeros((), bf16)
    parts = []
    for q in range(TE_HALF // PEER_NKEYS):
        row = row0 + q
        hs = h[q * PEER_NKEYS:(q + 1) * PEER_NKEYS, :]
        gate = None
        for hd in range(PEER_HEADS):
            cnt_row = cnt_ref[hd, row:row + 1, :].astype(bf16)
            e1_row = e1_ref[hd, row:row + 1, :].astype(bf16)
            term = jnp.where(r2_ref[hd] < cnt_row, e2_ref[hd] * e1_row, zero)
            gate = term if gate is None else gate + term
        gelu = 0.5 * hs * (1.0 + lax.erf(hs * (2.0 ** -0.5)))
        parts.append(gelu.astype(bf16) * gate)
    return jnp.concatenate(parts, axis=0)


def _dense_kernel(hn_ref, u_ref, vt_ref, vtp_ref, cnt_ref, e1_ref, cntp_ref, e1p_ref, r2_ref, e2_ref, o_ref,
                  xt_scr, acc_scr, hb_scr, *, n_chunks):
    c = pl.program_id(1)

    @pl.when(c == 0)
    def _():
        xt_scr[...] = hn_ref[...].T
        acc_scr[...] = jnp.zeros_like(acc_scr)
        hb_scr[...] = jnp.zeros_like(hb_scr)

    sub = TE_HALF // PEER_NKEYS
    xt = xt_scr[...]
    h_a = jnp.dot(u_ref[0:TE_HALF, :], xt, preferred_element_type=f32)
    act_bp = _gated_act(hb_scr[...], cntp_ref, e1p_ref, r2_ref, e2_ref, sub)
    part = jnp.dot(vtp_ref[...], act_bp, preferred_element_type=f32)
    hb_scr[...] = jnp.dot(u_ref[TE_HALF:TE, :], xt, preferred_element_type=f32)
    act_a = _gated_act(h_a, cnt_ref, e1_ref, r2_ref, e2_ref, 0)
    part = jnp.dot(vt_ref[:, 0:TE_HALF], act_a, preferred_element_type=f32) + part
    acc_scr[...] += part

    @pl.when(c == n_chunks - 1)
    def _():
        act_b = _gated_act(hb_scr[...], cnt_ref, e1_ref, r2_ref, e2_ref, sub)
        total = acc_scr[...] + jnp.dot(vt_ref[:, TE_HALF:TE], act_b, preferred_element_type=f32)
        o_ref[...] = total.T


def _peer_dense(hn, u_bf, vt_bf, cnt, e1, r2, e2):
    n = hn.shape[0]
    n_exp = u_bf.shape[0]
    n_chunks = n_exp // TE
    sub = TE // PEER_NKEYS
    prev = lambda c: jnp.maximum(c - 1, 0)
    return pl.pallas_call(
        functools.partial(_dense_kernel, n_chunks=n_chunks),
        grid=(n // TM, n_chunks),
        in_specs=[pl.BlockSpec((TM, D_MODEL), lambda i, c: (i, 0)),
                  pl.BlockSpec((TE, D_MODEL), lambda i, c: (c, 0)),
                  pl.BlockSpec((D_MODEL, TE), lambda i, c: (0, c)),
                  pl.BlockSpec((D_MODEL, TE_HALF), lambda i, c: (0, jnp.maximum(2 * c - 1, 0))),
                  pl.BlockSpec((PEER_HEADS, sub, TM), lambda i, c: (0, c, i)),
                  pl.BlockSpec((PEER_HEADS, sub, TM), lambda i, c: (0, c, i)),
                  pl.BlockSpec((PEER_HEADS, sub, TM), lambda i, c: (0, prev(c), i)),
                  pl.BlockSpec((PEER_HEADS, sub, TM), lambda i, c: (0, prev(c), i)),
                  pl.BlockSpec((PEER_HEADS, PEER_NKEYS, TM), lambda i, c: (0, 0, i)),
                  pl.BlockSpec((PEER_HEADS, PEER_NKEYS, TM), lambda i, c: (0, 0, i))],
        out_specs=pl.BlockSpec((TM, D_MODEL), lambda i, c: (i, 0)),
        out_shape=_sds((n, D_MODEL)),
        scratch_shapes=[pltpu.VMEM((D_MODEL, TM), bf16), pltpu.VMEM((D_MODEL, TM), f32),
                        pltpu.VMEM((TE_HALF, TM), f32)],
        compiler_params=_params("arbitrary", "arbitrary"),
        name="peer_dense",
    )(hn, u_bf, vt_bf, vt_bf, cnt, e1, cnt, e1, r2, e2)


def _peer_weights_kernel(u_ref, v_ref, ub_ref, vt_ref):
    ub_ref[...] = u_ref[...].astype(bf16)
    vt_ref[...] = v_ref[...].astype(bf16).T


def _peer_weights(u, v):
    n_exp, te = u.shape[0], 512
    rows = pl.BlockSpec((te, D_MODEL), lambda i: (i, 0))
    return pl.pallas_call(
        _peer_weights_kernel,
        grid=(n_exp // te,),
        in_specs=[rows, rows],
        out_specs=[rows, pl.BlockSpec((D_MODEL, te), lambda i: (0, i))],
        out_shape=[_sds((n_exp, D_MODEL), bf16), _sds((D_MODEL, n_exp), bf16)],
        compiler_params=_params("arbitrary"),
        name="peer_weights",
    )(u, v)


def _final_kernel(x_ref, f_ref, gc_ref, g_ref, sc_ref, sh_ref, o_ref):
    x = x_ref[...] + gc_ref[0] * f_ref[...]
    y = x * lax.rsqrt(jnp.mean(x * x, axis=-1, keepdims=True) + NORM_EPS) * g_ref[...]
    o_ref[...] = y * (1.0 + sc_ref[0]) + sh_ref[0]


def _final(x, ffn, gc3, g, sc3, sh3, idx):
    n = x.shape[0]
    tok = pl.BlockSpec((TM, D_MODEL), lambda i: (i, 0))
    return pl.pallas_call(
        _final_kernel,
        grid=(n // TM,),
        in_specs=[tok, tok, _mod_spec(gc3, idx), pl.BlockSpec((1, D_MODEL), lambda i: (0, 0)),
                  _mod_spec(sc3, idx), _mod_spec(sh3, idx)],
        out_specs=tok, out_shape=_sds((n, D_MODEL)),
        compiler_params=_params("arbitrary"),
        name="final_norm",
    )(x, ffn, gc3, g.reshape(1, D_MODEL), sc3, sh3)


def _trunk(x, mod, mod_f, states, p, shared):
    bsz, t_len, _ = x.shape
    n = bsz * t_len
    shift_s, wkv_s, conv_s, c_s, n_s, m_s = states
    xf = x.reshape(n, D_MODEL)
    m3 = [_mod3(m, bsz, t_len, TM) for m in jnp.split(mod, 6, axis=-1)]
    idx = m3[0][1]
    sh_a, sc_a, g_a, sh_c, sc_c, g_c = (a for a, _ in m3)
    (shift_f, _), (scale_f, _) = (_mod3(m, bsz, t_len, TM) for m in jnp.split(mod_f, 2, axis=-1))

    z, gates = _norm_mod_matmul(xf, p['norm_mix_g'], sc_a, sh_a, idx, shared['w_in_bf'], Z_MAIN_COLS, 3200,
                                w_gate=shared['w_gate'], name="in_proj")
    y_a, new_shift, new_wkv = _rwkv_mix(z, bsz, t_len, shift_s, wkv_s, p, shared['wwa'], shared['g1'],
                                        shared['g1t'])
    y_b, new_conv, new_c, new_n, new_m = _mlstm_mix(z, gates, bsz, t_len, conv_s, c_s, n_s, m_s, p)
    x1 = _oproj(y_a, y_b, shared['w_out_bf'], xf, g_a, idx)
    q, hn = _norm_mod_matmul(x1, p['norm_ffn_g'], sc_c, sh_c, idx, shared['wq_bf'], D_MODEL, D_MODEL,
                             emit_hn=True, name="peer_query")
    cnt, e1, r2, e2 = _route(q, p['peer_keys'])
    ffn = _peer_dense(hn, shared['u_bf'], shared['vt_bf'], cnt, e1, r2, e2)
    y = _final(x1, ffn, g_c, p['norm_final_g'], scale_f, shift_f, idx)
    return y.reshape(bsz, t_len, D_MODEL), (new_shift, new_wkv, new_conv, new_c, new_n, new_m)


def kernel(x_prompt, x_sample, c_prompt, c_sample, state_rwkv_shift, state_rwkv_wkv, state_mlstm_conv,
           state_mlstm_C, state_mlstm_n, state_mlstm_m, w_ada, b_ada, norm_mix_g, w_in, a_mu, a_w0, a_w2,
           a_a0, a_a2, a_g2, a_kk, a_ka, a_rk, a_lnw, a_lnb, b_conv_w, b_conv_b, b_ig_bias, b_fg_bias,
           b_norm_g, w_out, norm_ffn_g, peer_wq, peer_keys, peer_u, peer_v, w_ada_final, b_ada_final,
           norm_final_g):
    layer = dict(w_ada=w_ada, b_ada=b_ada, norm_mix_g=norm_mix_g, w_in=w_in, a_mu=a_mu, a_w0=a_w0, a_w2=a_w2,
                 a_a0=a_a0, a_a2=a_a2, a_g2=a_g2, a_kk=a_kk, a_ka=a_ka, a_rk=a_rk, a_lnw=a_lnw, a_lnb=a_lnb,
                 b_conv_w=b_conv_w, b_conv_b=b_conv_b, b_ig_bias=b_ig_bias, b_fg_bias=b_fg_bias,
                 b_norm_g=b_norm_g, w_out=w_out, norm_ffn_g=norm_ffn_g, peer_wq=peer_wq, peer_keys=peer_keys,
                 peer_u=peer_u, peer_v=peer_v)
    assert w_ada.shape[0] == 1, "single-layer trunk"
    p = {k: v[0] for k, v in layer.items()}
    p['norm_final_g'] = norm_final_g
    bp, bs = x_prompt.shape[0], x_sample.shape[0]

    c_all = jnp.concatenate([c_prompt, c_sample], axis=0)
    c_all = jnp.pad(c_all, ((0, -c_all.shape[0] % SUBLANES), (0, 0)))
    mod = _ada(c_all, p['w_ada'], p['b_ada'])
    mod_f = _ada(c_all, w_ada_final, b_ada_final)

    head_of = jnp.arange(MIX_A) // A_HEAD
    g1 = (head_of[:, None] == jnp.arange(LANES)[None, :]).astype(bf16)
    zero = jnp.zeros((64, MIX_A), f32)
    wwa = jnp.concatenate([jnp.concatenate([p['a_w2'], zero], axis=1),
                           jnp.concatenate([zero, p['a_a2']], axis=1)], axis=0).astype(bf16)
    w_in_bf = p['w_in'].astype(bf16)
    shared = dict(
        g1=g1, g1t=g1.T, wwa=wwa, w_in_bf=w_in_bf,
        w_gate=jnp.pad(w_in_bf[:, Z_MAIN_COLS:], ((0, 0), (0, LANES - 2 * B_HEADS))),
        w_out_bf=p['w_out'].astype(bf16), wq_bf=p['peer_wq'].astype(bf16))
    shared['u_bf'], shared['vt_bf'] = _peer_weights(p['peer_u'], p['peer_v'])

    dt = x_prompt.dtype
    zero_states = (jnp.zeros((bp, A_SHIFT_COLS), dt), jnp.zeros((bp, A_HEADS, A_HEAD, A_HEAD), dt),
                   jnp.zeros((bp, B_CONV - 1, B_QK_COLS), dt), jnp.zeros((bp, B_HEADS, B_DQK, B_DV), dt),
                   jnp.zeros((bp, B_HEADS, B_DQK), dt), jnp.zeros((bp, B_HEADS), dt))
    y_prompt, p_states = _trunk(x_prompt, mod[:bp], mod_f[:bp], zero_states, p, shared)
    s_states_in = (state_rwkv_shift[0], state_rwkv_wkv[0], state_mlstm_conv[0], state_mlstm_C[0],
                   state_mlstm_n[0], state_mlstm_m[0])
    y_sample, s_states = _trunk(x_sample, mod[bp:bp + bs], mod_f[bp:bp + bs], s_states_in, p, shared)
    return (y_prompt, y_sample) + tuple(s[None] for s in p_states) + tuple(s[None] for s in s_states)
```

```python
import functools

import jax
import jax.numpy as jnp
from jax import lax
from jax.experimental import pallas as pl
from jax.experimental.pallas import tpu as pltpu

f32 = jnp.float32
bf16 = jnp.bfloat16

D_MODEL = 2048
MIX_A = 1024
A_HEAD = 64
A_HEADS = 16
A_SHIFT_COLS = 3 * MIX_A + 64 + 64 + 128
A_GN_EPS = A_HEAD * 1e-5
B_QK_COLS = 1024
B_HEADS = 4
B_DV = 256
B_DQK = 128
B_CONV = 4
B_CHUNK = 64
Z_MAIN_COLS = A_SHIFT_COLS + B_QK_COLS + 2 * 1024
PEER_HEADS = 8
PEER_NKEYS = 128
PEER_TOPK = 16
NORM_EPS = 1e-6

LANES = 128
SUBLANES = 8
VMEM_LIMIT_BYTES = 56 * 1024 * 1024

TM = 512
TM_EW = 256
TM_ROUTE = 1024
TE = 1024
TE_HALF = TE // 2


def _params(*sem):
    return pltpu.CompilerParams(dimension_semantics=sem, vmem_limit_bytes=VMEM_LIMIT_BYTES)


def _sds(shape, dtype=f32):
    return jax.ShapeDtypeStruct(shape, dtype)


def _ada_kernel(c_ref, w_ref, b_ref, o_ref):
    c = c_ref[...]
    a = (c * jax.nn.sigmoid(c)).astype(bf16)
    o_ref[...] = jnp.dot(a, w_ref[...].astype(bf16), preferred_element_type=f32) + b_ref[...]


def _ada(c, w, b):
    m, n_out, tn = c.shape[0], w.shape[1], 1024
    return pl.pallas_call(
        _ada_kernel,
        grid=(n_out // tn,),
        in_specs=[pl.BlockSpec((m, D_MODEL), lambda j: (0, 0)),
                  pl.BlockSpec((D_MODEL, tn), lambda j: (0, j)),
                  pl.BlockSpec((1, tn), lambda j: (0, j))],
        out_specs=pl.BlockSpec((m, tn), lambda j: (0, j)),
        out_shape=_sds((m, n_out)),
        compiler_params=_params("arbitrary"),
        name="ada_mod",
    )(c, w, b.reshape(1, n_out))


def _mod3(m, bsz, t_len, tm):
    if t_len % tm == 0:
        per = t_len // tm
        return m.reshape(bsz, 1, D_MODEL), (lambda i: i // per)
    n = bsz * t_len
    assert tm % t_len == 0 and n % tm == 0
    return jnp.repeat(m, t_len, axis=0).reshape(n // tm, tm, D_MODEL), (lambda i: i)


def _mod_spec(arr, idx):
    return pl.BlockSpec((1, arr.shape[1], D_MODEL), lambda i, *_: (idx(i), 0, 0))


def _nmm_kernel(*refs, has_gate, emit_hn):
    it = iter(refs)
    x_ref, g_ref, sc_ref, sh_ref, w_ref = (next(it) for _ in range(5))
    wg_ref = next(it) if has_gate else None
    o_ref = next(it)
    og_ref = next(it) if has_gate else None
    hn_ref = next(it) if emit_hn else None
    hs_ref = next(it)

    @pl.when(pl.program_id(1) == 0)
    def _():
        x = x_ref[...]
        y = x * lax.rsqrt(jnp.mean(x * x, axis=-1, keepdims=True) + NORM_EPS) * g_ref[...]
        h = (y * (1.0 + sc_ref[0]) + sh_ref[0]).astype(bf16)
        hs_ref[...] = h
        if emit_hn:
            hn_ref[...] = h
        if has_gate:
            og_ref[...] = jnp.dot(h, wg_ref[...], preferred_element_type=f32)

    o_ref[...] = jnp.dot(hs_ref[...], w_ref[...], preferred_element_type=f32)


def _norm_mod_matmul(x, g, sc3, sh3, idx, w_bf, n_out, tn, w_gate=None, emit_hn=False, name="nmm"):
    n = x.shape[0]
    has_gate = w_gate is not None
    in_specs = [pl.BlockSpec((TM, D_MODEL), lambda i, j: (i, 0)),
                pl.BlockSpec((1, D_MODEL), lambda i, j: (0, 0)),
                _mod_spec(sc3, idx), _mod_spec(sh3, idx),
                pl.BlockSpec((D_MODEL, tn), lambda i, j: (0, j))]
    args = [x, g.reshape(1, D_MODEL), sc3, sh3, w_bf]
    out_specs = [pl.BlockSpec((TM, tn), lambda i, j: (i, j))]
    out_shape = [_sds((n, n_out))]
    if has_gate:
        in_specs.append(pl.BlockSpec((D_MODEL, LANES), lambda i, j: (0, 0)))
        args.append(w_gate)
        out_specs.append(pl.BlockSpec((TM, LANES), lambda i, j: (i, 0)))
        out_shape.append(_sds((n, LANES)))
    if emit_hn:
        out_specs.append(pl.BlockSpec((TM, D_MODEL), lambda i, j: (i, 0)))
        out_shape.append(_sds((n, D_MODEL), bf16))
    return pl.pallas_call(
        functools.partial(_nmm_kernel, has_gate=has_gate, emit_hn=emit_hn),
        grid=(n // TM, n_out // tn),
        in_specs=in_specs, out_specs=out_specs, out_shape=out_shape,
        scratch_shapes=[pltpu.VMEM((TM, D_MODEL), bf16)],
        compiler_params=_params("arbitrary", "arbitrary"),
        name=name,
    )(*args)


def _split_dot(x, w):
    hi = x.astype(bf16)
    lo = (x - hi.astype(f32)).astype(bf16)
    return (jnp.dot(hi, w, preferred_element_type=f32) + jnp.dot(lo, w, preferred_element_type=f32))


def _head_sum(x, g1, g1t):
    return _split_dot(_split_dot(x, g1), g1t)


def _rwkv_prep_kernel(z_ref, first_ref, mu_ref, w0_ref, wwa_ref, a0_ref, g2_ref, akk_ref, aka_ref, rk_ref,
                      g1_ref, g1t_ref, r_ref, k_ref, v_ref, d_ref, kk_ref, kka_ref, g_ref, bonus_ref,
                      *scratch, t_len, carry_mode):
    z = z_ref[...]
    tm = z.shape[0]
    row = lax.broadcasted_iota(jnp.int32, z.shape, 0)
    rolled = pltpu.roll(z, 1, axis=0)
    if carry_mode:
        carry_ref, = scratch
        first = jnp.where(pl.program_id(1) == 0, first_ref[0], carry_ref[0:1, :])
        prev = jnp.where(row == 0, first, rolled)
        carry_ref[0:1, :] = z[tm - 1:tm, :]
    else:
        prev = jnp.where(lax.rem(row, t_len) == 0, first_ref[...], rolled)
    zs = z + (prev - z) * mu_ref[...]
    r = zs[:, 0:MIX_A]
    k = zs[:, MIX_A:2 * MIX_A]
    v = zs[:, 2 * MIX_A:3 * MIX_A]
    xwa = zs[:, 3 * MIX_A:3 * MIX_A + 128]
    xg = zs[:, 3 * MIX_A + 128:3 * MIX_A + 256]
    lane = lax.broadcasted_iota(jnp.int32, xwa.shape, 1)
    xwa = jnp.where(lane < 64, jnp.tanh(xwa), xwa)
    lwa = jnp.dot(xwa.astype(bf16), wwa_ref[...], preferred_element_type=f32)
    u = w0_ref[...] + lwa[:, 0:MIX_A]
    w_log = -(jnp.maximum(-u, 0.0) + jnp.log(1.0 + jnp.exp(-jnp.abs(u)))) - 0.5
    decay = jnp.exp(-jnp.exp(w_log))
    a = jax.nn.sigmoid(a0_ref[...] + lwa[:, MIX_A:2 * MIX_A])
    g_ref[...] = jnp.dot(jax.nn.sigmoid(xg).astype(bf16), g2_ref[...].astype(bf16), preferred_element_type=f32)
    g1, g1t = g1_ref[...], g1t_ref[...]
    kk = k * akk_ref[...]
    nrm = jnp.sqrt(_head_sum(kk * kk, g1, g1t))
    kk = kk / jnp.maximum(nrm, 1e-12)
    k = k * (1.0 + (a - 1.0) * aka_ref[...])
    bonus_ref[...] = _head_sum(r * k * rk_ref[...], g1, g1t) * v
    for ref, val in ((r_ref, r), (k_ref, k), (v_ref, v), (d_ref, decay), (kk_ref, kk), (kka_ref, -(kk * a))):
        if carry_mode:
            ref[0] = val.T
        else:
            ref[...] = val


def _rwkv_prep(z, shift_buf, bsz, t_len, p, wwa, g1, g1t):
    n = bsz * t_len
    row = lambda a: a.reshape(1, -1)
    consts = [row(p['a_mu']), row(p['a_w0']), wwa, row(p['a_a0']), p['a_g2'], row(p['a_kk']), row(p['a_ka']),
              row(p['a_rk']), g1, g1t]
    carry_mode = t_len % TM_EW == 0
    if carry_mode:
        per = t_len // TM_EW
        grid = (bsz, per)
        tok = lambda w: pl.BlockSpec((TM_EW, w), lambda b, i: (b * per + i, 0))
        first = shift_buf.reshape(bsz, 1, A_SHIFT_COLS)
        first_spec = pl.BlockSpec((1, 1, A_SHIFT_COLS), lambda b, i: (b, 0, 0))
        scratch = [pltpu.VMEM((SUBLANES, A_SHIFT_COLS), f32)]
        sem = ("arbitrary", "arbitrary")
        scan_spec = pl.BlockSpec((1, MIX_A, TM_EW), lambda b, i: (b, 0, i))
        scan_shape = _sds((bsz, MIX_A, t_len))
    else:
        assert TM_EW % t_len == 0 and n % TM_EW == 0
        grid = (n // TM_EW,)
        tok = lambda w: pl.BlockSpec((TM_EW, w), lambda i: (i, 0))
        first = jnp.repeat(shift_buf, t_len, axis=0)
        first_spec = tok(A_SHIFT_COLS)
        scratch = []
        sem = ("arbitrary",)
        scan_spec, scan_shape = tok(MIX_A), _sds((n, MIX_A))
    full = lambda a: pl.BlockSpec(a.shape, lambda *_: (0,) * a.ndim)
    return pl.pallas_call(
        functools.partial(_rwkv_prep_kernel, t_len=t_len, carry_mode=carry_mode),
        grid=grid,
        in_specs=[tok(A_SHIFT_COLS), first_spec] + [full(a) for a in consts],
        out_specs=[scan_spec] * 6 + [tok(MIX_A)] * 2,
        out_shape=[scan_shape] * 6 + [_sds((n, MIX_A))] * 2,
        scratch_shapes=scratch,
        compiler_params=_params(*sem),
        name="rwkv_prep",
    )(z, first, *consts)


SCAN_CHUNK = 64
RELAYOUT_T = 128


def _rwkv_scan_kernel(r_ref, d_ref, k_ref, kk_ref, kka_ref, v_ref, s0_ref, y_ref, so_ref, *, n_rows, t_chunk):
    @pl.when(pl.program_id(1) == 0)
    def _():
        so_ref[...] = s0_ref[...]

    def step(t, gam):
        gam_new = gam * d_ref[t]
        inv = 1.0 / gam_new
        kk = kk_ref[t] * gam
        r = r_ref[t] * gam_new
        kka = kka_ref[t] * inv
        k = k_ref[t] * inv
        for i in range(n_rows):
            s_i = so_ref[i]
            sa = jnp.sum(s_i * kk, axis=0, keepdims=True)
            v_i = v_ref[t, pl.ds(i, 1), :]
            s_n = s_i + sa * kka + v_i * k
            so_ref[i] = s_n
            y_ref[t, pl.ds(i, 1), :] = jnp.sum(s_n * r, axis=0, keepdims=True)
        return gam_new

    gam = lax.fori_loop(0, t_chunk, step, jnp.ones((A_HEAD, LANES), f32))
    for i in range(n_rows):
        so_ref[i] = so_ref[i] * gam


def _rwkv_scan(r, d, k, kk, kka, v, s0, t_chunk):
    t_len, n_rows, ltot = v.shape
    jspec = pl.BlockSpec((t_chunk, A_HEAD, LANES), lambda l, t: (t, 0, l))
    vspec = pl.BlockSpec((t_chunk, n_rows, LANES), lambda l, t: (t, 0, l))
    sspec = pl.BlockSpec((n_rows, A_HEAD, LANES), lambda l, t: (0, 0, l))
    return pl.pallas_call(
        functools.partial(_rwkv_scan_kernel, n_rows=n_rows, t_chunk=t_chunk),
        grid=(ltot // LANES, t_len // t_chunk),
        in_specs=[jspec] * 5 + [vspec, sspec],
        out_specs=[vspec, sspec],
        out_shape=[_sds(v.shape), _sds((n_rows, A_HEAD, ltot))],
        compiler_params=_params("arbitrary", "arbitrary"),
        name="rwkv_scan",
    )(r, d, k, kk, kka, v, s0)


def _to_scan_kernel(x_ref, o_ref, *, bsz, pack, val_mode):
    n_idx = A_HEAD // pack if val_mode else A_HEAD
    for j0 in range(0, n_idx, SUBLANES):
        tiles = []
        for j in range(j0, j0 + SUBLANES):
            if val_mode:
                rows = [x_ref[b, pl.ds(g * n_idx + j, A_HEADS, stride=A_HEAD), :]
                        for g in range(pack) for b in range(bsz)]
            else:
                rows = [x_ref[b, pl.ds(j, A_HEADS, stride=A_HEAD), :] for b in range(bsz)] * pack
            tiles.append(jnp.concatenate(rows, axis=0).T)
        o_ref[:, j0:j0 + SUBLANES, :] = jnp.swapaxes(jnp.stack(tiles, axis=0), 0, 1)


def _to_scan_layout(a, bsz, t_len, pack, val_mode):
    per = t_len // RELAYOUT_T
    n_idx = A_HEAD // pack if val_mode else A_HEAD
    return pl.pallas_call(
        functools.partial(_to_scan_kernel, bsz=bsz, pack=pack, val_mode=val_mode),
        grid=(per,),
        in_specs=[pl.BlockSpec((bsz, MIX_A, RELAYOUT_T), lambda tb: (0, 0, tb))],
        out_specs=pl.BlockSpec((RELAYOUT_T, n_idx, LANES), lambda tb: (tb, 0, 0)),
        out_shape=_sds((t_len, n_idx, LANES)),
        compiler_params=_params("arbitrary"),
        name="to_scan_val" if val_mode else "to_scan_key",
    )(a)


def _group_norm_gate(y, bonus, g, lnw, lnb, g1, g1t):
    mu = _head_sum(y, g1, g1t) * (1.0 / A_HEAD)
    yc = y - mu
    var = _head_sum(yc * yc, g1, g1t) * (1.0 / A_HEAD)
    yn = yc * lax.rsqrt(var + A_GN_EPS) * lnw + lnb
    return (yn + bonus) * g


def _from_scan_kernel(y_ref, bonus_ref, g_ref, lnw_ref, lnb_ref, g1_ref, g1t_ref, o_ref, pt_ref, *, bsz, pack):
    n_idx = A_HEAD // pack
    for j0 in range(0, n_idx, SUBLANES):
        tiles = jnp.swapaxes(y_ref[:, j0:j0 + SUBLANES, :], 0, 1)
        for j in range(j0, j0 + SUBLANES):
            tt = tiles[j - j0].T
            for g in range(pack):
                for b in range(bsz):
                    lo = (g * bsz + b) * A_HEADS
                    pt_ref[b, pl.ds(g * n_idx + j, A_HEADS, stride=A_HEAD), :] = tt[lo:lo + A_HEADS, :]
    for b in range(bsz):
        o_ref[b] = _group_norm_gate(pt_ref[b].T, bonus_ref[b], g_ref[b], lnw_ref[...], lnb_ref[...],
                                    g1_ref[...], g1t_ref[...])


def _from_scan_layout(y, bonus, g, p, g1, g1t, bsz, t_len, pack):
    per = t_len // RELAYOUT_T
    n_idx = A_HEAD // pack
    tok = pl.BlockSpec((bsz, RELAYOUT_T, MIX_A), lambda tb: (0, tb, 0))
    full = lambda a: pl.BlockSpec(a.shape, lambda tb: (0,) * a.ndim)
    consts = [p['a_lnw'].reshape(1, -1), p['a_lnb'].reshape(1, -1), g1, g1t]
    out = pl.pallas_call(
        functools.partial(_from_scan_kernel, bsz=bsz, pack=pack),
        grid=(per,),
        in_specs=[pl.BlockSpec((RELAYOUT_T, n_idx, LANES), lambda tb: (tb, 0, 0)), tok, tok]
                 + [full(a) for a in consts],
        out_specs=tok,
        out_shape=_sds((bsz, t_len, MIX_A)),
        scratch_shapes=[pltpu.VMEM((bsz, MIX_A, RELAYOUT_T), f32)],
        compiler_params=_params("arbitrary"),
        name="from_scan",
    )(y, bonus.reshape(bsz, t_len, MIX_A), g.reshape(bsz, t_len, MIX_A), *consts)
    return out.reshape(bsz * t_len, MIX_A)


def _rwkv_post_kernel(y_ref, bonus_ref, g_ref, lnw_ref, lnb_ref, g1_ref, g1t_ref, o_ref):
    o_ref[...] = _group_norm_gate(y_ref[...], bonus_ref[...], g_ref[...], lnw_ref[...], lnb_ref[...],
                                  g1_ref[...], g1t_ref[...])


def _rwkv_post(y, bonus, g, p, g1, g1t):
    n = y.shape[0]
    tok = pl.BlockSpec((TM_EW, MIX_A), lambda i: (i, 0))
    full = lambda a: pl.BlockSpec(a.shape, lambda i: (0,) * a.ndim)
    consts = [p['a_lnw'].reshape(1, -1), p['a_lnb'].reshape(1, -1), g1, g1t]
    return pl.pallas_call(
        _rwkv_post_kernel,
        grid=(n // TM_EW,),
        in_specs=[tok] * 3 + [full(a) for a in consts],
        out_specs=tok, out_shape=_sds((n, MIX_A)),
        compiler_params=_params("arbitrary"),
        name="rwkv_post",
    )(y, bonus, g, *consts)


def _rwkv_mix(z, bsz, t_len, shift_buf, wkv0, p, wwa, g1, g1t):
    n = bsz * t_len
    r, k, v, d, kk, kka, g, bonus = _rwkv_prep(z, shift_buf, bsz, t_len, p, wwa, g1, g1t)

    bh = bsz * A_HEADS
    pack = LANES // bh if bh < LANES else 1
    n_rows = A_HEAD // pack

    s0 = wkv0.reshape(bsz, A_HEADS, pack, n_rows, A_HEAD).transpose(3, 4, 2, 0, 1).reshape(n_rows, A_HEAD, pack * bh)
    if pack * bh == LANES and t_len % RELAYOUT_T == 0:
        keys = [_to_scan_layout(a, bsz, t_len, pack, False) for a in (r, d, k, kk, kka)]
        y, s_out = _rwkv_scan(*keys, _to_scan_layout(v, bsz, t_len, pack, True), s0, SCAN_CHUNK)
        y = _from_scan_layout(y, bonus, g, p, g1, g1t, bsz, t_len, pack)
    else:
        assert pack == 1 and t_len <= SCAN_CHUNK

        def scan_rows(a):
            return a.reshape(bsz, t_len, A_HEADS, A_HEAD).transpose(1, 3, 0, 2).reshape(t_len, A_HEAD, bh)

        y, s_out = _rwkv_scan(*(scan_rows(a) for a in (r, d, k, kk, kka, v)), s0, t_len)
        y = y.reshape(t_len, A_HEAD, bsz, A_HEADS).transpose(2, 0, 3, 1).reshape(n, MIX_A)
        y = _rwkv_post(y, bonus, g, p, g1, g1t)
    new_wkv = s_out.reshape(n_rows, A_HEAD, pack, bsz, A_HEADS).transpose(3, 4, 2, 0, 1)
    new_wkv = new_wkv.reshape(bsz, A_HEADS, A_HEAD, A_HEAD)
    new_shift = z.reshape(bsz, t_len, Z_MAIN_COLS)[:, -1, :A_SHIFT_COLS]
    return y, new_shift, new_wkv


def _log_sigmoid(x):
    return jnp.minimum(x, 0.0) - jnp.log(1.0 + jnp.exp(-jnp.abs(x)))


def _mlstm_kernel(*refs, chunk, t_blk, bb):
    zq_refs = refs[0:2 * B_HEADS]
    v_refs = refs[2 * B_HEADS:3 * B_HEADS]
    o_refs = refs[3 * B_HEADS:4 * B_HEADS]
    (cw_ref, cb_ref, cbuf_ref, gc_ref, gr_ref, bias_r_ref, bias_c_ref, ng_ref, c0_ref, n0_ref, m0_ref,
     y_ref, c_ref, n_ref, m_ref, tail_ref) = refs[4 * B_HEADS:]
    n_tail = B_CONV - 1

    def pad_rows(x):
        if t_blk == chunk:
            return x
        return jnp.concatenate([x, jnp.zeros((chunk - t_blk, x.shape[1]), x.dtype)], axis=0)

    @pl.when(pl.program_id(1) == 0)
    def _():
        c_ref[...] = c0_ref[...]
        n_ref[...] = n0_ref[...]
        m_ref[...] = m0_ref[...]
        tail_ref[...] = jnp.zeros_like(tail_ref)
        for bi in range(bb):
            tail_ref[(bi + 1) * SUBLANES - n_tail:(bi + 1) * SUBLANES, :] = cbuf_ref[bi]

    def conv_silu(cb, bi):
        cols = slice(cb * B_DQK, (cb + 1) * B_DQK)
        blk = pad_rows(zq_refs[cb][bi])
        zp = jnp.concatenate([tail_ref[bi * SUBLANES:(bi + 1) * SUBLANES, cols], blk], axis=0)
        acc = cb_ref[:, cols]
        for w in range(B_CONV):
            off = SUBLANES - n_tail + w
            acc = acc + zp[off:off + chunk, :] * cw_ref[w:w + 1, cols]
        tail_ref[(bi + 1) * SUBLANES - n_tail:(bi + 1) * SUBLANES, cols] = blk[chunk - n_tail:chunk, :]
        return acc * jax.nn.sigmoid(acc)

    ti = lax.broadcasted_iota(jnp.int32, (chunk, chunk), 0)
    si = lax.broadcasted_iota(jnp.int32, (chunk, chunk), 1)
    tri = si <= ti
    nt = (((1,), (1,)), ((), ()))
    tn = (((0,), (0,)), ((), ()))
    def head_chain(bi, h):
        gcol = gc_ref[bi] + bias_r_ref[...]
        grow = gr_ref[bi, 0] + bias_c_ref[...]
        q = conv_silu(h, bi)
        k = conv_silu(B_HEADS + h, bi) * (B_DQK ** -0.5)
        v = pad_rows(v_refs[h][bi])
        yield
        i_col, f_col = gcol[:, h:h + 1], gcol[:, B_HEADS + h:B_HEADS + h + 1]
        i_row, f_row = grow[h:h + 1, :], grow[B_HEADS + h:B_HEADS + h + 1, :]
        lf_col, lf_row = _log_sigmoid(f_col), _log_sigmoid(f_row)
        b_col = jnp.sum(jnp.where(tri, lf_row, 0.0), axis=1, keepdims=True)
        b_row = jnp.sum(jnp.where(ti <= si, lf_col, 0.0), axis=0, keepdims=True)
        m_prev = m_ref[bi, h:h + 1, :]
        yield
        log_d = jnp.where(tri, b_col - b_row + i_row, -jnp.inf)
        inter = b_col + m_prev
        m_t = jnp.maximum(inter, jnp.max(log_d, axis=1, keepdims=True))
        qb, kb, vb = q.astype(bf16), k.astype(bf16), v.astype(bf16)
        yield
        s = lax.dot_general(qb, kb, nt, preferred_element_type=f32) * jnp.exp(log_d - m_t)
        w_inter = jnp.exp(inter - m_t)
        c_old = c_ref[bi, h]
        n_old = n_ref[bi, h:h + 1, :]
        yield
        num = (jnp.dot(s.astype(bf16), vb, preferred_element_type=f32)
               + w_inter * jnp.dot(qb, c_old.astype(bf16), preferred_element_type=f32))
        den = jnp.sum(s, axis=1, keepdims=True) + w_inter * jnp.sum(q * n_old, axis=1, keepdims=True)
        yield
        hh = num / jnp.maximum(jnp.abs(den), jnp.exp(-m_t))
        m_new = m_t[chunk - 1:chunk, :]
        b_last = b_col[chunk - 1:chunk, :]
        ws = jnp.exp(b_last - b_col + i_col - m_new)
        d_c = jnp.exp(b_last + m_prev - m_new)
        kw = k * ws
        yield
        c_ref[bi, h] = d_c * c_old + lax.dot_general(kw.astype(bf16), vb, tn, preferred_element_type=f32)
        n_ref[bi, h:h + 1, :] = d_c * n_old + jnp.sum(kw, axis=0, keepdims=True)
        m_ref[bi, h:h + 1, :] = m_new
        yield
        hn = hh * lax.rsqrt(jnp.mean(hh * hh, axis=1, keepdims=True) + NORM_EPS)
        y_ref[bi, :, h * B_DV:(h + 1) * B_DV] = ((hn * ng_ref[:, h * B_DV:(h + 1) * B_DV])[0:t_blk, :]
                                                 * jax.nn.sigmoid(o_refs[h][bi]))

    _interleave(*[head_chain(bi, h) for bi in range(bb) for h in range(B_HEADS)])


def _interleave(*gens):
    live = list(gens)
    while live:
        for gen in list(live):
            try:
                next(gen)
            except StopIteration:
                live.remove(gen)


MLSTM_BATCH = 4


def _mlstm(z3, conv_buf, gates_col, gates_row, p, c0, n0, m0, bsz, n_chunks, chunk):
    t_rows = z3.shape[1]
    t_blk = chunk if t_rows % chunk == 0 else t_rows
    assert t_blk == chunk or n_chunks == 1
    bb = MLSTM_BATCH if bsz % MLSTM_BATCH == 0 else 1
    qk_col0 = A_SHIFT_COLS // B_DQK
    v_col0 = (A_SHIFT_COLS + B_QK_COLS) // B_DV
    o_col0 = v_col0 + B_HEADS
    tok = lambda w, rows=t_blk: pl.BlockSpec((bb, rows, w), lambda b, c: (b, c, 0))
    zcol = lambda w, cb: pl.BlockSpec((bb, t_blk, w), lambda b, c: (b, c, cb))
    full = lambda a: pl.BlockSpec(a.shape, lambda b, c: (0,) * a.ndim)
    bias = jnp.concatenate([p['b_ig_bias'], p['b_fg_bias']])
    bias_r = jnp.pad(bias, (0, LANES - 2 * B_HEADS)).reshape(1, LANES)
    bias_c = bias.reshape(2 * B_HEADS, 1)
    ng = p['b_norm_g'].reshape(1, -1)
    cw, cb = p['b_conv_w'], p['b_conv_b'].reshape(1, -1)
    cspec = pl.BlockSpec((bb, B_HEADS, B_DQK, B_DV), lambda b, c: (b, 0, 0, 0))
    nspec = pl.BlockSpec((bb, B_HEADS, B_DQK), lambda b, c: (b, 0, 0))
    mspec = pl.BlockSpec((bb, B_HEADS, 1), lambda b, c: (b, 0, 0))
    return pl.pallas_call(
        functools.partial(_mlstm_kernel, chunk=chunk, t_blk=t_blk, bb=bb),
        grid=(bsz // bb, n_chunks),
        in_specs=[zcol(B_DQK, qk_col0 + j) for j in range(2 * B_HEADS)]
                 + [zcol(B_DV, v_col0 + h) for h in range(B_HEADS)]
                 + [zcol(B_DV, o_col0 + h) for h in range(B_HEADS)]
                 + [full(cw), full(cb), pl.BlockSpec((bb, B_CONV - 1, B_QK_COLS), lambda b, c: (b, 0, 0)),
                    tok(LANES, chunk), pl.BlockSpec((bb, 1, 2 * B_HEADS, chunk), lambda b, c: (b, c, 0, 0)),
                    full(bias_r), full(bias_c), full(ng), cspec, nspec, mspec],
        out_specs=[tok(B_HEADS * B_DV), cspec, nspec, mspec],
        out_shape=[_sds((bsz, t_rows, B_HEADS * B_DV)), _sds(c0.shape), _sds(n0.shape), _sds(m0.shape)],
        scratch_shapes=[pltpu.VMEM((bb * SUBLANES, B_QK_COLS), f32)],
        compiler_params=_params("arbitrary", "arbitrary"),
        name="mlstm",
    )(*([z3] * (4 * B_HEADS)), cw, cb, conv_buf, gates_col, gates_row, bias_r, bias_c, ng, c0, n0, m0)


def _mlstm_mix(z, gates, bsz, t_len, conv_buf, c0, n0, m0, p):
    n = bsz * t_len
    z3 = z.reshape(bsz, t_len, Z_MAIN_COLS)
    zqk = z3[:, :, A_SHIFT_COLS:A_SHIFT_COLS + B_QK_COLS]
    new_conv = jnp.concatenate([conv_buf, zqk[:, max(t_len - (B_CONV - 1), 0):]], axis=1)[:, -(B_CONV - 1):]
    chunk = B_CHUNK if t_len % B_CHUNK == 0 else SUBLANES
    t_pad = -(-t_len // chunk) * chunk
    g3 = gates.reshape(bsz, t_len, LANES)
    if t_pad != t_len:
        lane = jnp.arange(LANES)
        fill = jnp.where(lane < B_HEADS, -1e30, jnp.where(lane < 2 * B_HEADS, 1e30, 0.0)).astype(f32)
        g3 = jnp.concatenate([g3, jnp.broadcast_to(fill, (bsz, t_pad - t_len, LANES))], axis=1)
    n_chunks = t_pad // chunk
    gates_col = g3
    gates_row = g3[:, :, :2 * B_HEADS].reshape(bsz, n_chunks, chunk, 2 * B_HEADS).transpose(0, 1, 3, 2)
    y, c_new, n_new, m_new = _mlstm(z3, conv_buf, gates_col, gates_row, p, c0, n0, m0.reshape(bsz, B_HEADS, 1),
                                    bsz, n_chunks, chunk)
    y = y.reshape(n, -1)
    return y, new_conv, c_new, n_new, m_new.reshape(bsz, B_HEADS)


def _oproj_kernel(ya_ref, yb_ref, wt_ref, wb_ref, x_ref, ga_ref, o_ref):
    acc = jnp.dot(ya_ref[...].astype(bf16), wt_ref[...], preferred_element_type=f32)
    acc = acc + jnp.dot(yb_ref[...].astype(bf16), wb_ref[...], preferred_element_type=f32)
    o_ref[...] = x_ref[...] + ga_ref[0] * acc


def _oproj(ya, yb, w_out_bf, x, ga3, idx):
    n = x.shape[0]
    half = D_MODEL // 2
    tok = lambda w: pl.BlockSpec((TM, w), lambda i: (i, 0))
    return pl.pallas_call(
        _oproj_kernel,
        grid=(n // TM,),
        in_specs=[tok(half), tok(half),
                  pl.BlockSpec((half, D_MODEL), lambda i: (0, 0)), pl.BlockSpec((half, D_MODEL), lambda i: (1, 0)),
                  tok(D_MODEL), _mod_spec(ga3, idx)],
        out_specs=tok(D_MODEL),
        out_shape=_sds((n, D_MODEL)),
        compiler_params=_params("arbitrary"),
        name="out_proj",
    )(ya, yb, w_out_bf, w_out_bf, x, ga3)


def _route_kernel(q_ref, keys_ref, cnt_ref, e1_ref, r2_ref, e2_ref, v1_scr, v2_scr):
    q = q_ref[...]
    tm = q.shape[0]
    neg_inf = -jnp.inf
    iota_k = lax.broadcasted_iota(jnp.int32, (PEER_NKEYS, tm), 0).astype(f32)
    nt = (((1,), (1,)), ((), ()))
    ranks, es = [], []
    for half, vscr in enumerate((v1_scr, v2_scr)):
        qp = q[:, half * 128:(half + 1) * 128].astype(bf16)
        s = lax.dot_general(keys_ref[half, 0].astype(bf16), qp, nt, preferred_element_type=f32)
        rank = jnp.full((PEER_NKEYS, tm), float(PEER_TOPK), f32)
        work = s
        for a in range(PEER_TOPK):
            m = jnp.max(work, axis=0, keepdims=True)
            idx = jnp.min(jnp.where(work == m, iota_k, float(PEER_NKEYS)), axis=0, keepdims=True)
            sel = iota_k == idx
            rank = jnp.where(sel, float(a), rank)
            work = jnp.where(sel, neg_inf, work)
            vscr[a:a + 1, :] = m
        ranks.append(rank)
        es.append(jnp.exp(s - vscr[0:1, :]))
    v1, v2 = v1_scr[...], v2_scr[...]
    row8 = lax.broadcasted_iota(jnp.int32, (SUBLANES, tm), 0).astype(f32)
    row16 = lax.broadcasted_iota(jnp.int32, (PEER_TOPK, tm), 0).astype(f32)
    pieces, poss = [v1[0:1, :] + v2], [row16]
    for a in range(1, 8):
        nb = PEER_TOPK // (a + 1)
        pieces.append(jnp.where(row8 < float(nb), v1[a:a + 1, :] + v2[0:SUBLANES, :], neg_inf))
        poss.append(row8 + float(a * PEER_TOPK))
    pieces.append(v1[8:16, :] + v2[0:1, :])
    poss.append((row8 + 8.0) * float(PEER_TOPK))
    cand = jnp.concatenate(pieces, axis=0)
    pos = jnp.concatenate(poss, axis=0)
    cnt16 = jnp.zeros((PEER_TOPK, tm), f32)
    zsum = None
    best0 = None
    for j in range(PEER_TOPK):
        m = jnp.max(cand, axis=0, keepdims=True)
        pj = jnp.min(jnp.where(cand == m, pos, 1e9), axis=0, keepdims=True)
        cand = jnp.where(pos == pj, neg_inf, cand)
        if j == 0:
            best0 = m
            zsum = jnp.ones_like(m)
        else:
            zsum = zsum + jnp.exp(m - best0)
        a_j = jnp.floor(pj * (1.0 / PEER_TOPK))
        cnt16 = cnt16 + jnp.where(row16 == a_j, 1.0, 0.0)
    cnt_i = jnp.zeros((PEER_NKEYS, tm), f32)
    for a in range(PEER_TOPK):
        cnt_i = jnp.where(ranks[0] == float(a), cnt16[a:a + 1, :], cnt_i)
    cnt_ref[0] = cnt_i
    e1_ref[0] = es[0] / zsum
    r2_ref[0] = ranks[1].astype(bf16)
    e2_ref[0] = es[1].astype(bf16)


def _route(q, keys):
    n = q.shape[0]
    tm = min(TM_ROUTE, n)
    ospec = pl.BlockSpec((1, PEER_NKEYS, tm), lambda i, h: (h, 0, i))
    oshape = (PEER_HEADS, PEER_NKEYS, n)
    return pl.pallas_call(
        _route_kernel,
        grid=(n // tm, PEER_HEADS),
        in_specs=[pl.BlockSpec((tm, 256), lambda i, h: (i, h)),
                  pl.BlockSpec((2, 1, PEER_NKEYS, 128), lambda i, h: (0, h, 0, 0))],
        out_specs=[ospec] * 4,
        out_shape=[_sds(oshape), _sds(oshape), _sds(oshape, bf16), _sds(oshape, bf16)],
        scratch_shapes=[pltpu.VMEM((PEER_TOPK, tm), f32), pltpu.VMEM((PEER_TOPK, tm), f32)],
        compiler_params=_params("arbitrary", "arbitrary"),
        name="peer_route",
    )(q, keys)


def _gated_act(h, cnt_ref, e1_ref, r2_ref, e2_ref, row0):
    zero = jnp.zeros((), bf16)
    parts = []
    for q in range(TE_HALF // PEER_NKEYS):
        row = row0 + q
        hs = h[q * PEER_NKEYS:(q + 1) * PEER_NKEYS, :]
        gate = None
        for hd in range(PEER_HEADS):
            cnt_row = cnt_ref[hd, row:row + 1, :].astype(bf16)
            e1_row = e1_ref[hd, row:row + 1, :].astype(bf16)
            term = jnp.where(r2_ref[hd] < cnt_row, e2_ref[hd] * e1_row, zero)
            gate = term if gate is None else gate + term
        gelu = 0.5 * hs * (1.0 + lax.erf(hs * (2.0 ** -0.5)))
        parts.append(gelu.astype(bf16) * gate)
    return jnp.concatenate(parts, axis=0)


def _dense_kernel(hn_ref, u_ref, vt_ref, vtp_ref, cnt_ref, e1_ref, cntp_ref, e1p_ref, r2_ref, e2_ref, o_ref,
                  xt_scr, acc_scr, hb_scr, *, n_chunks):
    c = pl.program_id(1)

    @pl.when(c == 0)
    def _():
        xt_scr[...] = hn_ref[...].T
        acc_scr[...] = jnp.zeros_like(acc_scr)
        hb_scr[...] = jnp.zeros_like(hb_scr)

    sub = TE_HALF // PEER_NKEYS
    xt = xt_scr[...]
    h_a = jnp.dot(u_ref[0:TE_HALF, :], xt, preferred_element_type=f32)
    act_bp = _gated_act(hb_scr[...], cntp_ref, e1p_ref, r2_ref, e2_ref, sub)
    part = jnp.dot(vtp_ref[...], act_bp, preferred_element_type=f32)
    hb_scr[...] = jnp.dot(u_ref[TE_HALF:TE, :], xt, preferred_element_type=f32)
    act_a = _gated_act(h_a, cnt_ref, e1_ref, r2_ref, e2_ref, 0)
    part = jnp.dot(vt_ref[:, 0:TE_HALF], act_a, preferred_element_type=f32) + part
    acc_scr[...] += part

    @pl.when(c == n_chunks - 1)
    def _():
        act_b = _gated_act(hb_scr[...], cnt_ref, e1_ref, r2_ref, e2_ref, sub)
        total = acc_scr[...] + jnp.dot(vt_ref[:, TE_HALF:TE], act_b, preferred_element_type=f32)
        o_ref[...] = total.T


def _peer_dense(hn, u_bf, vt_bf, cnt, e1, r2, e2):
    n = hn.shape[0]
    n_exp = u_bf.shape[0]
    n_chunks = n_exp // TE
    sub = TE // PEER_NKEYS
    prev = lambda c: jnp.maximum(c - 1, 0)
    return pl.pallas_call(
        functools.partial(_dense_kernel, n_chunks=n_chunks),
        grid=(n // TM, n_chunks),
        in_specs=[pl.BlockSpec((TM, D_MODEL), lambda i, c: (i, 0)),
                  pl.BlockSpec((TE, D_MODEL), lambda i, c: (c, 0)),
                  pl.BlockSpec((D_MODEL, TE), lambda i, c: (0, c)),
                  pl.BlockSpec((D_MODEL, TE_HALF), lambda i, c: (0, jnp.maximum(2 * c - 1, 0))),
                  pl.BlockSpec((PEER_HEADS, sub, TM), lambda i, c: (0, c, i)),
                  pl.BlockSpec((PEER_HEADS, sub, TM), lambda i, c: (0, c, i)),
                  pl.BlockSpec((PEER_HEADS, sub, TM), lambda i, c: (0, prev(c), i)),
                  pl.BlockSpec((PEER_HEADS, sub, TM), lambda i, c: (0, prev(c), i)),
                  pl.BlockSpec((PEER_HEADS, PEER_NKEYS, TM), lambda i, c: (0, 0, i)),
                  pl.BlockSpec((PEER_HEADS, PEER_NKEYS, TM), lambda i, c: (0, 0, i))],
        out_specs=pl.BlockSpec((TM, D_MODEL), lambda i, c: (i, 0)),
        out_shape=_sds((n, D_MODEL)),
        scratch_shapes=[pltpu.VMEM((D_MODEL, TM), bf16), pltpu.VMEM((D_MODEL, TM), f32),
                        pltpu.VMEM((TE_HALF, TM), f32)],
        compiler_params=_params("arbitrary", "arbitrary"),
        name="peer_dense",
    )(hn, u_bf, vt_bf, vt_bf, cnt, e1, cnt, e1, r2, e2)


def _peer_weights_kernel(u_ref, v_ref, ub_ref, vt_ref):
    ub_ref[...] = u_ref[...].astype(bf16)
    vt_ref[...] = v_ref[...].astype(bf16).T


def _peer_weights(u, v):
    n_exp, te = u.shape[0], 512
    rows = pl.BlockSpec((te, D_MODEL), lambda i: (i, 0))
    return pl.pallas_call(
        _peer_weights_kernel,
        grid=(n_exp // te,),
        in_specs=[rows, rows],
        out_specs=[rows, pl.BlockSpec((D_MODEL, te), lambda i: (0, i))],
        out_shape=[_sds((n_exp, D_MODEL), bf16), _sds((D_MODEL, n_exp), bf16)],
        compiler_params=_params("arbitrary"),
        name="peer_weights",
    )(u, v)


def _final_kernel(x_ref, f_ref, gc_ref, g_ref, sc_ref, sh_ref, o_ref):
    x = x_ref[...] + gc_ref[0] * f_ref[...]
    y = x * lax.rsqrt(jnp.mean(x * x, axis=-1, keepdims=True) + NORM_EPS) * g_ref[...]
    o_ref[...] = y * (1.0 + sc_ref[0]) + sh_ref[0]


def _final(x, ffn, gc3, g, sc3, sh3, idx):
    n = x.shape[0]
    tok = pl.BlockSpec((TM, D_MODEL), lambda i: (i, 0))
    return pl.pallas_call(
        _final_kernel,
        grid=(n // TM,),
        in_specs=[tok, tok, _mod_spec(gc3, idx), pl.BlockSpec((1, D_MODEL), lambda i: (0, 0)),
                  _mod_spec(sc3, idx), _mod_spec(sh3, idx)],
        out_specs=tok, out_shape=_sds((n, D_MODEL)),
        compiler_params=_params("arbitrary"),
        name="final_norm",
    )(x, ffn, gc3, g.reshape(1, D_MODEL), sc3, sh3)


def _trunk(x, mod, mod_f, states, p, shared):
    bsz, t_len, _ = x.shape
    n = bsz * t_len
    shift_s, wkv_s, conv_s, c_s, n_s, m_s = states
    xf = x.reshape(n, D_MODEL)
    m3 = [_mod3(m, bsz, t_len, TM) for m in jnp.split(mod, 6, axis=-1)]
    idx = m3[0][1]
    sh_a, sc_a, g_a, sh_c, sc_c, g_c = (a for a, _ in m3)
    (shift_f, _), (scale_f, _) = (_mod3(m, bsz, t_len, TM) for m in jnp.split(mod_f, 2, axis=-1))

    z, gates = _norm_mod_matmul(xf, p['norm_mix_g'], sc_a, sh_a, idx, shared['w_in_bf'], Z_MAIN_COLS, 3200,
                                w_gate=shared['w_gate'], name="in_proj")
    y_a, new_shift, new_wkv = _rwkv_mix(z, bsz, t_len, shift_s, wkv_s, p, shared['wwa'], shared['g1'],
                                        shared['g1t'])
    y_b, new_conv, new_c, new_n, new_m = _mlstm_mix(z, gates, bsz, t_len, conv_s, c_s, n_s, m_s, p)
    x1 = _oproj(y_a, y_b, shared['w_out_bf'], xf, g_a, idx)
    q, hn = _norm_mod_matmul(x1, p['norm_ffn_g'], sc_c, sh_c, idx, shared['wq_bf'], D_MODEL, D_MODEL,
                             emit_hn=True, name="peer_query")
    cnt, e1, r2, e2 = _route(q, p['peer_keys'])
    ffn = _peer_dense(hn, shared['u_bf'], shared['vt_bf'], cnt, e1, r2, e2)
    y = _final(x1, ffn, g_c, p['norm_final_g'], scale_f, shift_f, idx)
    return y.reshape(bsz, t_len, D_MODEL), (new_shift, new_wkv, new_conv, new_c, new_n, new_m)


def kernel(x_prompt, x_sample, c_prompt, c_sample, state_rwkv_shift, state_rwkv_wkv, state_mlstm_conv,
           state_mlstm_C, state_mlstm_n, state_mlstm_m, w_ada, b_ada, norm_mix_g, w_in, a_mu, a_w0, a_w2,
           a_a0, a_a2, a_g2, a_kk, a_ka, a_rk, a_lnw, a_lnb, b_conv_w, b_conv_b, b_ig_bias, b_fg_bias,
           b_norm_g, w_out, norm_ffn_g, peer_wq, peer_keys, peer_u, peer_v, w_ada_final, b_ada_final,
           norm_final_g):
    layer = dict(w_ada=w_ada, b_ada=b_ada, norm_mix_g=norm_mix_g, w_in=w_in, a_mu=a_mu, a_w0=a_w0, a_w2=a_w2,
                 a_a0=a_a0, a_a2=a_a2, a_g2=a_g2, a_kk=a_kk, a_ka=a_ka, a_rk=a_rk, a_lnw=a_lnw, a_lnb=a_lnb,
                 b_conv_w=b_conv_w, b_conv_b=b_conv_b, b_ig_bias=b_ig_bias, b_fg_bias=b_fg_bias,
                 b_norm_g=b_norm_g, w_out=w_out, norm_ffn_g=norm_ffn_g, peer_wq=peer_wq, peer_keys=peer_keys,
                 peer_u=peer_u, peer_v=peer_v)
    assert w_ada.shape[0] == 1, "single-layer trunk"
    p = {k: v[0] for k, v in layer.items()}
    p['norm_final_g'] = norm_final_g
    bp, bs = x_prompt.shape[0], x_sample.shape[0]

    c_all = jnp.concatenate([c_prompt, c_sample], axis=0)
    c_all = jnp.pad(c_all, ((0, -c_all.shape[0] % SUBLANES), (0, 0)))
    mod = _ada(c_all, p['w_ada'], p['b_ada'])
    mod_f = _ada(c_all, w_ada_final, b_ada_final)

    head_of = jnp.arange(MIX_A) // A_HEAD
    g1 = (head_of[:, None] == jnp.arange(LANES)[None, :]).astype(bf16)
    zero = jnp.zeros((64, MIX_A), f32)
    wwa = jnp.concatenate([jnp.concatenate([p['a_w2'], zero], axis=1),
                           jnp.concatenate([zero, p['a_a2']], axis=1)], axis=0).astype(bf16)
    w_in_bf = p['w_in'].astype(bf16)
    shared = dict(
        g1=g1, g1t=g1.T, wwa=wwa, w_in_bf=w_in_bf,
        w_gate=jnp.pad(w_in_bf[:, Z_MAIN_COLS:], ((0, 0), (0, LANES - 2 * B_HEADS))),
        w_out_bf=p['w_out'].astype(bf16), wq_bf=p['peer_wq'].astype(bf16))
    shared['u_bf'], shared['vt_bf'] = _peer_weights(p['peer_u'], p['peer_v'])

    dt = x_prompt.dtype
    zero_states = (jnp.zeros((bp, A_SHIFT_COLS), dt), jnp.zeros((bp, A_HEADS, A_HEAD, A_HEAD), dt),
                   jnp.zeros((bp, B_CONV - 1, B_QK_COLS), dt), jnp.zeros((bp, B_HEADS, B_DQK, B_DV), dt),
                   jnp.zeros((bp, B_HEADS, B_DQK), dt), jnp.zeros((bp, B_HEADS), dt))
    y_prompt, p_states = _trunk(x_prompt, mod[:bp], mod_f[:bp], zero_states, p, shared)
    s_states_in = (state_rwkv_shift[0], state_rwkv_wkv[0], state_mlstm_conv[0], state_mlstm_C[0],
                   state_mlstm_n[0], state_mlstm_m[0])
    y_sample, s_states = _trunk(x_sample, mod[bp:bp + bs], mod_f[bp:bp + bs], s_states_in, p, shared)
    return (y_prompt, y_sample) + tuple(s[None] for s in p_states) + tuple(s[None] for s in s_states)
```
